```python
import math
import jax
import jax.numpy as jnp
from jax import lax
import numpy as np

D_MODEL = 1024
BATCH = 16
SEQ = 256
DEPTH = 2
DEC_BATCH = 4
DEC_SEQ = 4096
PAST_LEN = 512

GRID_W = 64
Q_BLOCK = 128
EPS = 1e-6
ROPE_THETA = 10000.0
H_A = 4
DH_A = 32
H_B = 4
DH_B = 64
WIN_R = 8
WIN_C = 16
H_C = 4
Q_LORA = 256
KV_LORA = 128
NOPE_C = 32
ROPE_C = 32
DV_C = 64
CONV_CH = 256
CONV_K = 31
N_BRANCH = 4
N_GROUPS = 4
EXPERTS_PER_GROUP = 4
N_EXPERTS = 16
TOP_K = 2
D_EXPERT = 256

A_QK = H_A * 2 * DH_A
A_V = H_A * 2 * DH_A
B_W = H_B * DH_B
IN_SIZES = (A_QK, A_QK, A_V, B_W, B_W, B_W, Q_LORA, KV_LORA, ROPE_C, 2 * CONV_CH)
IN_COLS = 2 * A_QK + A_V + 3 * B_W + Q_LORA + KV_LORA + ROPE_C + 2 * CONV_CH

kernel_name = 'hybrid_diffusion_prefix_step'


def _in_split_points():
    pts, acc = [], 0
    for sz in IN_SIZES[:-1]:
        acc += sz
        pts.append(acc)
    return pts


def rms_norm(x, g):
    xf = x.astype(jnp.float32)
    y = xf * lax.rsqrt(jnp.mean(xf * xf, axis=-1, keepdims=True) + EPS)
    return (y * g.astype(jnp.float32)).astype(x.dtype)


def layer_norm(x, g, b):
    xf = x.astype(jnp.float32)
    mu = jnp.mean(xf, axis=-1, keepdims=True)
    var = jnp.mean(jnp.square(xf - mu), axis=-1, keepdims=True)
    y = (xf - mu) * lax.rsqrt(var + EPS)
    return (y * g.astype(jnp.float32) + b.astype(jnp.float32)).astype(x.dtype)


def rope2d_tables(n_tok, dim):
    h = dim // 4
    t = jnp.arange(n_tok, dtype=jnp.int32)
    rows = (t // GRID_W).astype(jnp.float32)
    cols = (t % GRID_W).astype(jnp.float32)
    inv = ROPE_THETA ** (-jnp.arange(h, dtype=jnp.float32) / h)
    ar = rows[:, None] * inv[None, :]
    ac = cols[:, None] * inv[None, :]
    ang = jnp.concatenate([ar, ar, ac, ac], axis=-1)
    return jnp.cos(ang), jnp.sin(ang)


def apply_rope2d(x, cos, sin):
    h = x.shape[-1] // 4
    xs = x.reshape(x.shape[:-1] + (2, 2, h))
    rot = jnp.stack([-xs[..., 1, :], xs[..., 0, :]], axis=-2).reshape(x.shape)
    out = x.astype(jnp.float32) * cos[:, None, :] + rot.astype(jnp.float32) * sin[:, None, :]
    return out.astype(x.dtype)


def map_query_blocks(fn, q):
    b, s = q.shape[:2]
    nb = s // Q_BLOCK
    qb = jnp.moveaxis(q.reshape((b, nb, Q_BLOCK) + q.shape[2:]), 1, 0)
    out = jnp.moveaxis(lax.map(fn, qb), 0, 1)
    return out.reshape((b, s) + out.shape[3:])


def softmax_attn(q, k, v, scale):
    s = jnp.einsum('bqhd,bkhd->bhqk', q, k).astype(jnp.float32) * scale
    p = jax.nn.softmax(s, axis=-1).astype(v.dtype)
    return jnp.einsum('bhqk,bkhe->bqhe', p, v)


def diff_attn(q, k, v, lam, scale):
    s = jnp.einsum('bqhmd,bkhmd->bhmqk', q, k).astype(jnp.float32) * scale
    p = jax.nn.softmax(s, axis=-1)
    w = (p[:, :, 0] - lam * p[:, :, 1]).astype(v.dtype)
    return jnp.einsum('bhqk,bkhe->bqhe', w, v)


def diff_lambda(lam_vecs, layer):
    lam_init = 0.8 - 0.6 * math.exp(-0.3 * layer)
    lv = lam_vecs.astype(jnp.float32)
    lam = jnp.exp(jnp.sum(lv[0] * lv[1])) - jnp.exp(jnp.sum(lv[2] * lv[3])) + lam_init
    return lam, lam_init


def neighborhood_attn(q, k, v, ck, cv, rpb):
    b, s, nh, d = q.shape
    rows = s // GRID_W
    kr = min(WIN_R, rows)
    n_cb = GRID_W // WIN_C
    kb_w = 2 * WIN_C
    qcol = np.arange(GRID_W).reshape(n_cb, WIN_C)
    kstart = np.clip(np.arange(n_cb) * WIN_C - WIN_C // 2, 0, GRID_W - kb_w)
    kcols = (kstart[:, None] + np.arange(kb_w)).astype(np.int32)
    cstart = np.clip(qcol - WIN_C // 2, 0, GRID_W - WIN_C)
    kc = kcols[:, None, :]
    col_mask = (kc >= cstart[..., None]) & (kc < cstart[..., None] + WIN_C)
    col_idx = np.clip(kc - qcol[..., None] + WIN_C - 1, 0, 2 * WIN_C - 2).astype(np.int32)
    mask = jnp.asarray(np.broadcast_to(col_mask[:, :, None, :], (n_cb, WIN_C, kr, kb_w)).reshape(n_cb, WIN_C, kr * kb_w))
    qg = q.reshape(b, rows, GRID_W, nh, d)
    kg = k.reshape(b, rows, GRID_W, nh, d)
    vg = v.reshape(b, rows, GRID_W, nh, d)
    n_ctx = ck.shape[1]
    scale = d ** -0.5

    def row_fn(r):
        rs = jnp.clip(r - kr // 2, 0, rows - kr)
        qr = lax.dynamic_index_in_dim(qg, r, axis=1, keepdims=False).reshape(b, n_cb, WIN_C, nh, d)
        kw = lax.dynamic_slice_in_dim(kg, rs, kr, axis=1)[:, :, kcols]
        vw = lax.dynamic_slice_in_dim(vg, rs, kr, axis=1)[:, :, kcols]
        kw = jnp.moveaxis(kw, 2, 1).reshape(b, n_cb, kr * kb_w, nh, d)
        vw = jnp.moveaxis(vw, 2, 1).reshape(b, n_cb, kr * kb_w, nh, d)
        row_idx = rs + jnp.arange(kr) - r + WIN_R - 1
        bias = rpb[:, row_idx[None, None, :, None], col_idx[:, :, None, :]]
        bias = bias.reshape(nh, n_cb, WIN_C, kr * kb_w).astype(jnp.float32)
        s_w = jnp.einsum('bjqhd,bjkhd->bhjqk', qr, kw).astype(jnp.float32) * scale + bias
        s_w = jnp.where(mask, s_w, -jnp.inf)
        s_c = jnp.einsum('bjqhd,bkhd->bhjqk', qr, ck).astype(jnp.float32) * scale
        p = jax.nn.softmax(jnp.concatenate([s_c, s_w], axis=-1), axis=-1).astype(v.dtype)
        o = (jnp.einsum('bhjqk,bkhd->bjqhd', p[..., :n_ctx], cv)
             + jnp.einsum('bhjqk,bjkhd->bjqhd', p[..., n_ctx:], vw))
        return o.reshape(b, GRID_W, nh, d)

    out = lax.map(row_fn, jnp.arange(rows))
    return jnp.moveaxis(out, 0, 1).reshape(b, s, nh, d)


def mla_keys(ckv_n, krope, wukv):
    b, t, _ = ckv_n.shape
    kv = (ckv_n @ wukv).reshape(b, t, H_C, NOPE_C + DV_C)
    k = jnp.concatenate([kv[..., :NOPE_C], jnp.broadcast_to(krope[:, :, None, :], (b, t, H_C, ROPE_C))], axis=-1)
    return k, kv[..., NOPE_C:]


def conformer_conv(u, lp):
    a, g = jnp.split(u, 2, axis=-1)
    z = a * jax.nn.sigmoid(g)
    w = lp['d_conv_w'].astype(z.dtype)[:, None, :]
    z = lax.conv_general_dilated(z, w, (1,), [(CONV_K // 2, CONV_K // 2)],
                                 dimension_numbers=('NWC', 'WIO', 'NWC'),
                                 feature_group_count=CONV_CH) + lp['d_conv_b']
    z = jax.nn.silu(layer_norm(z, lp['d_ln_g'], lp['d_ln_b']))
    return z @ lp['d_wo']


def hier_moe(h, lp):
    b, s, _ = h.shape
    lg = (h @ lp['moe_w_group'] + lp['moe_b_group']).astype(jnp.float32)
    pg = jax.nn.softmax(lg, axis=-1)
    gi = jnp.argmax(lg, axis=-1)
    gw = jnp.max(pg, axis=-1, keepdims=True)
    le = (h @ lp['moe_w_expert'] + lp['moe_b_expert']).astype(jnp.float32)
    le = le.reshape(b, s, N_GROUPS, EXPERTS_PER_GROUP)
    le_sel = jnp.einsum('bsg,bsge->bse', jax.nn.one_hot(gi, N_GROUPS, dtype=jnp.float32), le)
    tv, ti = lax.top_k(le_sel, TOP_K)
    tw = jax.nn.softmax(tv, axis=-1) * gw
    eid = gi[..., None] * EXPERTS_PER_GROUP + ti
    dense_w = jnp.sum(jax.nn.one_hot(eid, N_EXPERTS, dtype=jnp.float32) * tw[..., None], axis=-2)
    hid = jnp.einsum('bsd,edf->bsef', h, lp['moe_w1'])
    gate, up = jnp.split(hid, 2, axis=-1)
    act = jax.nn.silu(gate) * up * dense_w[..., None].astype(h.dtype)
    return jnp.einsum('bsef,efd->bsd', act, lp['moe_w2'])


def mixers(h, lp, layer, ctx):
    b, s, _ = h.shape
    aq, ak, av, bq, bk, bv, cq, ckv, ckr, du = jnp.split(h @ lp['w_in'], _in_split_points(), axis=-1)
    aq = aq.reshape(b, s, H_A, 2, DH_A)
    ak = ak.reshape(b, s, H_A, 2 * DH_A)
    av = av.reshape(b, s, H_A, 2 * DH_A)
    bq = bq.reshape(b, s, H_B, DH_B)
    bk = bk.reshape(b, s, H_B, DH_B)
    bv = bv.reshape(b, s, H_B, DH_B)
    qc = (rms_norm(cq, lp['c_q_norm_g']) @ lp['c_wuq']).reshape(b, s, H_C, NOPE_C + ROPE_C)
    ckv = rms_norm(ckv, lp['c_kv_norm_g'])
    own = (ak, av, bk, bv, ckv, ckr)
    lam, lam_init = diff_lambda(lp['a_lambda'], layer)
    sa = DH_A ** -0.5
    sb = DH_B ** -0.5
    sc = (NOPE_C + ROPE_C) ** -0.5
    if ctx is None:
        ka = ak.reshape(b, s, H_A, 2, DH_A)
        va = av
        ob = map_query_blocks(lambda q: softmax_attn(q, bk, bv, sb), bq)
        kc, vc = mla_keys(ckv, ckr, lp['c_wukv'])
    else:
        a_k_c, a_v_c, b_k_c, b_v_c, c_kv_c, c_kr_c = ctx
        cos_a, sin_a = rope2d_tables(s, DH_A)
        aq = apply_rope2d(aq.reshape(b, s, 2 * H_A, DH_A), cos_a, sin_a).reshape(b, s, H_A, 2, DH_A)
        ak_rot = apply_rope2d(ak.reshape(b, s, 2 * H_A, DH_A), cos_a, sin_a).reshape(b, s, H_A, 2 * DH_A)
        ka = jnp.concatenate([a_k_c, ak_rot], axis=1).reshape(b, -1, H_A, 2, DH_A)
        va = jnp.concatenate([a_v_c, av], axis=1)
        ob = neighborhood_attn(bq, bk, bv, b_k_c, b_v_c, lp['b_rpb'])
        cos_c, sin_c = rope2d_tables(s, ROPE_C)
        qc = jnp.concatenate([qc[..., :NOPE_C], apply_rope2d(qc[..., NOPE_C:], cos_c, sin_c)], axis=-1)
        kr_lat = apply_rope2d(ckr[:, :, None, :], cos_c, sin_c)[:, :, 0, :]
        kc_ctx, vc_ctx = mla_keys(c_kv_c, c_kr_c, lp['c_wukv'])
        kc_lat, vc_lat = mla_keys(ckv, kr_lat, lp['c_wukv'])
        kc = jnp.concatenate([kc_ctx, kc_lat], axis=1)
        vc = jnp.concatenate([vc_ctx, vc_lat], axis=1)
    oa = map_query_blocks(lambda q: diff_attn(q, ka, va, lam, sa), aq)
    oc = map_query_blocks(lambda q: softmax_attn(q, kc, vc, sc), qc)
    oa = (rms_norm(oa, lp['a_subln_g']) * (1.0 - lam_init)).reshape(b, s, A_V) @ lp['a_wo']
    ob = ob.reshape(b, s, B_W) @ lp['b_wo']
    oc = oc.reshape(b, s, H_C * DV_C) @ lp['c_wo']
    od = conformer_conv(du, lp)
    g = jax.nn.sigmoid(h @ lp['w_gate'] + lp['b_gate']).reshape(b, s, N_BRANCH, D_MODEL)
    merged = g[:, :, 0] * oa + g[:, :, 1] * ob + g[:, :, 2] * oc + g[:, :, 3] * od
    return merged @ lp['w_out'], own


def adaln(cond, w_ada, b_ada):
    m = jax.nn.silu(cond) @ w_ada + b_ada
    return jnp.split(m[:, None, :], 6, axis=-1)


def trunk_layer(x, cond, lp, layer, ctx):
    shift1, scale1, gate1, shift2, scale2, gate2 = adaln(cond, lp['w_ada'], lp['b_ada'])
    h = rms_norm(x, lp['norm1_g']) * (1 + scale1) + shift1
    y, own = mixers(h, lp, layer, ctx)
    x = x + gate1 * y
    h2 = rms_norm(x, lp['norm2_g']) * (1 + scale2) + shift2
    x = x + gate2 * hier_moe(h2, lp)
    return x, own


def setup_inputs(seed: int = 0) -> dict:
    key = jax.random.key(seed)
    keys = jax.random.split(key, 48)
    counter = [0]

    def nrm(shape, scale):
        k = keys[counter[0]]
        counter[0] += 1
        return jax.random.normal(k, shape, jnp.float32) * scale

    def gain(shape):
        return 1.0 + nrm(shape, 0.02)

    L = DEPTH
    D = D_MODEL
    return {
        'x_prompt': nrm((BATCH, SEQ, D), 1.0),
        'x_sample': nrm((DEC_BATCH, DEC_SEQ, D), 1.0),
        'cache_a_k': nrm((DEC_BATCH, DEPTH, PAST_LEN, H_A, 2 * DH_A), 1.0),
        'cache_a_v': nrm((DEC_BATCH, DEPTH, PAST_LEN, H_A, 2 * DH_A), 1.0),
        'cache_b_k': nrm((DEC_BATCH, DEPTH, PAST_LEN, H_B, DH_B), 1.0),
        'cache_b_v': nrm((DEC_BATCH, DEPTH, PAST_LEN, H_B, DH_B), 1.0),
        'cache_c_kv': nrm((DEC_BATCH, DEPTH, PAST_LEN, KV_LORA), 1.0),
        'cache_c_krope': nrm((DEC_BATCH, DEPTH, PAST_LEN, ROPE_C), 1.0),
        'c': nrm((DEC_BATCH, D), 1.0),
        'c_ctx': nrm((D,), 1.0),
        'norm1_g': gain((L, D)),
        'norm2_g': gain((L, D)),
        'w_ada': nrm((L, D, 6 * D), 0.5 * D ** -0.5),
        'b_ada': nrm((L, 6 * D), 0.02),
        'w_in': nrm((L, D, IN_COLS), D ** -0.5),
        'a_lambda': nrm((L, 4, DH_A), 0.1),
        'a_subln_g': gain((L, 2 * DH_A)),
        'a_wo': nrm((L, A_V, D), A_V ** -0.5),
        'b_rpb': nrm((L, H_B, 2 * WIN_R - 1, 2 * WIN_C - 1), 0.1),
        'b_wo': nrm((L, B_W, D), B_W ** -0.5),
        'c_q_norm_g': gain((L, Q_LORA)),
        'c_kv_norm_g': gain((L, KV_LORA)),
        'c_wuq': nrm((L, Q_LORA, H_C * (NOPE_C + ROPE_C)), Q_LORA ** -0.5),
        'c_wukv': nrm((L, KV_LORA, H_C * (NOPE_C + DV_C)), KV_LORA ** -0.5),
        'c_wo': nrm((L, H_C * DV_C, D), (H_C * DV_C) ** -0.5),
        'd_conv_w': nrm((L, CONV_K, CONV_CH), CONV_K ** -0.5),
        'd_conv_b': nrm((L, CONV_CH), 0.02),
        'd_ln_g': gain((L, CONV_CH)),
        'd_ln_b': nrm((L, CONV_CH), 0.02),
        'd_wo': nrm((L, CONV_CH, D), CONV_CH ** -0.5),
        'w_gate': nrm((L, D, N_BRANCH * D), D ** -0.5),
        'b_gate': nrm((L, N_BRANCH * D), 0.02),
        'w_out': nrm((L, D, D), D ** -0.5),
        'moe_w_group': nrm((L, D, N_GROUPS), D ** -0.5),
        'moe_b_group': nrm((L, N_GROUPS), 0.01),
        'moe_w_expert': nrm((L, D, N_EXPERTS), D ** -0.5),
        'moe_b_expert': nrm((L, N_EXPERTS), 0.01),
        'moe_w1': nrm((L, N_EXPERTS, D, 2 * D_EXPERT), D ** -0.5),
        'moe_w2': nrm((L, N_EXPERTS, D_EXPERT, D), D_EXPERT ** -0.5),
        'final_norm_g': gain((D,)),
    }


def reference(x_prompt, x_sample, cache_a_k, cache_a_v, cache_b_k, cache_b_v, cache_c_kv, cache_c_krope,
              c, c_ctx, norm1_g, norm2_g, w_ada, b_ada, w_in, a_lambda, a_subln_g, a_wo, b_rpb, b_wo,
              c_q_norm_g, c_kv_norm_g, c_wuq, c_wukv, c_wo, d_conv_w, d_conv_b, d_ln_g, d_ln_b, d_wo,
              w_gate, b_gate, w_out, moe_w_group, moe_b_group, moe_w_expert, moe_b_expert, moe_w1, moe_w2,
              final_norm_g):
    lps = [dict(norm1_g=norm1_g[l], norm2_g=norm2_g[l], w_ada=w_ada[l], b_ada=b_ada[l], w_in=w_in[l],
                a_lambda=a_lambda[l], a_subln_g=a_subln_g[l], a_wo=a_wo[l], b_rpb=b_rpb[l], b_wo=b_wo[l],
                c_q_norm_g=c_q_norm_g[l], c_kv_norm_g=c_kv_norm_g[l], c_wuq=c_wuq[l], c_wukv=c_wukv[l],
                c_wo=c_wo[l], d_conv_w=d_conv_w[l], d_conv_b=d_conv_b[l], d_ln_g=d_ln_g[l], d_ln_b=d_ln_b[l],
                d_wo=d_wo[l], w_gate=w_gate[l], b_gate=b_gate[l], w_out=w_out[l],
                moe_w_group=moe_w_group[l], moe_b_group=moe_b_group[l], moe_w_expert=moe_w_expert[l],
                moe_b_expert=moe_b_expert[l], moe_w1=moe_w1[l], moe_w2=moe_w2[l])
           for l in range(DEPTH)]

    xp = x_prompt
    states = []
    for l in range(DEPTH):
        xp, st = trunk_layer(xp, c_ctx[None, :], lps[l], l, None)
        states.append(st)
    y_prompt = rms_norm(xp, final_norm_g)

    xs = x_sample
    for l in range(DEPTH):
        ctx = (cache_a_k[:, l], cache_a_v[:, l], cache_b_k[:, l], cache_b_v[:, l],
               cache_c_kv[:, l], cache_c_krope[:, l])
        xs, _ = trunk_layer(xs, c, lps[l], l, ctx)
    y_sample = rms_norm(xs, final_norm_g)

    new_a_k = jnp.stack([st[0] for st in states], axis=1)
    new_a_v = jnp.stack([st[1] for st in states], axis=1)
    new_b_k = jnp.stack([st[2] for st in states], axis=1)
    new_b_v = jnp.stack([st[3] for st in states], axis=1)
    new_c_kv = jnp.stack([st[4] for st in states], axis=1)
    new_c_krope = jnp.stack([st[5] for st in states], axis=1)
    return (y_prompt, y_sample, new_a_k, new_a_v, new_b_k, new_b_v, new_c_kv, new_c_krope)
```

```python
import functools
import math

import numpy as np
import jax
import jax.numpy as jnp
from jax import lax
from jax.experimental import pallas as pl
from jax.experimental.pallas import tpu as pltpu

GRID_W = 64
EPS = 1e-6
ROPE_THETA = 10000.0
H_A, DH_A = 4, 32
H_B, DH_B = 4, 64
WIN_R, WIN_C = 8, 16
H_C, Q_LORA, KV_LORA, NOPE_C, ROPE_C, DV_C = 4, 256, 128, 32, 32, 64
CONV_CH, CONV_K = 256, 31
N_BRANCH = 4
N_GROUPS, EXPERTS_PER_GROUP, N_EXPERTS, D_EXPERT = 4, 4, 16, 256
HEAD_W = 256
LANES = 128
V7X_VMEM_BYTES = 64 * 1024 * 1024

_MM = jnp.bfloat16
_F32 = jnp.float32


def _cparams(sem, vmem_mb):
    return pltpu.CompilerParams(dimension_semantics=sem,
                                vmem_limit_bytes=min(vmem_mb * 1024 * 1024, V7X_VMEM_BYTES - 8 * 1024 * 1024))


def _dot(a, b):
    return jnp.dot(a.astype(_MM), b.astype(_MM), preferred_element_type=_F32)


def _dot_nt(a, b):
    return lax.dot_general(a.astype(_MM), b.astype(_MM), (((1,), (1,)), ((), ())),
                           preferred_element_type=_F32)


def _sigmoid(x):
    return 1.0 / (1.0 + jnp.exp(-x))


def _rms(x, g):
    return x * lax.rsqrt(jnp.mean(x * x, axis=-1, keepdims=True) + EPS) * g


def _rope(x, cos, sin):
    w = x.shape[-1]
    lane = lax.broadcasted_iota(jnp.int32, x.shape, 1)
    first = (lane % 16) < 8
    fwd = pltpu.roll(x, w - 8, axis=1)
    bwd = pltpu.roll(x, 8, axis=1)
    return x * cos + jnp.where(first, -fwd, bwd) * sin


def _adaln_kernel(cond_ref, w_ref, b_ref, o_ref):
    c = cond_ref[...]
    a = c * _sigmoid(c)
    o_ref[0] = _dot(a, w_ref[0]) + b_ref[0]


def _adaln(cond8, w_ada, b_ada):
    n_l, d, d6 = w_ada.shape
    tn = 1536
    return pl.pallas_call(
        _adaln_kernel,
        grid=(n_l, d6 // tn),
        in_specs=[pl.BlockSpec((8, d), lambda l, j: (0, 0)),
                  pl.BlockSpec((1, d, tn), lambda l, j: (l, 0, j)),
                  pl.BlockSpec((1, 1, tn), lambda l, j: (l, 0, j))],
        out_specs=pl.BlockSpec((1, 8, tn), lambda l, j: (l, 0, j)),
        out_shape=jax.ShapeDtypeStruct((n_l, 8, d6), _F32),
        compiler_params=_cparams(("arbitrary", "arbitrary"), 40),
        name="adaln",
    )(cond8, w_ada, b_ada.reshape(n_l, 1, d6))


def _pre_kernel(*refs, d, latent, sa, sb, sc):
    (x_ref, mod_ref, g1_ref, wa_ref, wb_ref, wcq_ref, wckv_ref, wckr_ref, wdu_ref,
     cqg_ref, ckvg_ref, wuq_ref, wkc_ref, wv_ref) = refs[:14]
    refs = refs[14:]
    if latent:
        cosa_ref, sina_ref, cosc_ref, sinc_ref = refs[:4]
        refs = refs[4:]
        aq_ref, ak_ref, av_ref, bq_ref, bk_ref, bv_ref, qc_ref, kc_ref, vc_ref, z_ref = refs
    else:
        (aq_ref, ak_ref, av_ref, bq_ref, bk_ref, bv_ref, qc_ref, kc_ref, vc_ref,
         ckvn_ref, ckr_ref, z_ref) = refs

    x = x_ref[...]
    mod = mod_ref[0]
    shift1, scale1 = mod[:, 0:d], mod[:, d:2 * d]
    h = (_rms(x, g1_ref[...]) * (1.0 + scale1) + shift1).astype(_MM)

    pa = _dot(h, wa_ref[...])
    aq, ak, av = pa[:, 0:HEAD_W], pa[:, HEAD_W:2 * HEAD_W], pa[:, 2 * HEAD_W:3 * HEAD_W]
    if latent:
        cosa, sina = cosa_ref[...], sina_ref[...]
        aq = _rope(aq, cosa, sina)
        ak = _rope(ak, cosa, sina)
    aq_ref[...] = (aq * sa).astype(aq_ref.dtype)
    ak_ref[...] = ak.astype(ak_ref.dtype)
    av_ref[...] = av.astype(av_ref.dtype)

    pb = _dot(h, wb_ref[...])
    bq_ref[...] = (pb[:, 0:HEAD_W] * sb).astype(bq_ref.dtype)
    bk_ref[...] = pb[:, HEAD_W:2 * HEAD_W].astype(bk_ref.dtype)
    bv_ref[...] = pb[:, 2 * HEAD_W:3 * HEAD_W].astype(bv_ref.dtype)

    cq = _dot(h, wcq_ref[...])
    qc = _dot(_rms(cq, cqg_ref[...]), wuq_ref[...])
    ckv = _dot(h, wckv_ref[...])
    ckvn = _rms(ckv, ckvg_ref[...])
    kr = _dot(h, wckr_ref[...])
    if latent:
        cosc, sinc = cosc_ref[...], sinc_ref[...]
        qc = _rope(qc, cosc, sinc)
        kr = _rope(kr, cosc, sinc)
    else:
        ckvn_ref[...] = ckvn
        ckr_ref[...] = kr[:, NOPE_C:NOPE_C + ROPE_C]
    qc_ref[...] = (qc * sc).astype(qc_ref.dtype)
    kc_ref[...] = (_dot(ckvn, wkc_ref[...]) + kr).astype(kc_ref.dtype)
    vc_ref[...] = _dot(ckvn, wv_ref[...]).astype(vc_ref.dtype)

    du = _dot(h, wdu_ref[...])
    z_ref[...] = du[:, 0:CONV_CH] * _sigmoid(du[:, CONV_CH:2 * CONV_CH])


def _pre(x2d, mods3, mod_row, lw, *, latent, seq, tables):
    t, d = x2d.shape
    tm = min(512, seq)
    n_seq_tiles = seq // tm
    grid = (t // tm,)
    row = lambda i: (i, 0)
    const = lambda i: (0, 0)
    wnames = ("wa", "wb", "wcq", "wckv", "wckr", "wdu", "cqg", "ckvg", "wuq", "wkc", "wv")
    in_specs = [pl.BlockSpec((tm, d), row),
                pl.BlockSpec((1, 1, 6 * d), lambda i: (mod_row(i * tm), 0, 0)),
                pl.BlockSpec((1, d), const)]
    args = [x2d, mods3, lw["g1"]]
    for n in wnames:
        in_specs.append(pl.BlockSpec(lw[n].shape, const))
        args.append(lw[n])
    if latent:
        for tb in tables:
            in_specs.append(pl.BlockSpec((tm, HEAD_W), lambda i: (i % n_seq_tiles, 0)))
            args.append(tb)
    slab = lambda dt: jax.ShapeDtypeStruct((t, HEAD_W), dt)
    kv_dt = _MM if latent else _F32
    out_shape = [slab(_MM), slab(kv_dt), slab(kv_dt), slab(_MM), slab(kv_dt), slab(kv_dt),
                 slab(_MM), slab(_MM), slab(_MM)]
    out_specs = [pl.BlockSpec((tm, HEAD_W), row)] * 9
    if not latent:
        out_shape += [jax.ShapeDtypeStruct((t, KV_LORA), _F32), jax.ShapeDtypeStruct((t, ROPE_C), _F32)]
        out_specs += [pl.BlockSpec((tm, KV_LORA), row), pl.BlockSpec((tm, ROPE_C), row)]
    out_shape.append(jax.ShapeDtypeStruct((t, CONV_CH), _F32))
    out_specs.append(pl.BlockSpec((tm, CONV_CH), row))
    kern = functools.partial(_pre_kernel, d=d, latent=latent, sa=DH_A ** -0.5, sb=DH_B ** -0.5,
                             sc=(NOPE_C + ROPE_C) ** -0.5)
    return pl.pallas_call(kern, grid=grid, in_specs=in_specs, out_specs=out_specs, out_shape=out_shape,
                          compiler_params=_cparams(("arbitrary",), 48),
                          name="pre_lat" if latent else "pre_ctx")(*args)


def _attn_kernel(*refs, n_heads, n_maps, dqk, dv, tk, n_own, has_cache, diff, lam_init):
    q_ref, k_ref, v_ref = refs[:3]
    refs = refs[3:]
    if has_cache:
        kc_ref, vc_ref = refs[:2]
        refs = refs[2:]
    if diff:
        alam_ref, subg_ref = refs[:2]
        refs = refs[2:]
    o_ref, m_sc, l_sc, acc_sc = refs

    if diff:
        lv = alam_ref[...]
        lam = (jnp.exp(jnp.sum(lv[0:1] * lv[1:2], axis=-1, keepdims=True))
               - jnp.exp(jnp.sum(lv[2:3] * lv[3:4], axis=-1, keepdims=True)) + lam_init)

    for h in range(n_heads):
        vsl = slice(h * dv, (h + 1) * dv)
        qs = [q_ref[:, (h * n_maps + mi) * dqk:(h * n_maps + mi + 1) * dqk] for mi in range(n_maps)]

        def first_tile(kt, vt):
            for mi in range(n_maps):
                s = _dot_nt(qs[mi], kt[:, mi * dqk:(mi + 1) * dqk])
                m = jnp.max(s, axis=-1, keepdims=True)
                p = jnp.exp(s - m)
                m_sc[mi] = m
                l_sc[mi] = jnp.sum(p, axis=-1, keepdims=True)
                acc_sc[mi] = _dot(p, vt)

        def next_tile(kt, vt):
            for mi in range(n_maps):
                s = _dot_nt(qs[mi], kt[:, mi * dqk:(mi + 1) * dqk])
                m_prev = m_sc[mi]
                m_new = jnp.maximum(m_prev, jnp.max(s, axis=-1, keepdims=True))
                alpha = jnp.exp(m_prev - m_new)
                p = jnp.exp(s - m_new)
                l_sc[mi] = alpha * l_sc[mi] + jnp.sum(p, axis=-1, keepdims=True)
                acc_sc[mi] = alpha * acc_sc[mi] + _dot(p, vt)
                m_sc[mi] = m_new

        ksl = slice(h * n_maps * dqk, (h + 1) * n_maps * dqk)

        def own_tile(j):
            start = pl.multiple_of(j * tk, tk)
            return k_ref[pl.ds(start, tk), ksl], v_ref[pl.ds(start, tk), vsl]

        if has_cache:
            first_tile(kc_ref[:, ksl], vc_ref[:, vsl])
            lo = 0
        else:
            first_tile(*own_tile(0))
            lo = 1
        if n_own > lo:
            def body(j, carry):
                next_tile(*own_tile(j))
                return carry
            lax.fori_loop(lo, n_own, body, 0)

        if diff:
            o = acc_sc[0] / l_sc[0] - lam * (acc_sc[1] / l_sc[1])
            o = _rms(o, subg_ref[...]) * (1.0 - lam_init)
        else:
            o = acc_sc[0] / l_sc[0]
        o_ref[:, vsl] = o.astype(o_ref.dtype)


def _attn(q, k, v, cache, diff_args, *, n_heads, n_maps, dqk, dv, lam_init=0.0, name):
    nb, s, _ = q.shape
    tq = min(512, s)
    tk = min(512, s)
    n_own = s // tk
    qmap = lambda b, i: (b, i, 0)
    bmap = lambda b, i: (b, 0, 0)
    in_specs = [pl.BlockSpec((None, tq, HEAD_W), qmap),
                pl.BlockSpec((None, s, HEAD_W), bmap),
                pl.BlockSpec((None, s, HEAD_W), bmap)]
    args = [q, k, v]
    if cache is not None:
        past = cache[0].shape[1]
        in_specs += [pl.BlockSpec((None, past, HEAD_W), bmap)] * 2
        args += list(cache)
    if diff_args is not None:
        in_specs += [pl.BlockSpec(diff_args[0].shape, lambda b, i: (0, 0)),
                     pl.BlockSpec(diff_args[1].shape, lambda b, i: (0, 0))]
        args += list(diff_args)
    kern = functools.partial(_attn_kernel, n_heads=n_heads, n_maps=n_maps, dqk=dqk, dv=dv, tk=tk, n_own=n_own,
                             has_cache=cache is not None, diff=diff_args is not None, lam_init=lam_init)
    return pl.pallas_call(
        kern, grid=(nb, s // tq), in_specs=in_specs,
        out_specs=pl.BlockSpec((None, tq, HEAD_W), qmap),
        out_shape=jax.ShapeDtypeStruct((nb, s, HEAD_W), _MM),
        scratch_shapes=[pltpu.VMEM((n_maps, tq, 1), _F32), pltpu.VMEM((n_maps, tq, 1), _F32),
                        pltpu.VMEM((n_maps, tq, dv), _F32)],
        compiler_params=_cparams(("arbitrary", "arbitrary"), 48),
        name=name,
    )(*args)


def _nbr_kernel(q_ref, k_ref, v_ref, kc_ref, vc_ref, tb_ref, o_ref, *, rows_per_step, rows, span):
    i = pl.program_id(1)

    def row_body(rr, carry):
        r = i * rows_per_step + rr
        rs = jnp.clip(r - WIN_R // 2, 0, rows - WIN_R)
        off = r - rs
        qstart = pl.multiple_of(rr * GRID_W, GRID_W)
        kstart = pl.multiple_of(rs * GRID_W, GRID_W)
        q = q_ref[pl.ds(qstart, GRID_W), :]
        ks = k_ref[pl.ds(kstart, span), :]
        vs = v_ref[pl.ds(kstart, span), :]
        outs = []
        for h in range(H_B):
            hsl = slice(h * DH_B, (h + 1) * DH_B)
            qh = q[:, hsl]
            s_w = _dot_nt(qh, ks[:, hsl]) + tb_ref[h, off]
            s_c = _dot_nt(qh, kc_ref[:, hsl])
            m = jnp.maximum(jnp.max(s_w, axis=-1, keepdims=True), jnp.max(s_c, axis=-1, keepdims=True))
            pw = jnp.exp(s_w - m)
            pc = jnp.exp(s_c - m)
            l = jnp.sum(pw, axis=-1, keepdims=True) + jnp.sum(pc, axis=-1, keepdims=True)
            outs.append((_dot(pc, vc_ref[:, hsl]) + _dot(pw, vs[:, hsl])) / l)
        o_ref[pl.ds(qstart, GRID_W), :] = jnp.concatenate(outs, axis=-1).astype(o_ref.dtype)
        return carry

    lax.fori_loop(0, rows_per_step, row_body, 0)


def _nbr_bias_table(rpb):
    off = np.arange(WIN_R)[:, None, None, None]
    c = np.arange(GRID_W)[None, :, None, None]
    i = np.arange(WIN_R)[None, None, :, None]
    kc = np.arange(GRID_W)[None, None, None, :]
    cstart = np.clip(c - WIN_C // 2, 0, GRID_W - WIN_C)
    valid = np.broadcast_to((kc >= cstart) & (kc < cstart + WIN_C), (WIN_R, GRID_W, WIN_R, GRID_W))
    ridx = np.broadcast_to(i - off + WIN_R - 1, valid.shape)
    cidx = np.broadcast_to(np.clip(kc - c + WIN_C - 1, 0, 2 * WIN_C - 2), valid.shape)
    tb = jnp.where(valid[None], rpb[:, ridx, cidx].astype(_F32), -jnp.inf)
    return tb.reshape(rpb.shape[0], WIN_R, GRID_W, WIN_R * GRID_W)


def _nbr(q, k, v, kc, vc, tb):
    nb, s, _ = q.shape
    rows = s // GRID_W
    rows_per_step = min(8, rows)
    span = WIN_R * GRID_W
    past = kc.shape[1]
    tq = rows_per_step * GRID_W
    qmap = lambda b, i: (b, i, 0)
    bmap = lambda b, i: (b, 0, 0)
    kern = functools.partial(_nbr_kernel, rows_per_step=rows_per_step, rows=rows, span=span)
    return pl.pallas_call(
        kern, grid=(nb, rows // rows_per_step),
        in_specs=[pl.BlockSpec((None, tq, HEAD_W), qmap),
                  pl.BlockSpec((None, s, HEAD_W), bmap),
                  pl.BlockSpec((None, s, HEAD_W), bmap),
                  pl.BlockSpec((None, past, HEAD_W), bmap),
                  pl.BlockSpec((None, past, HEAD_W), bmap),
                  pl.BlockSpec(tb.shape, lambda b, i: (0, 0, 0, 0))],
        out_specs=pl.BlockSpec((None, tq, HEAD_W), qmap),
        out_shape=jax.ShapeDtypeStruct((nb, s, HEAD_W), _MM),
        compiler_params=_cparams(("arbitrary", "arbitrary"), 48),
        name="nbr_attn",
    )(q, k, v, kc, vc, tb)


def _mla_cache_kernel(ckv_ref, kr_ref, wkc_ref, wv_ref, e_ref, kc_ref, vc_ref):
    ckv = ckv_ref[...]
    kc_ref[...] = (_dot(ckv, wkc_ref[...]) + _dot(kr_ref[...], e_ref[...])).astype(kc_ref.dtype)
    vc_ref[...] = _dot(ckv, wv_ref[...]).astype(vc_ref.dtype)


def _mla_cache(ckv, kr, wkc, wv, expand):
    nb, past, _ = ckv.shape
    bmap = lambda b: (b, 0, 0)
    const = lambda b: (0, 0)
    return pl.pallas_call(
        _mla_cache_kernel, grid=(nb,),
        in_specs=[pl.BlockSpec((None, past, KV_LORA), bmap), pl.BlockSpec((None, past, ROPE_C), bmap),
                  pl.BlockSpec(wkc.shape, const), pl.BlockSpec(wv.shape, const), pl.BlockSpec(expand.shape, const)],
        out_specs=[pl.BlockSpec((None, past, HEAD_W), bmap)] * 2,
        out_shape=[jax.ShapeDtypeStruct((nb, past, HEAD_W), _MM)] * 2,
        compiler_params=_cparams(("arbitrary",), 32),
        name="mla_cache",
    )(ckv, kr, wkc, wv, expand)


CONV_HALO = 16


def _conv_kernel(z_ref, w_ref, b_ref, g_ref, beta_ref, o_ref, zp_ref, *, seq, rc):
    zp_ref[0:CONV_HALO, :] = jnp.zeros((CONV_HALO, CONV_CH), _F32)
    zp_ref[CONV_HALO:CONV_HALO + seq, :] = z_ref[...]
    zp_ref[CONV_HALO + seq:2 * CONV_HALO + seq, :] = jnp.zeros((CONV_HALO, CONV_CH), _F32)
    w = w_ref[...]
    shift = CONV_HALO - CONV_K // 2

    def chunk(c, carry):
        base = pl.multiple_of(c * rc, rc)
        win = zp_ref[pl.ds(base, rc + 2 * CONV_HALO), :]
        acc = jnp.zeros((rc, CONV_CH), _F32)
        for j in range(CONV_K):
            acc = acc + win[j + shift:j + shift + rc, :] * w[j:j + 1, :]
        y = acc + b_ref[...]
        mu = jnp.mean(y, axis=-1, keepdims=True)
        yc = y - mu
        var = jnp.mean(yc * yc, axis=-1, keepdims=True)
        yn = yc * lax.rsqrt(var + EPS) * g_ref[...] + beta_ref[...]
        o_ref[pl.ds(base, rc), :] = (yn * _sigmoid(yn)).astype(o_ref.dtype)
        return carry

    lax.fori_loop(0, seq // rc, chunk, 0)


def _conv(z, w, b, g, beta):
    nb, seq, _ = z.shape
    rc = min(128, seq)
    bmap = lambda i: (i, 0, 0)
    const = lambda i: (0, 0)
    kern = functools.partial(_conv_kernel, seq=seq, rc=rc)
    return pl.pallas_call(
        kern, grid=(nb,),
        in_specs=[pl.BlockSpec((None, seq, CONV_CH), bmap), pl.BlockSpec(w.shape, const),
                  pl.BlockSpec(b.shape, const), pl.BlockSpec(g.shape, const), pl.BlockSpec(beta.shape, const)],
        out_specs=pl.BlockSpec((None, seq, CONV_CH), bmap),
        out_shape=jax.ShapeDtypeStruct((nb, seq, CONV_CH), _MM),
        scratch_shapes=[pltpu.VMEM((seq + 2 * CONV_HALO, CONV_CH), _F32)],
        compiler_params=_cparams(("arbitrary",), 40),
        name="conv",
    )(z, w, b, g, beta)


def _post_kernel(x_ref, mod_ref, oa_ref, ob_ref, oc_ref, od_ref, g1_ref, wg_ref, bg_ref,
                 awo_ref, bwo_ref, cwo_ref, dwo_ref, wout_ref, g2_ref, rhi_ref, rlo_ref, rb_ref,
                 x1_ref, h2_ref, dw_ref, *, d):
    x = x_ref[...]
    mod = mod_ref[0]
    shift1, scale1, gate1 = mod[:, 0:d], mod[:, d:2 * d], mod[:, 2 * d:3 * d]
    shift2, scale2 = mod[:, 3 * d:4 * d], mod[:, 4 * d:5 * d]
    h = (_rms(x, g1_ref[...]) * (1.0 + scale1) + shift1).astype(_MM)

    merged = None
    for bi, (o_ref, wo_ref) in enumerate(((oa_ref, awo_ref), (ob_ref, bwo_ref), (oc_ref, cwo_ref), (od_ref, dwo_ref))):
        gate = _sigmoid(_dot(h, wg_ref[:, bi * d:(bi + 1) * d]) + bg_ref[:, bi * d:(bi + 1) * d])
        term = gate * _dot(o_ref[...], wo_ref[...])
        merged = term if merged is None else merged + term
    x1 = x + gate1 * _dot(merged, wout_ref[...])
    x1_ref[...] = x1

    h2 = _rms(x1, g2_ref[...]) * (1.0 + scale2) + shift2
    h2_hi = h2.astype(_MM)
    h2_lo = (h2 - h2_hi.astype(_F32)).astype(_MM)
    h2_ref[...] = h2_hi
    r = (_dot(h2_hi, rhi_ref[...]) + _dot(h2_lo, rhi_ref[...]) + _dot(h2_hi, rlo_ref[...])) + rb_ref[...]
    lgp, lep = r[:, 0:LANES], r[:, LANES:2 * LANES]
    lane = lax.broadcasted_iota(jnp.int32, lgp.shape, 1)
    neg = -jnp.inf
    lg = jnp.where(lane < N_GROUPS, lgp, neg)
    mg = jnp.max(lg, axis=-1, keepdims=True)
    gi = jnp.min(jnp.where(lg == mg, lane, LANES), axis=-1, keepdims=True)
    gw = 1.0 / jnp.sum(jnp.exp(lg - mg), axis=-1, keepdims=True)
    in_group = (lane < N_EXPERTS) & ((lane // EXPERTS_PER_GROUP) == gi)
    le = jnp.where(in_group, lep, neg)
    t1 = jnp.max(le, axis=-1, keepdims=True)
    i1 = jnp.min(jnp.where(le == t1, lane, LANES), axis=-1, keepdims=True)
    le2 = jnp.where(lane == i1, neg, le)
    t2 = jnp.max(le2, axis=-1, keepdims=True)
    i2 = jnp.min(jnp.where(le2 == t2, lane, LANES), axis=-1, keepdims=True)
    e2 = jnp.exp(t2 - t1)
    w1 = gw / (1.0 + e2)
    w2 = gw * e2 / (1.0 + e2)
    dw_ref[...] = jnp.where(lane == i1, w1, 0.0) + jnp.where(lane == i2, w2, 0.0)


def _post(x2d, mods3, mod_row, oa, ob, oc, od, lw):
    t, d = x2d.shape
    tm = min(512, t)
    row = lambda i: (i, 0)
    const = lambda i: (0, 0)
    wnames = ("g1", "wgate", "bgate", "awo", "bwo", "cwo", "dwo", "wout", "g2", "rhi", "rlo", "rb")
    in_specs = [pl.BlockSpec((tm, d), row), pl.BlockSpec((1, 1, 6 * d), lambda i: (mod_row(i * tm), 0, 0))]
    in_specs += [pl.BlockSpec((tm, HEAD_W), row)] * 4
    in_specs += [pl.BlockSpec(lw[n].shape, const) for n in wnames]
    args = [x2d, mods3, oa, ob, oc, od] + [lw[n] for n in wnames]
    return pl.pallas_call(
        functools.partial(_post_kernel, d=d), grid=(t // tm,), in_specs=in_specs,
        out_specs=[pl.BlockSpec((tm, d), row), pl.BlockSpec((tm, d), row), pl.BlockSpec((tm, LANES), row)],
        out_shape=[jax.ShapeDtypeStruct((t, d), _F32), jax.ShapeDtypeStruct((t, d), _MM),
                   jax.ShapeDtypeStruct((t, LANES), _F32)],
        compiler_params=_cparams(("arbitrary",), 56),
        name="post",
    )(*args)


def _moe_kernel(h2_ref, dw_ref, x1_ref, mod_ref, w1_ref, w2_ref, fg_ref, o_ref, acc_ref, *, d, final):
    e = pl.program_id(1)

    @pl.when(e == 0)
    def _():
        acc_ref[...] = jnp.zeros_like(acc_ref)

    hid = _dot(h2_ref[...], w1_ref[0])
    gate, up = hid[:, 0:D_EXPERT], hid[:, D_EXPERT:2 * D_EXPERT]
    dw = dw_ref[...]
    lane = lax.broadcasted_iota(jnp.int32, dw.shape, 1)
    we = jnp.sum(jnp.where(lane == e, dw, 0.0), axis=-1, keepdims=True)
    act = gate * _sigmoid(gate) * up * we
    acc_ref[...] += _dot(act, w2_ref[0])

    @pl.when(e == pl.num_programs(1) - 1)
    def _():
        gate2 = mod_ref[0][:, 5 * d:6 * d]
        x2 = x1_ref[...] + gate2 * acc_ref[...]
        if final:
            x2 = _rms(x2, fg_ref[...])
        o_ref[...] = x2


def _moe(h2, dw, x1, mods3, mod_row, w1, w2, fg, *, final):
    t, d = x1.shape
    tm = min(1024, t)
    n_e = w1.shape[0]
    row = lambda i, e: (i, 0)
    return pl.pallas_call(
        functools.partial(_moe_kernel, d=d, final=final), grid=(t // tm, n_e),
        in_specs=[pl.BlockSpec((tm, d), row), pl.BlockSpec((tm, LANES), row), pl.BlockSpec((tm, d), row),
                  pl.BlockSpec((1, 1, 6 * d), lambda i, e: (mod_row(i * tm), 0, 0)),
                  pl.BlockSpec((1,) + w1.shape[1:], lambda i, e: (e, 0, 0)),
                  pl.BlockSpec((1,) + w2.shape[1:], lambda i, e: (e, 0, 0)),
                  pl.BlockSpec((1, d), lambda i, e: (0, 0))],
        out_specs=pl.BlockSpec((tm, d), row),
        out_shape=jax.ShapeDtypeStruct((t, d), _F32),
        scratch_shapes=[pltpu.VMEM((tm, d), _F32)],
        compiler_params=_cparams(("arbitrary", "arbitrary"), 56),
        name="moe_final" if final else "moe",
    )(h2, dw, x1, mods3, w1, w2, fg)


def _rope_tables(n_tok, dim):
    hq = dim // 4
    tpos = np.arange(n_tok)
    rows = (tpos // GRID_W).astype(np.float32)
    cols = (tpos % GRID_W).astype(np.float32)
    inv = jnp.asarray(ROPE_THETA, _F32) ** (-jnp.arange(hq, dtype=_F32) / hq)
    ar = jnp.asarray(rows)[:, None] * inv[None, :]
    ac = jnp.asarray(cols)[:, None] * inv[None, :]
    ang = jnp.concatenate([ar, ar, ac, ac], axis=-1)
    return jnp.cos(ang), jnp.sin(ang)


def _layer_weights(l, p):
    mm = lambda a: a.astype(_MM)
    w_in = p["w_in"][l]
    pts = np.cumsum([0, 2 * H_A * DH_A, 2 * H_A * DH_A, 2 * H_A * DH_A, H_B * DH_B, H_B * DH_B, H_B * DH_B,
                     Q_LORA, KV_LORA, ROPE_C, 2 * CONV_CH])
    seg = lambda i, j: w_in[:, pts[i]:pts[j]]
    d = w_in.shape[0]
    wckr = seg(8, 9)
    zeros_n = jnp.zeros((d, NOPE_C), w_in.dtype)
    wckr_t = jnp.concatenate([zeros_n, wckr] * H_C, axis=1)
    wukv = p["c_wukv"][l].reshape(KV_LORA, H_C, NOPE_C + DV_C)
    wkc = jnp.concatenate([wukv[:, :, :NOPE_C], jnp.zeros((KV_LORA, H_C, ROPE_C), wukv.dtype)], axis=-1)
    wkc = wkc.reshape(KV_LORA, H_C * (NOPE_C + ROPE_C))
    wv = wukv[:, :, NOPE_C:].reshape(KV_LORA, H_C * DV_C)
    router = jnp.zeros((d, 2 * LANES), _F32)
    router = router.at[:, 0:N_GROUPS].set(p["moe_w_group"][l]).at[:, LANES:LANES + N_EXPERTS].set(p["moe_w_expert"][l])
    rhi = router.astype(_MM)
    rlo = (router - rhi.astype(_F32)).astype(_MM)
    rb = jnp.zeros((1, 2 * LANES), _F32)
    rb = rb.at[0, 0:N_GROUPS].set(p["moe_b_group"][l]).at[0, LANES:LANES + N_EXPERTS].set(p["moe_b_expert"][l])
    conv_w = p["d_conv_w"][l]
    return dict(
        g1=p["norm1_g"][l][None], g2=p["norm2_g"][l][None],
        wa=mm(seg(0, 3)), wb=mm(seg(3, 6)), wcq=mm(seg(6, 7)), wckv=mm(seg(7, 8)), wckr=mm(wckr_t), wdu=mm(seg(9, 10)),
        cqg=p["c_q_norm_g"][l][None], ckvg=p["c_kv_norm_g"][l][None],
        wuq=mm(p["c_wuq"][l]), wkc=mm(wkc), wv=mm(wv),
        alam=p["a_lambda"][l], subg=p["a_subln_g"][l][None],
        wgate=mm(p["w_gate"][l]), bgate=p["b_gate"][l][None],
        awo=mm(p["a_wo"][l]), bwo=mm(p["b_wo"][l]), cwo=mm(p["c_wo"][l]), dwo=mm(p["d_wo"][l]),
        wout=mm(p["w_out"][l]), rhi=rhi, rlo=rlo, rb=rb,
        conv_w=conv_w, conv_b=p["d_conv_b"][l][None], ln_g=p["d_ln_g"][l][None], ln_b=p["d_ln_b"][l][None],
        w1=mm(p["moe_w1"][l]), w2=mm(p["moe_w2"][l]),
        tb=_nbr_bias_table(p["b_rpb"][l]),
    )


def _layer(x2d, mods3, mod_row, lw, l, n_layers, *, nb, seq, fg, ctx, tables):
    latent = ctx is not None
    t = x2d.shape[0]
    outs = _pre(x2d, mods3, mod_row, lw, latent=latent, seq=seq, tables=tables)
    if latent:
        aq, ak, av, bq, bk, bv, qc, kc, vc, z = outs
        own = None
    else:
        aq, ak, av, bq, bk, bv, qc, kc, vc, ckvn, ckr, z = outs
        own = (ak, av, bk, bv, ckvn, ckr)
    r3 = lambda a: a.reshape(nb, seq, a.shape[-1])
    lam_init = 0.8 - 0.6 * math.exp(-0.3 * l)
    if latent:
        a_k_c, a_v_c, b_k_c, b_v_c, c_k_c, c_v_c = ctx
        oa = _attn(r3(aq), r3(ak), r3(av), (a_k_c, a_v_c), (lw["alam"], lw["subg"]), n_heads=H_A, n_maps=2,
                   dqk=DH_A, dv=2 * DH_A, lam_init=lam_init, name="attn_a_lat")
        ob = _nbr(r3(bq), r3(bk), r3(bv), b_k_c, b_v_c, lw["tb"])
        oc = _attn(r3(qc), r3(kc), r3(vc), (c_k_c, c_v_c), None, n_heads=H_C, n_maps=1,
                   dqk=NOPE_C + ROPE_C, dv=DV_C, name="attn_c_lat")
    else:
        oa = _attn(r3(aq), r3(ak), r3(av), None, (lw["alam"], lw["subg"]), n_heads=H_A, n_maps=2,
                   dqk=DH_A, dv=2 * DH_A, lam_init=lam_init, name="attn_a_ctx")
        ob = _attn(r3(bq), r3(bk), r3(bv), None, None, n_heads=H_B, n_maps=1, dqk=DH_B, dv=DH_B, name="attn_b_ctx")
        oc = _attn(r3(qc), r3(kc), r3(vc), None, None, n_heads=H_C, n_maps=1,
                   dqk=NOPE_C + ROPE_C, dv=DV_C, name="attn_c_ctx")
    od = _conv(r3(z), lw["conv_w"], lw["conv_b"], lw["ln_g"], lw["ln_b"])
    f2 = lambda a: a.reshape(t, a.shape[-1])
    x1, h2, dw = _post(x2d, mods3, mod_row, f2(oa), f2(ob), f2(oc), f2(od), lw)
    x2 = _moe(h2, dw, x1, mods3, mod_row, lw["w1"], lw["w2"], fg, final=(l == n_layers - 1))
    return x2, own


def kernel(x_prompt, x_sample, cache_a_k, cache_a_v, cache_b_k, cache_b_v, cache_c_kv, cache_c_krope, c, c_ctx, norm1_g, norm2_g, w_ada, b_ada, w_in, a_lambda, a_subln_g, a_wo, b_rpb, b_wo, c_q_norm_g, c_kv_norm_g, c_wuq, c_wukv, c_wo, d_conv_w, d_conv_b, d_ln_g, d_ln_b, d_wo, w_gate, b_gate, w_out, moe_w_group, moe_b_group, moe_w_expert, moe_b_expert, moe_w1, moe_w2, final_norm_g):
    p = dict(norm1_g=norm1_g, norm2_g=norm2_g, w_in=w_in, a_lambda=a_lambda, a_subln_g=a_subln_g, a_wo=a_wo,
             b_rpb=b_rpb, b_wo=b_wo, c_q_norm_g=c_q_norm_g, c_kv_norm_g=c_kv_norm_g, c_wuq=c_wuq, c_wukv=c_wukv,
             c_wo=c_wo, d_conv_w=d_conv_w, d_conv_b=d_conv_b, d_ln_g=d_ln_g, d_ln_b=d_ln_b, d_wo=d_wo,
             w_gate=w_gate, b_gate=b_gate, w_out=w_out, moe_w_group=moe_w_group, moe_b_group=moe_b_group,
             moe_w_expert=moe_w_expert, moe_b_expert=moe_b_expert, moe_w1=moe_w1, moe_w2=moe_w2)
    n_layers = w_in.shape[0]
    bc, sc, d = x_prompt.shape
    bl, sl, _ = x_sample.shape
    past = cache_a_k.shape[2]
    assert bl + 1 <= 8 and sl % GRID_W == 0

    cond8 = jnp.zeros((8, d), _F32).at[0].set(c_ctx).at[1:1 + bl].set(c)
    mods3 = _adaln(cond8, w_ada, b_ada).reshape(n_layers * 8, 1, 6 * d)
    fg = final_norm_g[None]

    cos_a, sin_a = _rope_tables(sl, DH_A)
    cos_c, sin_c = _rope_tables(sl, ROPE_C)
    reps_a = HEAD_W // DH_A
    ones_n, zeros_n = jnp.ones((sl, NOPE_C), _F32), jnp.zeros((sl, NOPE_C), _F32)
    tables = (jnp.tile(cos_a, (1, reps_a)), jnp.tile(sin_a, (1, reps_a)),
              jnp.concatenate([ones_n, cos_c] * H_C, axis=1), jnp.concatenate([zeros_n, sin_c] * H_C, axis=1))
    expand = np.zeros((ROPE_C, HEAD_W), np.float32)
    for hh in range(H_C):
        expand[np.arange(ROPE_C), hh * (NOPE_C + ROPE_C) + NOPE_C + np.arange(ROPE_C)] = 1.0
    expand = jnp.asarray(expand, _MM)

    lws = [_layer_weights(l, p) for l in range(n_layers)]

    xp = x_prompt.reshape(bc * sc, d)
    states = []
    for l in range(n_layers):
        xp, own = _layer(xp, mods3, (lambda tok, l=l: l * 8), lws[l], l, n_layers, nb=bc, seq=sc, fg=fg, ctx=None,
                         tables=None)
        states.append(own)
    y_prompt = xp.reshape(bc, sc, d)

    xs = x_sample.reshape(bl * sl, d)
    for l in range(n_layers):
        flat = lambda a: a[:, l].reshape(bl, past, -1)
        c_k_c, c_v_c = _mla_cache(flat(cache_c_kv), flat(cache_c_krope), lws[l]["wkc"], lws[l]["wv"], expand)
        ctx = (flat(cache_a_k).astype(_MM), flat(cache_a_v).astype(_MM),
               flat(cache_b_k).astype(_MM), flat(cache_b_v).astype(_MM), c_k_c, c_v_c)
        xs, _ = _layer(xs, mods3, (lambda tok, l=l: l * 8 + 1 + tok // sl), lws[l], l, n_layers,
                       nb=bl, seq=sl, fg=fg, ctx=ctx, tables=tables)
    y_sample = xs.reshape(bl, sl, d)

    st = lambda k, shape: jnp.stack([s[k].reshape(shape) for s in states], axis=1)
    new_a_k = st(0, (bc, sc, H_A, 2 * DH_A))
    new_a_v = st(1, (bc, sc, H_A, 2 * DH_A))
    new_b_k = st(2, (bc, sc, H_B, DH_B))
    new_b_v = st(3, (bc, sc, H_B, DH_B))
    new_c_kv = st(4, (bc, sc, KV_LORA))
    new_c_krope = st(5, (bc, sc, ROPE_C))
    return (y_prompt, y_sample, new_a_k, new_a_v, new_b_k, new_b_v, new_c_kv, new_c_krope)
```

```python
import functools
import math

import numpy as np
import jax
import jax.numpy as jnp
from jax import lax
from jax.experimental import pallas as pl
from jax.experimental.pallas import tpu as pltpu

GRID_W = 64
EPS = 1e-6
ROPE_THETA = 10000.0
H_A, DH_A = 4, 32
H_B, DH_B = 4, 64
WIN_R, WIN_C = 8, 16
H_C, Q_LORA, KV_LORA, NOPE_C, ROPE_C, DV_C = 4, 256, 128, 32, 32, 64
CONV_CH, CONV_K = 256, 31
N_BRANCH = 4
N_GROUPS, EXPERTS_PER_GROUP, N_EXPERTS, D_EXPERT = 4, 4, 16, 256
HEAD_W = 256
LANES = 128
V7X_VMEM_BYTES = 64 * 1024 * 1024
LOG2E = math.log2(math.e)

_MM = jnp.bfloat16
_F32 = jnp.float32


def _cparams(sem, vmem_mb):
    return pltpu.CompilerParams(dimension_semantics=sem,
                                vmem_limit_bytes=min(vmem_mb * 1024 * 1024, V7X_VMEM_BYTES - 8 * 1024 * 1024))


def _dot(a, b):
    return jnp.dot(a.astype(_MM), b.astype(_MM), preferred_element_type=_F32)


def _dot_nt(a, b):
    return lax.dot_general(a.astype(_MM), b.astype(_MM), (((1,), (1,)), ((), ())),
                           preferred_element_type=_F32)


def _sigmoid(x):
    return 1.0 / (1.0 + jnp.exp(-x))


def _rms(x, g):
    return x * lax.rsqrt(jnp.mean(x * x, axis=-1, keepdims=True) + EPS) * g


def _rope(x, cos, sin):
    w = x.shape[-1]
    lane = lax.broadcasted_iota(jnp.int32, x.shape, 1)
    first = (lane % 16) < 8
    fwd = pltpu.roll(x, w - 8, axis=1)
    bwd = pltpu.roll(x, 8, axis=1)
    return x * cos + jnp.where(first, -fwd, bwd) * sin


def _adaln_kernel(cond_ref, w_ref, b_ref, o_ref):
    c = cond_ref[...]
    a = c * _sigmoid(c)
    o_ref[0] = _dot(a, w_ref[0]) + b_ref[0]


def _adaln(cond8, w_ada, b_ada):
    n_l, d, d6 = w_ada.shape
    tn = 1536
    return pl.pallas_call(
        _adaln_kernel,
        grid=(n_l, d6 // tn),
        in_specs=[pl.BlockSpec((8, d), lambda l, j: (0, 0)),
                  pl.BlockSpec((1, d, tn), lambda l, j: (l, 0, j)),
                  pl.BlockSpec((1, 1, tn), lambda l, j: (l, 0, j))],
        out_specs=pl.BlockSpec((1, 8, tn), lambda l, j: (l, 0, j)),
        out_shape=jax.ShapeDtypeStruct((n_l, 8, d6), _F32),
        compiler_params=_cparams(("arbitrary", "arbitrary"), 40),
        name="adaln",
    )(cond8, w_ada, b_ada.reshape(n_l, 1, d6))


def _pre_kernel(*refs, d, latent, sa, sb, sc):
    (x_ref, mod_ref, g1_ref, wa_ref, wb_ref, wcq_ref, wckv_ref, wckr_ref, wdu_ref,
     cqg_ref, ckvg_ref, wuq_ref, wkc_ref, wv_ref) = refs[:14]
    refs = refs[14:]
    if latent:
        cosa_ref, sina_ref, cosc_ref, sinc_ref = refs[:4]
        refs = refs[4:]
        aqt_ref, ak_ref, avt_ref, bq_ref, bk_ref, bv_ref, qct_ref, kc_ref, vct_ref, z_ref = refs
    else:
        (aqt_ref, ak_ref, av_ref, avt_ref, bqt_ref, bk_ref, bv_ref, bvt_ref, qct_ref, kc_ref, vct_ref,
         ckvn_ref, ckr_ref, z_ref) = refs

    x = x_ref[...]
    mod = mod_ref[0]
    shift1, scale1 = mod[:, 0:d], mod[:, d:2 * d]
    h = (_rms(x, g1_ref[...]) * (1.0 + scale1) + shift1).astype(_MM)

    pa = _dot(h, wa_ref[...])
    aq, ak, av = pa[:, 0:HEAD_W], pa[:, HEAD_W:2 * HEAD_W], pa[:, 2 * HEAD_W:3 * HEAD_W]
    if latent:
        cosa, sina = cosa_ref[...], sina_ref[...]
        aq = _rope(aq, cosa, sina)
        ak = _rope(ak, cosa, sina)
    else:
        av_ref[...] = av
    aqt_ref[...] = (aq * (sa * LOG2E)).T.astype(aqt_ref.dtype)
    ak_ref[...] = ak.astype(ak_ref.dtype)
    avt_ref[...] = av.T.astype(avt_ref.dtype)

    pb = _dot(h, wb_ref[...])
    bq, bk, bv = pb[:, 0:HEAD_W], pb[:, HEAD_W:2 * HEAD_W], pb[:, 2 * HEAD_W:3 * HEAD_W]
    bk_ref[...] = bk.astype(bk_ref.dtype)
    bv_ref[...] = bv.astype(bv_ref.dtype)
    if latent:
        bq_ref[...] = (bq * sb).astype(bq_ref.dtype)
    else:
        bqt_ref[...] = (bq * (sb * LOG2E)).T.astype(bqt_ref.dtype)
        bvt_ref[...] = bv.T.astype(bvt_ref.dtype)

    cq = _dot(h, wcq_ref[...])
    qc = _dot(_rms(cq, cqg_ref[...]), wuq_ref[...])
    ckv = _dot(h, wckv_ref[...])
    ckvn = _rms(ckv, ckvg_ref[...])
    kr = _dot(h, wckr_ref[...])
    if latent:
        cosc, sinc = cosc_ref[...], sinc_ref[...]
        qc = _rope(qc, cosc, sinc)
        kr = _rope(kr, cosc, sinc)
    else:
        ckvn_ref[...] = ckvn
        ckr_ref[...] = kr[:, NOPE_C:NOPE_C + ROPE_C]
    qct_ref[...] = (qc * (sc * LOG2E)).T.astype(qct_ref.dtype)
    kc_ref[...] = (_dot(ckvn, wkc_ref[...]) + kr).astype(kc_ref.dtype)
    vct_ref[...] = _dot(ckvn, wv_ref[...]).T.astype(vct_ref.dtype)

    du = _dot(h, wdu_ref[...])
    z_ref[...] = du[:, 0:CONV_CH] * _sigmoid(du[:, CONV_CH:2 * CONV_CH])


def _pre(x2d, mods3, mod_row, lw, *, latent, seq, tables):
    t, d = x2d.shape
    tm = min(512, seq)
    n_seq_tiles = seq // tm
    grid = (t // tm,)
    row = lambda i: (i, 0)
    const = lambda i: (0, 0)
    wnames = ("wa", "wb", "wcq", "wckv", "wckr", "wdu", "cqg", "ckvg", "wuq", "wkc", "wv")
    in_specs = [pl.BlockSpec((tm, d), row),
                pl.BlockSpec((1, 1, 6 * d), lambda i: (mod_row(i * tm), 0, 0)),
                pl.BlockSpec((1, d), const)]
    args = [x2d, mods3, lw["g1"]]
    for n in wnames:
        in_specs.append(pl.BlockSpec(lw[n].shape, const))
        args.append(lw[n])
    if latent:
        for tb in tables:
            in_specs.append(pl.BlockSpec((tm, HEAD_W), lambda i: (i % n_seq_tiles, 0)))
            args.append(tb)
    nat = lambda dt: (jax.ShapeDtypeStruct((t, HEAD_W), dt), pl.BlockSpec((tm, HEAD_W), row))
    tr = (jax.ShapeDtypeStruct((t // seq, HEAD_W, seq), _MM),
          pl.BlockSpec((None, HEAD_W, tm), lambda i: (i // n_seq_tiles, 0, i % n_seq_tiles)))
    if latent:
        outs = [tr, nat(_MM), tr, nat(_MM), nat(_MM), nat(_MM), tr, nat(_MM), tr]
    else:
        outs = [tr, nat(_F32), nat(_F32), tr, tr, nat(_F32), nat(_F32), tr, tr, nat(_MM), tr,
                (jax.ShapeDtypeStruct((t, KV_LORA), _F32), pl.BlockSpec((tm, KV_LORA), row)),
                (jax.ShapeDtypeStruct((t, ROPE_C), _F32), pl.BlockSpec((tm, ROPE_C), row))]
    outs.append((jax.ShapeDtypeStruct((t, CONV_CH), _F32), pl.BlockSpec((tm, CONV_CH), row)))
    out_shape = [o[0] for o in outs]
    out_specs = [o[1] for o in outs]
    kern = functools.partial(_pre_kernel, d=d, latent=latent, sa=DH_A ** -0.5, sb=DH_B ** -0.5,
                             sc=(NOPE_C + ROPE_C) ** -0.5)
    return pl.pallas_call(kern, grid=grid, in_specs=in_specs, out_specs=out_specs, out_shape=out_shape,
                          compiler_params=_cparams(("arbitrary",), 48),
                          name="pre_lat" if latent else "pre_ctx")(*args)


SCORE_LOOKAHEAD = 6
SUM_ROWS = 16


def _attn_kernel(*refs, n_heads, n_maps, dqk, dv, tk, n_own, unroll, tkc, n_cache, diff, lam_init):
    qt_ref, k_ref, vt_ref = refs[:3]
    refs = refs[3:]
    if n_cache:
        kc_ref, vct_ref = refs[:2]
        refs = refs[2:]
    if diff:
        alam_ref, subg_ref = refs[:2]
        refs = refs[2:]
    o_ref, qz_sc, m_sc, acc_sc, ot_sc = refs
    n_chains = n_heads * n_maps

    qz_sc[...] = jnp.zeros(qz_sc.shape, qz_sc.dtype)
    for c in range(n_chains):
        qz_sc[c, c * dqk:(c + 1) * dqk, :] = qt_ref[c * dqk:(c + 1) * dqk, :]
    m_sc[...] = jnp.full(m_sc.shape, -jnp.inf, _F32)
    acc_sc[...] = jnp.zeros(acc_sc.shape, _F32)

    def update(tiles):
        steps = [(ti, c) for ti in range(len(tiles)) for c in range(n_chains)]
        kts, vts = {}, {}

        def scores(ti, c):
            if ti not in kts:
                kts[ti] = tiles[ti][0]().astype(_MM)
            return jnp.dot(kts[ti], qz_sc[c], preferred_element_type=_F32)

        def values(ti, h):
            if (ti, h) not in vts:
                vt = tiles[ti][1](h).astype(_MM)
                vts[ti, h] = jnp.concatenate([vt, jnp.ones((SUM_ROWS, vt.shape[1]), _MM)], axis=0)
            return vts[ti, h]

        pending = [scores(*st) for st in steps[:SCORE_LOOKAHEAD]]
        for idx, (ti, c) in enumerate(steps):
            s = pending.pop(0)
            if idx + SCORE_LOOKAHEAD < len(steps):
                pending.append(scores(*steps[idx + SCORE_LOOKAHEAD]))
            m_prev = m_sc[c]
            m_new = jnp.maximum(m_prev, jnp.max(s, axis=0, keepdims=True))
            alpha = jnp.exp2(m_prev - m_new)
            p = jnp.exp2(s - m_new).astype(_MM)
            acc_sc[c] = alpha * acc_sc[c] + jnp.dot(values(ti, c // n_maps), p, preferred_element_type=_F32)
            m_sc[c] = m_new

    if n_cache:
        update([(lambda j=j: kc_ref[j * tkc:(j + 1) * tkc, :],
                 lambda h, j=j: vct_ref[h * dv:(h + 1) * dv, j * tkc:(j + 1) * tkc]) for j in range(n_cache)])

    def body(j, carry):
        starts = [pl.multiple_of((j * unroll + u) * tk, tk) for u in range(unroll)]
        update([(lambda st=st: k_ref[pl.ds(st, tk), :],
                 lambda h, st=st: vt_ref[h * dv:(h + 1) * dv, pl.ds(st, tk)]) for st in starts])
        return carry

    lax.fori_loop(0, n_own // unroll, body, 0)

    def normalised(c):
        a = acc_sc[c]
        return a[0:dv] / a[dv:dv + 1]

    if diff:
        lv = alam_ref[...]
        lam = (jnp.exp(jnp.sum(lv[0:1] * lv[1:2], axis=-1, keepdims=True))
               - jnp.exp(jnp.sum(lv[2:3] * lv[3:4], axis=-1, keepdims=True)) + lam_init)
    for h in range(n_heads):
        if diff:
            o = normalised(2 * h) - lam * normalised(2 * h + 1)
            o = o * lax.rsqrt(jnp.mean(o * o, axis=0, keepdims=True) + EPS) * subg_ref[...] * (1.0 - lam_init)
        else:
            o = normalised(h)
        ot_sc[h * dv:(h + 1) * dv, :] = o
    o_ref[...] = ot_sc[...].T.astype(o_ref.dtype)


def _attn(qt, k, vt, cache, diff_args, *, n_heads, n_maps, dqk, dv, lam_init=0.0, name):
    nb, s, _ = k.shape
    tq = min(256, s)
    tk = min(256, s)
    n_own = s // tk
    qmap = lambda b, i: (b, 0, i)
    bmap = lambda b, i: (b, 0, 0)
    in_specs = [pl.BlockSpec((None, HEAD_W, tq), qmap),
                pl.BlockSpec((None, s, HEAD_W), bmap),
                pl.BlockSpec((None, HEAD_W, s), bmap)]
    args = [qt, k, vt]
    tkc, n_cache = tk, 0
    if cache is not None:
        past = cache[0].shape[1]
        tkc = min(256, past)
        n_cache = past // tkc
        in_specs += [pl.BlockSpec((None, past, HEAD_W), bmap), pl.BlockSpec((None, HEAD_W, past), bmap)]
        args += list(cache)
    if diff_args is not None:
        in_specs += [pl.BlockSpec(diff_args[0].shape, lambda b, i: (0, 0)),
                     pl.BlockSpec(diff_args[1].shape, lambda b, i: (0, 0))]
        args += list(diff_args)
    n_chains = n_heads * n_maps
    unroll = 2 if n_own % 2 == 0 else 1
    kern = functools.partial(_attn_kernel, n_heads=n_heads, n_maps=n_maps, dqk=dqk, dv=dv, tk=tk, n_own=n_own,
                             unroll=unroll, tkc=tkc, n_cache=n_cache, diff=diff_args is not None, lam_init=lam_init)
    return pl.pallas_call(
        kern, grid=(nb, s // tq), in_specs=in_specs,
        out_specs=pl.BlockSpec((None, tq, HEAD_W), lambda b, i: (b, i, 0)),
        out_shape=jax.ShapeDtypeStruct((nb, s, HEAD_W), _MM),
        scratch_shapes=[pltpu.VMEM((n_chains, HEAD_W, tq), _MM), pltpu.VMEM((n_chains, 1, tq), _F32),
                        pltpu.VMEM((n_chains, dv + SUM_ROWS, tq), _F32), pltpu.VMEM((n_heads * dv, tq), _F32)],
        compiler_params=_cparams(("arbitrary", "arbitrary"), 48),
        name=name,
    )(*args)


def _nbr_kernel(q_ref, k_ref, v_ref, kc_ref, vc_ref, tb_ref, o_ref, *, rows_per_step, rows, span):
    i = pl.program_id(1)

    def row_body(rr, carry):
        r = i * rows_per_step + rr
        rs = jnp.clip(r - WIN_R // 2, 0, rows - WIN_R)
        off = r - rs
        qstart = pl.multiple_of(rr * GRID_W, GRID_W)
        kstart = pl.multiple_of(rs * GRID_W, GRID_W)
        q = q_ref[pl.ds(qstart, GRID_W), :]
        ks = k_ref[pl.ds(kstart, span), :]
        vs = v_ref[pl.ds(kstart, span), :]
        outs = []
        for h in range(H_B):
            hsl = slice(h * DH_B, (h + 1) * DH_B)
            qh = q[:, hsl]
            s_w = _dot_nt(qh, ks[:, hsl]) + tb_ref[h, off]
            s_c = _dot_nt(qh, kc_ref[:, hsl])
            m = jnp.maximum(jnp.max(s_w, axis=-1, keepdims=True), jnp.max(s_c, axis=-1, keepdims=True))
            pw = jnp.exp(s_w - m)
            pc = jnp.exp(s_c - m)
            l = jnp.sum(pw, axis=-1, keepdims=True) + jnp.sum(pc, axis=-1, keepdims=True)
            outs.append((_dot(pc, vc_ref[:, hsl]) + _dot(pw, vs[:, hsl])) / l)
        o_ref[pl.ds(qstart, GRID_W), :] = jnp.concatenate(outs, axis=-1).astype(o_ref.dtype)
        return carry

    lax.fori_loop(0, rows_per_step, row_body, 0)


def _nbr_bias_table(rpb):
    nh, n_r, n_c = rpb.shape
    c = np.arange(GRID_W)[:, None]
    kc = np.arange(GRID_W)[None, :]
    cidx = np.clip(kc - c + WIN_C - 1, 0, n_c - 1).reshape(1, -1)
    onehot = jnp.asarray((cidx == np.arange(n_c)[:, None]).astype(np.float32))
    toep = jnp.dot(rpb.reshape(nh * n_r, n_c).astype(_F32), onehot, precision=lax.Precision.HIGHEST)
    toep = toep.reshape(nh, n_r, GRID_W, GRID_W)
    cstart = np.clip(c - WIN_C // 2, 0, GRID_W - WIN_C)
    valid = (kc >= cstart) & (kc < cstart + WIN_C)
    toep = jnp.where(valid[None, None], toep, -jnp.inf)
    per_off = [toep[:, WIN_R - 1 - off:2 * WIN_R - 1 - off] for off in range(WIN_R)]
    tb = jnp.stack(per_off, axis=1).transpose(0, 1, 3, 2, 4)
    return tb.reshape(nh, WIN_R, GRID_W, WIN_R * GRID_W)


def _nbr(q, k, v, kc, vc, tb):
    nb, s, _ = q.shape
    rows = s // GRID_W
    rows_per_step = min(8, rows)
    span = WIN_R * GRID_W
    past = kc.shape[1]
    tq = rows_per_step * GRID_W
    qmap = lambda b, i: (b, i, 0)
    bmap = lambda b, i: (b, 0, 0)
    kern = functools.partial(_nbr_kernel, rows_per_step=rows_per_step, rows=rows, span=span)
    return pl.pallas_call(
        kern, grid=(nb, rows // rows_per_step),
        in_specs=[pl.BlockSpec((None, tq, HEAD_W), qmap),
                  pl.BlockSpec((None, s, HEAD_W), bmap),
                  pl.BlockSpec((None, s, HEAD_W), bmap),
                  pl.BlockSpec((None, past, HEAD_W), bmap),
                  pl.BlockSpec((None, past, HEAD_W), bmap),
                  pl.BlockSpec(tb.shape, lambda b, i: (0, 0, 0, 0))],
        out_specs=pl.BlockSpec((None, tq, HEAD_W), qmap),
        out_shape=jax.ShapeDtypeStruct((nb, s, HEAD_W), _MM),
        compiler_params=_cparams(("arbitrary", "arbitrary"), 48),
        name="nbr_attn",
    )(q, k, v, kc, vc, tb)


def _mla_cache_kernel(ckv_ref, kr_ref, wkc_ref, wvt_ref, e_ref, kc_ref, vct_ref):
    ckv = ckv_ref[...]
    kc_ref[...] = (_dot(ckv, wkc_ref[...]) + _dot(kr_ref[...], e_ref[...])).astype(kc_ref.dtype)
    vct_ref[...] = _dot_nt(wvt_ref[...], ckv).astype(vct_ref.dtype)


def _mla_cache(ckv, kr, wkc, wvt, expand):
    nb, past, _ = ckv.shape
    bmap = lambda b: (b, 0, 0)
    const = lambda b: (0, 0)
    return pl.pallas_call(
        _mla_cache_kernel, grid=(nb,),
        in_specs=[pl.BlockSpec((None, past, KV_LORA), bmap), pl.BlockSpec((None, past, ROPE_C), bmap),
                  pl.BlockSpec(wkc.shape, const), pl.BlockSpec(wvt.shape, const), pl.BlockSpec(expand.shape, const)],
        out_specs=[pl.BlockSpec((None, past, HEAD_W), bmap), pl.BlockSpec((None, HEAD_W, past), bmap)],
        out_shape=[jax.ShapeDtypeStruct((nb, past, HEAD_W), _MM), jax.ShapeDtypeStruct((nb, HEAD_W, past), _MM)],
        compiler_params=_cparams(("arbitrary",), 32),
        name="mla_cache",
    )(ckv, kr, wkc, wvt, expand)


CONV_HALO = 16


def _conv_kernel(z_ref, w_ref, b_ref, g_ref, beta_ref, o_ref, zp_ref, *, seq, rc):
    zp_ref[0:CONV_HALO, :] = jnp.zeros((CONV_HALO, CONV_CH), _F32)
    zp_ref[CONV_HALO:CONV_HALO + seq, :] = z_ref[...]
    zp_ref[CONV_HALO + seq:2 * CONV_HALO + seq, :] = jnp.zeros((CONV_HALO, CONV_CH), _F32)
    w = w_ref[...]
    shift = CONV_HALO - CONV_K // 2

    def chunk(c, carry):
        base = pl.multiple_of(c * rc, rc)
        win = zp_ref[pl.ds(base, rc + 2 * CONV_HALO), :]
        acc = jnp.zeros((rc, CONV_CH), _F32)
        for j in range(CONV_K):
            acc = acc + win[j + shift:j + shift + rc, :] * w[j:j + 1, :]
        y = acc + b_ref[...]
        mu = jnp.mean(y, axis=-1, keepdims=True)
        yc = y - mu
        var = jnp.mean(yc * yc, axis=-1, keepdims=True)
        yn = yc * lax.rsqrt(var + EPS) * g_ref[...] + beta_ref[...]
        o_ref[pl.ds(base, rc), :] = (yn * _sigmoid(yn)).astype(o_ref.dtype)
        return carry

    lax.fori_loop(0, seq // rc, chunk, 0)


def _conv(z, w, b, g, beta):
    nb, seq, _ = z.shape
    rc = min(128, seq)
    bmap = lambda i: (i, 0, 0)
    const = lambda i: (0, 0)
    kern = functools.partial(_conv_kernel, seq=seq, rc=rc)
    return pl.pallas_call(
        kern, grid=(nb,),
        in_specs=[pl.BlockSpec((None, seq, CONV_CH), bmap), pl.BlockSpec(w.shape, const),
                  pl.BlockSpec(b.shape, const), pl.BlockSpec(g.shape, const), pl.BlockSpec(beta.shape, const)],
        out_specs=pl.BlockSpec((None, seq, CONV_CH), bmap),
        out_shape=jax.ShapeDtypeStruct((nb, seq, CONV_CH), _MM),
        scratch_shapes=[pltpu.VMEM((seq + 2 * CONV_HALO, CONV_CH), _F32)],
        compiler_params=_cparams(("arbitrary",), 40),
        name="conv",
    )(z, w, b, g, beta)


def _post_kernel(x_ref, mod_ref, oa_ref, ob_ref, oc_ref, od_ref, g1_ref, wg_ref, bg_ref,
                 awo_ref, bwo_ref, cwo_ref, dwo_ref, wout_ref, g2_ref, rhi_ref, rlo_ref, rb_ref,
                 x1_ref, h2_ref, dw_ref, *, d):
    x = x_ref[...]
    mod = mod_ref[0]
    shift1, scale1, gate1 = mod[:, 0:d], mod[:, d:2 * d], mod[:, 2 * d:3 * d]
    shift2, scale2 = mod[:, 3 * d:4 * d], mod[:, 4 * d:5 * d]
    h = (_rms(x, g1_ref[...]) * (1.0 + scale1) + shift1).astype(_MM)

    merged = None
    for bi, (o_ref, wo_ref) in enumerate(((oa_ref, awo_ref), (ob_ref, bwo_ref), (oc_ref, cwo_ref), (od_ref, dwo_ref))):
        gate = _sigmoid(_dot(h, wg_ref[:, bi * d:(bi + 1) * d]) + bg_ref[:, bi * d:(bi + 1) * d])
        term = gate * _dot(o_ref[...], wo_ref[...])
        merged = term if merged is None else merged + term
    x1 = x + gate1 * _dot(merged, wout_ref[...])
    x1_ref[...] = x1

    h2 = _rms(x1, g2_ref[...]) * (1.0 + scale2) + shift2
    h2_hi = h2.astype(_MM)
    h2_lo = (h2 - h2_hi.astype(_F32)).astype(_MM)
    h2_ref[...] = h2_hi
    r = (_dot(h2_hi, rhi_ref[...]) + _dot(h2_lo, rhi_ref[...]) + _dot(h2_hi, rlo_ref[...])) + rb_ref[...]
    lgp, lep = r[:, 0:LANES], r[:, LANES:2 * LANES]
    lane = lax.broadcasted_iota(jnp.int32, lgp.shape, 1)
    neg = -jnp.inf
    lg = jnp.where(lane < N_GROUPS, lgp, neg)
    mg = jnp.max(lg, axis=-1, keepdims=True)
    gi = jnp.min(jnp.where(lg == mg, lane, LANES), axis=-1, keepdims=True)
    gw = 1.0 / jnp.sum(jnp.exp(lg - mg), axis=-1, keepdims=True)
    in_group = (lane < N_EXPERTS) & ((lane // EXPERTS_PER_GROUP) == gi)
    le = jnp.where(in_group, lep, neg)
    t1 = jnp.max(le, axis=-1, keepdims=True)
    i1 = jnp.min(jnp.where(le == t1, lane, LANES), axis=-1, keepdims=True)
    le2 = jnp.where(lane == i1, neg, le)
    t2 = jnp.max(le2, axis=-1, keepdims=True)
    i2 = jnp.min(jnp.where(le2 == t2, lane, LANES), axis=-1, keepdims=True)
    e2 = jnp.exp(t2 - t1)
    w1 = gw / (1.0 + e2)
    w2 = gw * e2 / (1.0 + e2)
    dw_ref[...] = jnp.where(lane == i1, w1, 0.0) + jnp.where(lane == i2, w2, 0.0)


def _post(x2d, mods3, mod_row, oa, ob, oc, od, lw):
    t, d = x2d.shape
    tm = min(512, t)
    row = lambda i: (i, 0)
    const = lambda i: (0, 0)
    wnames = ("g1", "wgate", "bgate", "awo", "bwo", "cwo", "dwo", "wout", "g2", "rhi", "rlo", "rb")
    in_specs = [pl.BlockSpec((tm, d), row), pl.BlockSpec((1, 1, 6 * d), lambda i: (mod_row(i * tm), 0, 0))]
    in_specs += [pl.BlockSpec((tm, HEAD_W), row)] * 4
    in_specs += [pl.BlockSpec(lw[n].shape, const) for n in wnames]
    args = [x2d, mods3, oa, ob, oc, od] + [lw[n] for n in wnames]
    return pl.pallas_call(
        functools.partial(_post_kernel, d=d), grid=(t // tm,), in_specs=in_specs,
        out_specs=[pl.BlockSpec((tm, d), row), pl.BlockSpec((tm, d), row), pl.BlockSpec((tm, LANES), row)],
        out_shape=[jax.ShapeDtypeStruct((t, d), _F32), jax.ShapeDtypeStruct((t, d), _MM),
                   jax.ShapeDtypeStruct((t, LANES), _F32)],
        compiler_params=_cparams(("arbitrary",), 56),
        name="post",
    )(*args)


def _moe_kernel(h2_ref, dw_ref, x1_ref, mod_ref, w1_ref, w2_ref, fg_ref, o_ref, acc_ref, *, d, final):
    e = pl.program_id(1)

    @pl.when(e == 0)
    def _():
        acc_ref[...] = jnp.zeros_like(acc_ref)

    hid = _dot(h2_ref[...], w1_ref[0])
    gate, up = hid[:, 0:D_EXPERT], hid[:, D_EXPERT:2 * D_EXPERT]
    dw = dw_ref[...]
    lane = lax.broadcasted_iota(jnp.int32, dw.shape, 1)
    we = jnp.sum(jnp.where(lane == e, dw, 0.0), axis=-1, keepdims=True)
    act = gate * _sigmoid(gate) * up * we
    acc_ref[...] += _dot(act, w2_ref[0])

    @pl.when(e == pl.num_programs(1) - 1)
    def _():
        gate2 = mod_ref[0][:, 5 * d:6 * d]
        x2 = x1_ref[...] + gate2 * acc_ref[...]
        if final:
            x2 = _rms(x2, fg_ref[...])
        o_ref[...] = x2


def _moe(h2, dw, x1, mods3, mod_row, w1, w2, fg, *, final):
    t, d = x1.shape
    tm = min(1024, t)
    n_e = w1.shape[0]
    row = lambda i, e: (i, 0)
    return pl.pallas_call(
        functools.partial(_moe_kernel, d=d, final=final), grid=(t // tm, n_e),
        in_specs=[pl.BlockSpec((tm, d), row), pl.BlockSpec((tm, LANES), row), pl.BlockSpec((tm, d), row),
                  pl.BlockSpec((1, 1, 6 * d), lambda i, e: (mod_row(i * tm), 0, 0)),
                  pl.BlockSpec((1,) + w1.shape[1:], lambda i, e: (e, 0, 0)),
                  pl.BlockSpec((1,) + w2.shape[1:], lambda i, e: (e, 0, 0)),
                  pl.BlockSpec((1, d), lambda i, e: (0, 0))],
        out_specs=pl.BlockSpec((tm, d), row),
        out_shape=jax.ShapeDtypeStruct((t, d), _F32),
        scratch_shapes=[pltpu.VMEM((tm, d), _F32)],
        compiler_params=_cparams(("arbitrary", "arbitrary"), 56),
        name="moe_final" if final else "moe",
    )(h2, dw, x1, mods3, w1, w2, fg)


def _rope_tables(n_tok, dim):
    hq = dim // 4
    tpos = np.arange(n_tok)
    rows = (tpos // GRID_W).astype(np.float32)
    cols = (tpos % GRID_W).astype(np.float32)
    inv = jnp.asarray(ROPE_THETA, _F32) ** (-jnp.arange(hq, dtype=_F32) / hq)
    ar = jnp.asarray(rows)[:, None] * inv[None, :]
    ac = jnp.asarray(cols)[:, None] * inv[None, :]
    ang = jnp.concatenate([ar, ar, ac, ac], axis=-1)
    return jnp.cos(ang), jnp.sin(ang)


def _layer_weights(l, p):
    mm = lambda a: a.astype(_MM)
    w_in = p["w_in"][l]
    pts = np.cumsum([0, 2 * H_A * DH_A, 2 * H_A * DH_A, 2 * H_A * DH_A, H_B * DH_B, H_B * DH_B, H_B * DH_B,
                     Q_LORA, KV_LORA, ROPE_C, 2 * CONV_CH])
    seg = lambda i, j: w_in[:, pts[i]:pts[j]]
    d = w_in.shape[0]
    wckr = seg(8, 9)
    zeros_n = jnp.zeros((d, NOPE_C), w_in.dtype)
    wckr_t = jnp.concatenate([zeros_n, wckr] * H_C, axis=1)
    wukv = p["c_wukv"][l].reshape(KV_LORA, H_C, NOPE_C + DV_C)
    wkc = jnp.concatenate([wukv[:, :, :NOPE_C], jnp.zeros((KV_LORA, H_C, ROPE_C), wukv.dtype)], axis=-1)
    wkc = wkc.reshape(KV_LORA, H_C * (NOPE_C + ROPE_C))
    wv = wukv[:, :, NOPE_C:].reshape(KV_LORA, H_C * DV_C)
    router = jnp.zeros((d, 2 * LANES), _F32)
    router = router.at[:, 0:N_GROUPS].set(p["moe_w_group"][l]).at[:, LANES:LANES + N_EXPERTS].set(p["moe_w_expert"][l])
    rhi = router.astype(_MM)
    rlo = (router - rhi.astype(_F32)).astype(_MM)
    rb = jnp.zeros((1, 2 * LANES), _F32)
    rb = rb.at[0, 0:N_GROUPS].set(p["moe_b_group"][l]).at[0, LANES:LANES + N_EXPERTS].set(p["moe_b_expert"][l])
    conv_w = p["d_conv_w"][l]
    return dict(
        g1=p["norm1_g"][l][None], g2=p["norm2_g"][l][None],
        wa=mm(seg(0, 3)), wb=mm(seg(3, 6)), wcq=mm(seg(6, 7)), wckv=mm(seg(7, 8)), wckr=mm(wckr_t), wdu=mm(seg(9, 10)),
        cqg=p["c_q_norm_g"][l][None], ckvg=p["c_kv_norm_g"][l][None],
        wuq=mm(p["c_wuq"][l]), wkc=mm(wkc), wv=mm(wv), wvt=mm(wv.T),
        alam=p["a_lambda"][l], subg=p["a_subln_g"][l][:, None],
        wgate=mm(p["w_gate"][l]), bgate=p["b_gate"][l][None],
        awo=mm(p["a_wo"][l]), bwo=mm(p["b_wo"][l]), cwo=mm(p["c_wo"][l]), dwo=mm(p["d_wo"][l]),
        wout=mm(p["w_out"][l]), rhi=rhi, rlo=rlo, rb=rb,
        conv_w=conv_w, conv_b=p["d_conv_b"][l][None], ln_g=p["d_ln_g"][l][None], ln_b=p["d_ln_b"][l][None],
        w1=mm(p["moe_w1"][l]), w2=mm(p["moe_w2"][l]),
        tb=_nbr_bias_table(p["b_rpb"][l]),
    )


def _layer(x2d, mods3, mod_row, lw, l, n_layers, *, nb, seq, fg, ctx, tables):
    latent = ctx is not None
    t = x2d.shape[0]
    outs = _pre(x2d, mods3, mod_row, lw, latent=latent, seq=seq, tables=tables)
    r3 = lambda a: a.reshape(nb, seq, a.shape[-1])
    lam_init = 0.8 - 0.6 * math.exp(-0.3 * l)
    diff_args = (lw["alam"], lw["subg"])
    if latent:
        aqt, ak, avt, bq, bk, bv, qct, kc, vct, z = outs
        own = None
        a_k_c, a_vt_c, b_k_c, b_v_c, c_k_c, c_vt_c = ctx
        oa = _attn(aqt, r3(ak), avt, (a_k_c, a_vt_c), diff_args, n_heads=H_A, n_maps=2,
                   dqk=DH_A, dv=2 * DH_A, lam_init=lam_init, name="attn_a_lat")
        ob = _nbr(r3(bq), r3(bk), r3(bv), b_k_c, b_v_c, lw["tb"])
        oc = _attn(qct, r3(kc), vct, (c_k_c, c_vt_c), None, n_heads=H_C, n_maps=1,
                   dqk=NOPE_C + ROPE_C, dv=DV_C, name="attn_c_lat")
    else:
        aqt, ak, av, avt, bqt, bk, bv, bvt, qct, kc, vct, ckvn, ckr, z = outs
        own = (ak, av, bk, bv, ckvn, ckr)
        oa = _attn(aqt, r3(ak), avt, None, diff_args, n_heads=H_A, n_maps=2,
                   dqk=DH_A, dv=2 * DH_A, lam_init=lam_init, name="attn_a_ctx")
        ob = _attn(bqt, r3(bk), bvt, None, None, n_heads=H_B, n_maps=1, dqk=DH_B, dv=DH_B, name="attn_b_ctx")
        oc = _attn(qct, r3(kc), vct, None, None, n_heads=H_C, n_maps=1,
                   dqk=NOPE_C + ROPE_C, dv=DV_C, name="attn_c_ctx")
    od = _conv(r3(z), lw["conv_w"], lw["conv_b"], lw["ln_g"], lw["ln_b"])
    f2 = lambda a: a.reshape(t, a.shape[-1])
    x1, h2, dw = _post(x2d, mods3, mod_row, f2(oa), f2(ob), f2(oc), f2(od), lw)
    x2 = _moe(h2, dw, x1, mods3, mod_row, lw["w1"], lw["w2"], fg, final=(l == n_layers - 1))
    return x2, own


def kernel(x_prompt, x_sample, cache_a_k, cache_a_v, cache_b_k, cache_b_v, cache_c_kv, cache_c_krope, c, c_ctx, norm1_g, norm2_g, w_ada, b_ada, w_in, a_lambda, a_subln_g, a_wo, b_rpb, b_wo, c_q_norm_g, c_kv_norm_g, c_wuq, c_wukv, c_wo, d_conv_w, d_conv_b, d_ln_g, d_ln_b, d_wo, w_gate, b_gate, w_out, moe_w_group, moe_b_group, moe_w_expert, moe_b_expert, moe_w1, moe_w2, final_norm_g):
    p = dict(norm1_g=norm1_g, norm2_g=norm2_g, w_in=w_in, a_lambda=a_lambda, a_subln_g=a_subln_g, a_wo=a_wo,
             b_rpb=b_rpb, b_wo=b_wo, c_q_norm_g=c_q_norm_g, c_kv_norm_g=c_kv_norm_g, c_wuq=c_wuq, c_wukv=c_wukv,
             c_wo=c_wo, d_conv_w=d_conv_w, d_conv_b=d_conv_b, d_ln_g=d_ln_g, d_ln_b=d_ln_b, d_wo=d_wo,
             w_gate=w_gate, b_gate=b_gate, w_out=w_out, moe_w_group=moe_w_group, moe_b_group=moe_b_group,
             moe_w_expert=moe_w_expert, moe_b_expert=moe_b_expert, moe_w1=moe_w1, moe_w2=moe_w2)
    n_layers = w_in.shape[0]
    bc, sc, d = x_prompt.shape
    bl, sl, _ = x_sample.shape
    past = cache_a_k.shape[2]
    assert bl + 1 <= 8 and sl % GRID_W == 0

    cond8 = jnp.zeros((8, d), _F32).at[0].set(c_ctx).at[1:1 + bl].set(c)
    mods3 = _adaln(cond8, w_ada, b_ada).reshape(n_layers * 8, 1, 6 * d)
    fg = final_norm_g[None]

    cos_a, sin_a = _rope_tables(sl, DH_A)
    cos_c, sin_c = _rope_tables(sl, ROPE_C)
    reps_a = HEAD_W // DH_A
    ones_n, zeros_n = jnp.ones((sl, NOPE_C), _F32), jnp.zeros((sl, NOPE_C), _F32)
    tables = (jnp.tile(cos_a, (1, reps_a)), jnp.tile(sin_a, (1, reps_a)),
              jnp.concatenate([ones_n, cos_c] * H_C, axis=1), jnp.concatenate([zeros_n, sin_c] * H_C, axis=1))
    expand = np.zeros((ROPE_C, HEAD_W), np.float32)
    for hh in range(H_C):
        expand[np.arange(ROPE_C), hh * (NOPE_C + ROPE_C) + NOPE_C + np.arange(ROPE_C)] = 1.0
    expand = jnp.asarray(expand, _MM)

    lws = [_layer_weights(l, p) for l in range(n_layers)]

    xp = x_prompt.reshape(bc * sc, d)
    states = []
    for l in range(n_layers):
        xp, own = _layer(xp, mods3, (lambda tok, l=l: l * 8), lws[l], l, n_layers, nb=bc, seq=sc, fg=fg, ctx=None,
                         tables=None)
        states.append(own)
    y_prompt = xp.reshape(bc, sc, d)

    xs = x_sample.reshape(bl * sl, d)
    for l in range(n_layers):
        flat = lambda a: a[:, l].reshape(bl, past, -1)
        c_k_c, c_vt_c = _mla_cache(flat(cache_c_kv), flat(cache_c_krope), lws[l]["wkc"], lws[l]["wvt"], expand)
        ctx = (flat(cache_a_k).astype(_MM), jnp.swapaxes(flat(cache_a_v), 1, 2).astype(_MM),
               flat(cache_b_k).astype(_MM), flat(cache_b_v).astype(_MM), c_k_c, c_vt_c)
        xs, _ = _layer(xs, mods3, (lambda tok, l=l: l * 8 + 1 + tok // sl), lws[l], l, n_layers,
                       nb=bl, seq=sl, fg=fg, ctx=ctx, tables=tables)
    y_sample = xs.reshape(bl, sl, d)

    st = lambda k, shape: jnp.stack([s[k].reshape(shape) for s in states], axis=1)
    new_a_k = st(0, (bc, sc, H_A, 2 * DH_A))
    new_a_v = st(1, (bc, sc, H_A, 2 * DH_A))
    new_b_k = st(2, (bc, sc, H_B, DH_B))
    new_b_v = st(3, (bc, sc, H_B, DH_B))
    new_c_kv = st(4, (bc, sc, KV_LORA))
    new_c_krope = st(5, (bc, sc, ROPE_C))
    return (y_prompt, y_sample, new_a_k, new_a_v, new_b_k, new_b_v, new_c_kv, new_c_krope)
```

```python
import functools
import math

import numpy as np
import jax
import jax.numpy as jnp
from jax import lax
from jax.experimental import pallas as pl
from jax.experimental.pallas import tpu as pltpu

GRID_W = 64
EPS = 1e-6
ROPE_THETA = 10000.0
H_A, DH_A = 4, 32
H_B, DH_B = 4, 64
WIN_R, WIN_C = 8, 16
H_C, Q_LORA, KV_LORA, NOPE_C, ROPE_C, DV_C = 4, 256, 128, 32, 32, 64
CONV_CH, CONV_K = 256, 31
N_BRANCH = 4
N_GROUPS, EXPERTS_PER_GROUP, N_EXPERTS, D_EXPERT = 4, 4, 16, 256
HEAD_W = 256
LANES = 128
V7X_VMEM_BYTES = 64 * 1024 * 1024
LOG2E = math.log2(math.e)

_MM = jnp.bfloat16
_F32 = jnp.float32


def _cparams(sem, vmem_mb):
    return pltpu.CompilerParams(dimension_semantics=sem,
                                vmem_limit_bytes=min(vmem_mb * 1024 * 1024, V7X_VMEM_BYTES - 8 * 1024 * 1024))


def _dot(a, b):
    return jnp.dot(a.astype(_MM), b.astype(_MM), preferred_element_type=_F32)


def _dot_nt(a, b):
    return lax.dot_general(a.astype(_MM), b.astype(_MM), (((1,), (1,)), ((), ())),
                           preferred_element_type=_F32)


def _sigmoid(x):
    return 1.0 / (1.0 + jnp.exp(-x))


def _rms(x, g):
    return x * lax.rsqrt(jnp.mean(x * x, axis=-1, keepdims=True) + EPS) * g


def _rope(x, cos, sin):
    w = x.shape[-1]
    lane = lax.broadcasted_iota(jnp.int32, x.shape, 1)
    first = (lane % 16) < 8
    fwd = pltpu.roll(x, w - 8, axis=1)
    bwd = pltpu.roll(x, 8, axis=1)
    return x * cos + jnp.where(first, -fwd, bwd) * sin


def _adaln_kernel(cond_ref, w_ref, b_ref, o_ref):
    c = cond_ref[...]
    a = c * _sigmoid(c)
    o_ref[0] = _dot(a, w_ref[0]) + b_ref[0]


def _adaln(cond8, w_ada, b_ada):
    n_l, d, d6 = w_ada.shape
    tn = 1536
    return pl.pallas_call(
        _adaln_kernel,
        grid=(n_l, d6 // tn),
        in_specs=[pl.BlockSpec((8, d), lambda l, j: (0, 0)),
                  pl.BlockSpec((1, d, tn), lambda l, j: (l, 0, j)),
                  pl.BlockSpec((1, 1, tn), lambda l, j: (l, 0, j))],
        out_specs=pl.BlockSpec((1, 8, tn), lambda l, j: (l, 0, j)),
        out_shape=jax.ShapeDtypeStruct((n_l, 8, d6), _F32),
        compiler_params=_cparams(("arbitrary", "arbitrary"), 40),
        name="adaln",
    )(cond8, w_ada, b_ada.reshape(n_l, 1, d6))


def _pre_kernel(*refs, d, latent, sa, sb, sc):
    (x_ref, mod_ref, g1_ref, wa_ref, wb_ref, wcq_ref, wckv_ref, wckr_ref, wdu_ref,
     cqg_ref, ckvg_ref, wuq_ref, wkc_ref, wv_ref) = refs[:14]
    refs = refs[14:]
    if latent:
        cosa_ref, sina_ref, cosc_ref, sinc_ref = refs[:4]
        refs = refs[4:]
        aqt_ref, ak_ref, avt_ref, bqt_ref, bk_ref, bvt_ref, qct_ref, kc_ref, vct_ref, z_ref = refs
    else:
        (aqt_ref, ak_ref, av_ref, avt_ref, bqt_ref, bk_ref, bv_ref, bvt_ref, qct_ref, kc_ref, vct_ref,
         ckvn_ref, ckr_ref, z_ref) = refs

    x = x_ref[...]
    mod = mod_ref[0]
    shift1, scale1 = mod[:, 0:d], mod[:, d:2 * d]
    h = (_rms(x, g1_ref[...]) * (1.0 + scale1) + shift1).astype(_MM)

    pa = _dot(h, wa_ref[...])
    aq, ak, av = pa[:, 0:HEAD_W], pa[:, HEAD_W:2 * HEAD_W], pa[:, 2 * HEAD_W:3 * HEAD_W]
    if latent:
        cosa, sina = cosa_ref[...], sina_ref[...]
        aq = _rope(aq, cosa, sina)
        ak = _rope(ak, cosa, sina)
    else:
        av_ref[...] = av
    aqt_ref[...] = (aq * (sa * LOG2E)).T.astype(aqt_ref.dtype)
    ak_ref[...] = ak.astype(ak_ref.dtype)
    avt_ref[...] = av.T.astype(avt_ref.dtype)

    pb = _dot(h, wb_ref[...])
    bq, bk, bv = pb[:, 0:HEAD_W], pb[:, HEAD_W:2 * HEAD_W], pb[:, 2 * HEAD_W:3 * HEAD_W]
    bk_ref[...] = bk.astype(bk_ref.dtype)
    if not latent:
        bv_ref[...] = bv
    bqt_ref[...] = (bq * (sb * LOG2E)).T.astype(bqt_ref.dtype)
    bvt_ref[...] = bv.T.astype(bvt_ref.dtype)

    cq = _dot(h, wcq_ref[...])
    qc = _dot(_rms(cq, cqg_ref[...]), wuq_ref[...])
    ckv = _dot(h, wckv_ref[...])
    ckvn = _rms(ckv, ckvg_ref[...])
    kr = _dot(h, wckr_ref[...])
    if latent:
        cosc, sinc = cosc_ref[...], sinc_ref[...]
        qc = _rope(qc, cosc, sinc)
        kr = _rope(kr, cosc, sinc)
    else:
        ckvn_ref[...] = ckvn
        ckr_ref[...] = kr[:, NOPE_C:NOPE_C + ROPE_C]
    qct_ref[...] = (qc * (sc * LOG2E)).T.astype(qct_ref.dtype)
    kc_ref[...] = (_dot(ckvn, wkc_ref[...]) + kr).astype(kc_ref.dtype)
    vct_ref[...] = _dot(ckvn, wv_ref[...]).T.astype(vct_ref.dtype)

    du = _dot(h, wdu_ref[...])
    z_ref[...] = du[:, 0:CONV_CH] * _sigmoid(du[:, CONV_CH:2 * CONV_CH])


def _pre(x2d, mods3, mod_row, lw, *, latent, seq, tables):
    t, d = x2d.shape
    tm = min(512, seq)
    n_seq_tiles = seq // tm
    grid = (t // tm,)
    row = lambda i: (i, 0)
    const = lambda i: (0, 0)
    wnames = ("wa", "wb", "wcq", "wckv", "wckr", "wdu", "cqg", "ckvg", "wuq", "wkc", "wv")
    in_specs = [pl.BlockSpec((tm, d), row),
                pl.BlockSpec((1, 1, 6 * d), lambda i: (mod_row(i * tm), 0, 0)),
                pl.BlockSpec((1, d), const)]
    args = [x2d, mods3, lw["g1"]]
    for n in wnames:
        in_specs.append(pl.BlockSpec(lw[n].shape, const))
        args.append(lw[n])
    if latent:
        for tb in tables:
            in_specs.append(pl.BlockSpec((tm, HEAD_W), lambda i: (i % n_seq_tiles, 0)))
            args.append(tb)
    nat = lambda dt: (jax.ShapeDtypeStruct((t, HEAD_W), dt), pl.BlockSpec((tm, HEAD_W), row))
    tr = (jax.ShapeDtypeStruct((t // seq, HEAD_W, seq), _MM),
          pl.BlockSpec((None, HEAD_W, tm), lambda i: (i // n_seq_tiles, 0, i % n_seq_tiles)))
    if latent:
        outs = [tr, nat(_MM), tr, tr, nat(_MM), tr, tr, nat(_MM), tr]
    else:
        outs = [tr, nat(_F32), nat(_F32), tr, tr, nat(_F32), nat(_F32), tr, tr, nat(_MM), tr,
                (jax.ShapeDtypeStruct((t, KV_LORA), _F32), pl.BlockSpec((tm, KV_LORA), row)),
                (jax.ShapeDtypeStruct((t, ROPE_C), _F32), pl.BlockSpec((tm, ROPE_C), row))]
    outs.append((jax.ShapeDtypeStruct((t, CONV_CH), _F32), pl.BlockSpec((tm, CONV_CH), row)))
    out_shape = [o[0] for o in outs]
    out_specs = [o[1] for o in outs]
    kern = functools.partial(_pre_kernel, d=d, latent=latent, sa=DH_A ** -0.5, sb=DH_B ** -0.5,
                             sc=(NOPE_C + ROPE_C) ** -0.5)
    return pl.pallas_call(kern, grid=grid, in_specs=in_specs, out_specs=out_specs, out_shape=out_shape,
                          compiler_params=_cparams(("arbitrary",), 48),
                          name="pre_lat" if latent else "pre_ctx")(*args)


SCORE_LOOKAHEAD = 6
SUM_ROWS = 16


def _attn_kernel(*refs, n_heads, n_maps, dqk, dv, tk, n_own, unroll, tkc, n_cache, diff, lam_init):
    qt_ref, k_ref, vt_ref = refs[:3]
    refs = refs[3:]
    if n_cache:
        kc_ref, vct_ref = refs[:2]
        refs = refs[2:]
    if diff:
        alam_ref, subg_ref = refs[:2]
        refs = refs[2:]
    o_ref, qz_sc, m_sc, acc_sc, ot_sc = refs
    n_chains = n_heads * n_maps

    qz_sc[...] = jnp.zeros(qz_sc.shape, qz_sc.dtype)
    for c in range(n_chains):
        qz_sc[c, (c * dqk) % LANES:(c * dqk) % LANES + dqk, :] = qt_ref[c * dqk:(c + 1) * dqk, :]
    m_sc[...] = jnp.full(m_sc.shape, -jnp.inf, _F32)
    acc_sc[...] = jnp.zeros(acc_sc.shape, _F32)

    def update(tiles):
        steps = [(ti, c) for ti in range(len(tiles)) for c in range(n_chains)]
        kts, vts = {}, {}

        def scores(ti, c):
            if ti not in kts:
                kts[ti] = tiles[ti][0]().astype(_MM)
            grp = (c * dqk) // LANES
            return jnp.dot(kts[ti][:, grp * LANES:(grp + 1) * LANES], qz_sc[c], preferred_element_type=_F32)

        def values(ti, h):
            if (ti, h) not in vts:
                vt = tiles[ti][1](h).astype(_MM)
                vts[ti, h] = jnp.concatenate([vt, jnp.ones((SUM_ROWS, vt.shape[1]), _MM)], axis=0)
            return vts[ti, h]

        pending = [scores(*st) for st in steps[:SCORE_LOOKAHEAD]]
        for idx, (ti, c) in enumerate(steps):
            s = pending.pop(0)
            if idx + SCORE_LOOKAHEAD < len(steps):
                pending.append(scores(*steps[idx + SCORE_LOOKAHEAD]))
            m_prev = m_sc[c]
            m_new = jnp.maximum(m_prev, jnp.max(s, axis=0, keepdims=True))
            alpha = jnp.exp2(m_prev - m_new)
            p = jnp.exp2(s - m_new).astype(_MM)
            acc_sc[c] = alpha * acc_sc[c] + jnp.dot(values(ti, c // n_maps), p, preferred_element_type=_F32)
            m_sc[c] = m_new

    if n_cache:
        update([(lambda j=j: kc_ref[j * tkc:(j + 1) * tkc, :],
                 lambda h, j=j: vct_ref[h * dv:(h + 1) * dv, j * tkc:(j + 1) * tkc]) for j in range(n_cache)])

    def body(j, carry):
        starts = [pl.multiple_of((j * unroll + u) * tk, tk) for u in range(unroll)]
        update([(lambda st=st: k_ref[pl.ds(st, tk), :],
                 lambda h, st=st: vt_ref[h * dv:(h + 1) * dv, pl.ds(st, tk)]) for st in starts])
        return carry

    lax.fori_loop(0, n_own // unroll, body, 0)

    def normalised(c):
        a = acc_sc[c]
        return a[0:dv] / a[dv:dv + 1]

    if diff:
        lv = alam_ref[...]
        lam = (jnp.exp(jnp.sum(lv[0:1] * lv[1:2], axis=-1, keepdims=True))
               - jnp.exp(jnp.sum(lv[2:3] * lv[3:4], axis=-1, keepdims=True)) + lam_init)
    for h in range(n_heads):
        if diff:
            o = normalised(2 * h) - lam * normalised(2 * h + 1)
            o = o * lax.rsqrt(jnp.mean(o * o, axis=0, keepdims=True) + EPS) * subg_ref[...] * (1.0 - lam_init)
        else:
            o = normalised(h)
        ot_sc[h * dv:(h + 1) * dv, :] = o
    o_ref[...] = ot_sc[...].T.astype(o_ref.dtype)


def _attn(qt, k, vt, cache, diff_args, *, n_heads, n_maps, dqk, dv, lam_init=0.0, name):
    nb, s, _ = k.shape
    tq = min(256, s)
    tk = min(256, s)
    n_own = s // tk
    qmap = lambda b, i: (b, 0, i)
    bmap = lambda b, i: (b, 0, 0)
    in_specs = [pl.BlockSpec((None, HEAD_W, tq), qmap),
                pl.BlockSpec((None, s, HEAD_W), bmap),
                pl.BlockSpec((None, HEAD_W, s), bmap)]
    args = [qt, k, vt]
    tkc, n_cache = tk, 0
    if cache is not None:
        past = cache[0].shape[1]
        tkc = min(256, past)
        n_cache = past // tkc
        in_specs += [pl.BlockSpec((None, past, HEAD_W), bmap), pl.BlockSpec((None, HEAD_W, past), bmap)]
        args += list(cache)
    if diff_args is not None:
        in_specs += [pl.BlockSpec(diff_args[0].shape, lambda b, i: (0, 0)),
                     pl.BlockSpec(diff_args[1].shape, lambda b, i: (0, 0))]
        args += list(diff_args)
    n_chains = n_heads * n_maps
    unroll = 8 if n_own % 8 == 0 else (4 if n_own % 4 == 0 else 1)
    kern = functools.partial(_attn_kernel, n_heads=n_heads, n_maps=n_maps, dqk=dqk, dv=dv, tk=tk, n_own=n_own,
                             unroll=unroll, tkc=tkc, n_cache=n_cache, diff=diff_args is not None, lam_init=lam_init)
    return pl.pallas_call(
        kern, grid=(nb, s // tq), in_specs=in_specs,
        out_specs=pl.BlockSpec((None, tq, HEAD_W), lambda b, i: (b, i, 0)),
        out_shape=jax.ShapeDtypeStruct((nb, s, HEAD_W), _MM),
        scratch_shapes=[pltpu.VMEM((n_chains, LANES, tq), _MM), pltpu.VMEM((n_chains, 1, tq), _F32),
                        pltpu.VMEM((n_chains, dv + SUM_ROWS, tq), _F32), pltpu.VMEM((n_heads * dv, tq), _F32)],
        compiler_params=_cparams(("arbitrary", "arbitrary"), 48),
        name=name,
    )(*args)


NBR_ROWS = 4
NBR_SPAN_ROWS = 12


def _nbr_kernel(qt_ref, k_ref, vt_ref, kc_ref, vct_ref, tb_ref, o_ref, qz_sc, ot_sc, *, rows):
    i = pl.program_id(1)
    ss = jnp.clip(i * NBR_ROWS - WIN_R // 2, 0, rows - NBR_SPAN_ROWS)
    kstart = pl.multiple_of(ss * GRID_W, NBR_ROWS * GRID_W)
    span = NBR_SPAN_ROWS * GRID_W

    qz_sc[...] = jnp.zeros(qz_sc.shape, qz_sc.dtype)
    for h in range(H_B):
        qz_sc[h, (h * DH_B) % LANES:(h * DH_B) % LANES + DH_B, :] = qt_ref[h * DH_B:(h + 1) * DH_B, :]
    ks = k_ref[pl.ds(kstart, span), :].astype(_MM)
    kc = kc_ref[...].astype(_MM)

    def scores(h):
        gsl = slice((h * DH_B) // LANES * LANES, ((h * DH_B) // LANES + 1) * LANES)
        return (jnp.dot(ks[:, gsl], qz_sc[h], preferred_element_type=_F32),
                jnp.dot(kc[:, gsl], qz_sc[h], preferred_element_type=_F32))

    def with_ones(vt):
        return jnp.concatenate([vt.astype(_MM), jnp.ones((SUM_ROWS, vt.shape[1]), _MM)], axis=0)

    pending = scores(0)
    for h in range(H_B):
        s_w, s_c = pending
        if h + 1 < H_B:
            pending = scores(h + 1)
        s_w = s_w + tb_ref[h]
        m = jnp.maximum(jnp.max(s_w, axis=0, keepdims=True), jnp.max(s_c, axis=0, keepdims=True))
        p_w = jnp.exp2(s_w - m).astype(_MM)
        p_c = jnp.exp2(s_c - m).astype(_MM)
        hsl = slice(h * DH_B, (h + 1) * DH_B)
        acc = (jnp.dot(with_ones(vt_ref[hsl, pl.ds(kstart, span)]), p_w, preferred_element_type=_F32)
               + jnp.dot(with_ones(vct_ref[hsl, :]), p_c, preferred_element_type=_F32))
        ot_sc[hsl, :] = acc[0:DH_B] / acc[DH_B:DH_B + 1]
    o_ref[...] = ot_sc[...].T.astype(o_ref.dtype)


def _nbr_bias_table(rpb, rows):
    nh, n_r, n_c = rpb.shape
    c = np.arange(GRID_W)[:, None]
    kc = np.arange(GRID_W)[None, :]
    cidx = np.clip(kc - c + WIN_C - 1, 0, n_c - 1).reshape(1, -1)
    onehot = jnp.asarray((cidx == np.arange(n_c)[:, None]).astype(np.float32))
    toep = jnp.dot(rpb.reshape(nh * n_r, n_c).astype(_F32), onehot, precision=lax.Precision.HIGHEST)
    toep = toep.reshape(nh, n_r, GRID_W, GRID_W)
    cstart = np.clip(c - WIN_C // 2, 0, GRID_W - WIN_C)
    valid = (kc >= cstart) & (kc < cstart + WIN_C)
    toep = jnp.where(valid[None, None], toep, -jnp.inf) * LOG2E
    assert rows % NBR_ROWS == 0 and rows >= NBR_SPAN_ROWS
    r0 = np.array([0, NBR_ROWS, rows - NBR_ROWS])[:, None, None]
    ss = np.clip(r0 - WIN_R // 2, 0, rows - NBR_SPAN_ROWS)
    key_row = ss + np.arange(NBR_SPAN_ROWS)[None, :, None]
    r = r0 + np.arange(NBR_ROWS)[None, None, :]
    rs = np.clip(r - WIN_R // 2, 0, rows - WIN_R)
    in_rows = (key_row >= rs) & (key_row < rs + WIN_R)
    ridx = np.clip(key_row - r + WIN_R - 1, 0, n_r - 1)
    tb = jnp.take(toep, jnp.asarray(ridx.reshape(-1)), axis=1)
    tb = tb.reshape((nh,) + ridx.shape + (GRID_W, GRID_W))
    tb = jnp.where(in_rows[None, :, :, :, None, None], tb, -jnp.inf)
    tb = tb.transpose(1, 0, 2, 5, 3, 4)
    return tb.reshape(3, nh, NBR_SPAN_ROWS * GRID_W, NBR_ROWS * GRID_W)


def _nbr(qt, k, vt, kc, vct, tb):
    nb, s, _ = k.shape
    rows = s // GRID_W
    n_steps = rows // NBR_ROWS
    past = kc.shape[1]
    tq = NBR_ROWS * GRID_W
    bmap = lambda b, i: (b, 0, 0)
    variant = lambda b, i: (jnp.where(i == 0, 0, jnp.where(i == n_steps - 1, 2, 1)), 0, 0, 0)
    return pl.pallas_call(
        functools.partial(_nbr_kernel, rows=rows), grid=(nb, n_steps),
        in_specs=[pl.BlockSpec((None, HEAD_W, tq), lambda b, i: (b, 0, i)),
                  pl.BlockSpec((None, s, HEAD_W), bmap),
                  pl.BlockSpec((None, HEAD_W, s), bmap),
                  pl.BlockSpec((None, past, HEAD_W), bmap),
                  pl.BlockSpec((None, HEAD_W, past), bmap),
                  pl.BlockSpec((None,) + tb.shape[1:], variant)],
        out_specs=pl.BlockSpec((None, tq, HEAD_W), lambda b, i: (b, i, 0)),
        out_shape=jax.ShapeDtypeStruct((nb, s, HEAD_W), _MM),
        scratch_shapes=[pltpu.VMEM((H_B, LANES, tq), _MM), pltpu.VMEM((HEAD_W, tq), _F32)],
        compiler_params=_cparams(("arbitrary", "arbitrary"), 48),
        name="nbr_attn",
    )(qt, k, vt, kc, vct, tb)


def _mla_cache_kernel(ckv_ref, kr_ref, wkc_ref, wvt_ref, e_ref, kc_ref, vct_ref):
    ckv = ckv_ref[...]
    kc_ref[...] = (_dot(ckv, wkc_ref[...]) + _dot(kr_ref[...], e_ref[...])).astype(kc_ref.dtype)
    vct_ref[...] = _dot_nt(wvt_ref[...], ckv).astype(vct_ref.dtype)


def _mla_cache(ckv, kr, wkc, wvt, expand):
    nb, past, _ = ckv.shape
    bmap = lambda b: (b, 0, 0)
    const = lambda b: (0, 0)
    return pl.pallas_call(
        _mla_cache_kernel, grid=(nb,),
        in_specs=[pl.BlockSpec((None, past, KV_LORA), bmap), pl.BlockSpec((None, past, ROPE_C), bmap),
                  pl.BlockSpec(wkc.shape, const), pl.BlockSpec(wvt.shape, const), pl.BlockSpec(expand.shape, const)],
        out_specs=[pl.BlockSpec((None, past, HEAD_W), bmap), pl.BlockSpec((None, HEAD_W, past), bmap)],
        out_shape=[jax.ShapeDtypeStruct((nb, past, HEAD_W), _MM), jax.ShapeDtypeStruct((nb, HEAD_W, past), _MM)],
        compiler_params=_cparams(("arbitrary",), 32),
        name="mla_cache",
    )(ckv, kr, wkc, wvt, expand)


CONV_HALO = 16


def _conv_kernel(z_ref, w_ref, b_ref, g_ref, beta_ref, o_ref, zp_ref, *, seq, rc):
    zp_ref[0:CONV_HALO, :] = jnp.zeros((CONV_HALO, CONV_CH), _F32)
    zp_ref[CONV_HALO:CONV_HALO + seq, :] = z_ref[...]
    zp_ref[CONV_HALO + seq:2 * CONV_HALO + seq, :] = jnp.zeros((CONV_HALO, CONV_CH), _F32)
    w = w_ref[...]
    shift = CONV_HALO - CONV_K // 2

    def chunk(c, carry):
        base = pl.multiple_of(c * rc, rc)
        win = zp_ref[pl.ds(base, rc + 2 * CONV_HALO), :]
        acc = jnp.zeros((rc, CONV_CH), _F32)
        for j in range(CONV_K):
            acc = acc + win[j + shift:j + shift + rc, :] * w[j:j + 1, :]
        y = acc + b_ref[...]
        mu = jnp.mean(y, axis=-1, keepdims=True)
        yc = y - mu
        var = jnp.mean(yc * yc, axis=-1, keepdims=True)
        yn = yc * lax.rsqrt(var + EPS) * g_ref[...] + beta_ref[...]
        o_ref[pl.ds(base, rc), :] = (yn * _sigmoid(yn)).astype(o_ref.dtype)
        return carry

    lax.fori_loop(0, seq // rc, chunk, 0)


def _conv(z, w, b, g, beta):
    nb, seq, _ = z.shape
    rc = min(128, seq)
    bmap = lambda i: (i, 0, 0)
    const = lambda i: (0, 0)
    kern = functools.partial(_conv_kernel, seq=seq, rc=rc)
    return pl.pallas_call(
        kern, grid=(nb,),
        in_specs=[pl.BlockSpec((None, seq, CONV_CH), bmap), pl.BlockSpec(w.shape, const),
                  pl.BlockSpec(b.shape, const), pl.BlockSpec(g.shape, const), pl.BlockSpec(beta.shape, const)],
        out_specs=pl.BlockSpec((None, seq, CONV_CH), bmap),
        out_shape=jax.ShapeDtypeStruct((nb, seq, CONV_CH), _MM),
        scratch_shapes=[pltpu.VMEM((seq + 2 * CONV_HALO, CONV_CH), _F32)],
        compiler_params=_cparams(("arbitrary",), 40),
        name="conv",
    )(z, w, b, g, beta)


def _post_kernel(x_ref, mod_ref, oa_ref, ob_ref, oc_ref, od_ref, g1_ref, wg_ref, bg_ref,
                 awo_ref, bwo_ref, cwo_ref, dwo_ref, wout_ref, g2_ref, rhi_ref, rlo_ref, rb_ref,
                 x1_ref, h2_ref, dw_ref, *, d):
    x = x_ref[...]
    mod = mod_ref[0]
    shift1, scale1, gate1 = mod[:, 0:d], mod[:, d:2 * d], mod[:, 2 * d:3 * d]
    shift2, scale2 = mod[:, 3 * d:4 * d], mod[:, 4 * d:5 * d]
    h = (_rms(x, g1_ref[...]) * (1.0 + scale1) + shift1).astype(_MM)

    merged = None
    for bi, (o_ref, wo_ref) in enumerate(((oa_ref, awo_ref), (ob_ref, bwo_ref), (oc_ref, cwo_ref), (od_ref, dwo_ref))):
        gate = _sigmoid(_dot(h, wg_ref[:, bi * d:(bi + 1) * d]) + bg_ref[:, bi * d:(bi + 1) * d])
        term = gate * _dot(o_ref[...], wo_ref[...])
        merged = term if merged is None else merged + term
    x1 = x + gate1 * _dot(merged, wout_ref[...])
    x1_ref[...] = x1

    h2 = _rms(x1, g2_ref[...]) * (1.0 + scale2) + shift2
    h2_hi = h2.astype(_MM)
    h2_lo = (h2 - h2_hi.astype(_F32)).astype(_MM)
    h2_ref[...] = h2_hi
    r = (_dot(h2_hi, rhi_ref[...]) + _dot(h2_lo, rhi_ref[...]) + _dot(h2_hi, rlo_ref[...])) + rb_ref[...]
    lgp, lep = r[:, 0:LANES], r[:, LANES:2 * LANES]
    lane = lax.broadcasted_iota(jnp.int32, lgp.shape, 1)
    neg = -jnp.inf
    lg = jnp.where(lane < N_GROUPS, lgp, neg)
    mg = jnp.max(lg, axis=-1, keepdims=True)
    gi = jnp.min(jnp.where(lg == mg, lane, LANES), axis=-1, keepdims=True)
    gw = 1.0 / jnp.sum(jnp.exp(lg - mg), axis=-1, keepdims=True)
    in_group = (lane < N_EXPERTS) & ((lane // EXPERTS_PER_GROUP) == gi)
    le = jnp.where(in_group, lep, neg)
    t1 = jnp.max(le, axis=-1, keepdims=True)
    i1 = jnp.min(jnp.where(le == t1, lane, LANES), axis=-1, keepdims=True)
    le2 = jnp.where(lane == i1, neg, le)
    t2 = jnp.max(le2, axis=-1, keepdims=True)
    i2 = jnp.min(jnp.where(le2 == t2, lane, LANES), axis=-1, keepdims=True)
    e2 = jnp.exp(t2 - t1)
    w1 = gw / (1.0 + e2)
    w2 = gw * e2 / (1.0 + e2)
    dw_ref[...] = jnp.where(lane == i1, w1, 0.0) + jnp.where(lane == i2, w2, 0.0)


def _post(x2d, mods3, mod_row, oa, ob, oc, od, lw):
    t, d = x2d.shape
    tm = min(512, t)
    row = lambda i: (i, 0)
    const = lambda i: (0, 0)
    wnames = ("g1", "wgate", "bgate", "awo", "bwo", "cwo", "dwo", "wout", "g2", "rhi", "rlo", "rb")
    in_specs = [pl.BlockSpec((tm, d), row), pl.BlockSpec((1, 1, 6 * d), lambda i: (mod_row(i * tm), 0, 0))]
    in_specs += [pl.BlockSpec((tm, HEAD_W), row)] * 4
    in_specs += [pl.BlockSpec(lw[n].shape, const) for n in wnames]
    args = [x2d, mods3, oa, ob, oc, od] + [lw[n] for n in wnames]
    return pl.pallas_call(
        functools.partial(_post_kernel, d=d), grid=(t // tm,), in_specs=in_specs,
        out_specs=[pl.BlockSpec((tm, d), row), pl.BlockSpec((tm, d), row), pl.BlockSpec((tm, LANES), row)],
        out_shape=[jax.ShapeDtypeStruct((t, d), _F32), jax.ShapeDtypeStruct((t, d), _MM),
                   jax.ShapeDtypeStruct((t, LANES), _F32)],
        compiler_params=_cparams(("arbitrary",), 56),
        name="post",
    )(*args)


def _moe_kernel(h2_ref, dw_ref, x1_ref, mod_ref, w1_ref, w2_ref, fg_ref, o_ref, acc_ref, *, d, final):
    e = pl.program_id(1)

    @pl.when(e == 0)
    def _():
        acc_ref[...] = jnp.zeros_like(acc_ref)

    hid = _dot(h2_ref[...], w1_ref[0])
    gate, up = hid[:, 0:D_EXPERT], hid[:, D_EXPERT:2 * D_EXPERT]
    dw = dw_ref[...]
    lane = lax.broadcasted_iota(jnp.int32, dw.shape, 1)
    we = jnp.sum(jnp.where(lane == e, dw, 0.0), axis=-1, keepdims=True)
    act = gate * _sigmoid(gate) * up * we
    acc_ref[...] += _dot(act, w2_ref[0])

    @pl.when(e == pl.num_programs(1) - 1)
    def _():
        gate2 = mod_ref[0][:, 5 * d:6 * d]
        x2 = x1_ref[...] + gate2 * acc_ref[...]
        if final:
            x2 = _rms(x2, fg_ref[...])
        o_ref[...] = x2


def _moe(h2, dw, x1, mods3, mod_row, w1, w2, fg, *, final):
    t, d = x1.shape
    tm = min(1024, t)
    n_e = w1.shape[0]
    row = lambda i, e: (i, 0)
    return pl.pallas_call(
        functools.partial(_moe_kernel, d=d, final=final), grid=(t // tm, n_e),
        in_specs=[pl.BlockSpec((tm, d), row), pl.BlockSpec((tm, LANES), row), pl.BlockSpec((tm, d), row),
                  pl.BlockSpec((1, 1, 6 * d), lambda i, e: (mod_row(i * tm), 0, 0)),
                  pl.BlockSpec((1,) + w1.shape[1:], lambda i, e: (e, 0, 0)),
                  pl.BlockSpec((1,) + w2.shape[1:], lambda i, e: (e, 0, 0)),
                  pl.BlockSpec((1, d), lambda i, e: (0, 0))],
        out_specs=pl.BlockSpec((tm, d), row),
        out_shape=jax.ShapeDtypeStruct((t, d), _F32),
        scratch_shapes=[pltpu.VMEM((tm, d), _F32)],
        compiler_params=_cparams(("arbitrary", "arbitrary"), 56),
        name="moe_final" if final else "moe",
    )(h2, dw, x1, mods3, w1, w2, fg)


def _rope_tables(n_tok, dim):
    hq = dim // 4
    tpos = np.arange(n_tok)
    rows = (tpos // GRID_W).astype(np.float32)
    cols = (tpos % GRID_W).astype(np.float32)
    inv = jnp.asarray(ROPE_THETA, _F32) ** (-jnp.arange(hq, dtype=_F32) / hq)
    ar = jnp.asarray(rows)[:, None] * inv[None, :]
    ac = jnp.asarray(cols)[:, None] * inv[None, :]
    ang = jnp.concatenate([ar, ar, ac, ac], axis=-1)
    return jnp.cos(ang), jnp.sin(ang)


def _layer_weights(l, p, grid_rows):
    mm = lambda a: a.astype(_MM)
    w_in = p["w_in"][l]
    pts = np.cumsum([0, 2 * H_A * DH_A, 2 * H_A * DH_A, 2 * H_A * DH_A, H_B * DH_B, H_B * DH_B, H_B * DH_B,
                     Q_LORA, KV_LORA, ROPE_C, 2 * CONV_CH])
    seg = lambda i, j: w_in[:, pts[i]:pts[j]]
    d = w_in.shape[0]
    wckr = seg(8, 9)
    zeros_n = jnp.zeros((d, NOPE_C), w_in.dtype)
    wckr_t = jnp.concatenate([zeros_n, wckr] * H_C, axis=1)
    wukv = p["c_wukv"][l].reshape(KV_LORA, H_C, NOPE_C + DV_C)
    wkc = jnp.concatenate([wukv[:, :, :NOPE_C], jnp.zeros((KV_LORA, H_C, ROPE_C), wukv.dtype)], axis=-1)
    wkc = wkc.reshape(KV_LORA, H_C * (NOPE_C + ROPE_C))
    wv = wukv[:, :, NOPE_C:].reshape(KV_LORA, H_C * DV_C)
    router = jnp.zeros((d, 2 * LANES), _F32)
    router = router.at[:, 0:N_GROUPS].set(p["moe_w_group"][l]).at[:, LANES:LANES + N_EXPERTS].set(p["moe_w_expert"][l])
    rhi = router.astype(_MM)
    rlo = (router - rhi.astype(_F32)).astype(_MM)
    rb = jnp.zeros((1, 2 * LANES), _F32)
    rb = rb.at[0, 0:N_GROUPS].set(p["moe_b_group"][l]).at[0, LANES:LANES + N_EXPERTS].set(p["moe_b_expert"][l])
    conv_w = p["d_conv_w"][l]
    return dict(
        g1=p["norm1_g"][l][None], g2=p["norm2_g"][l][None],
        wa=mm(seg(0, 3)), wb=mm(seg(3, 6)), wcq=mm(seg(6, 7)), wckv=mm(seg(7, 8)), wckr=mm(wckr_t), wdu=mm(seg(9, 10)),
        cqg=p["c_q_norm_g"][l][None], ckvg=p["c_kv_norm_g"][l][None],
        wuq=mm(p["c_wuq"][l]), wkc=mm(wkc), wv=mm(wv), wvt=mm(wv.T),
        alam=p["a_lambda"][l], subg=p["a_subln_g"][l][:, None],
        wgate=mm(p["w_gate"][l]), bgate=p["b_gate"][l][None],
        awo=mm(p["a_wo"][l]), bwo=mm(p["b_wo"][l]), cwo=mm(p["c_wo"][l]), dwo=mm(p["d_wo"][l]),
        wout=mm(p["w_out"][l]), rhi=rhi, rlo=rlo, rb=rb,
        conv_w=conv_w, conv_b=p["d_conv_b"][l][None], ln_g=p["d_ln_g"][l][None], ln_b=p["d_ln_b"][l][None],
        w1=mm(p["moe_w1"][l]), w2=mm(p["moe_w2"][l]),
        tb=_nbr_bias_table(p["b_rpb"][l], grid_rows),
    )


def _layer(x2d, mods3, mod_row, lw, l, n_layers, *, nb, seq, fg, ctx, tables):
    latent = ctx is not None
    t = x2d.shape[0]
    outs = _pre(x2d, mods3, mod_row, lw, latent=latent, seq=seq, tables=tables)
    r3 = lambda a: a.reshape(nb, seq, a.shape[-1])
    lam_init = 0.8 - 0.6 * math.exp(-0.3 * l)
    diff_args = (lw["alam"], lw["subg"])
    if latent:
        aqt, ak, avt, bqt, bk, bvt, qct, kc, vct, z = outs
        own = None
        a_k_c, a_vt_c, b_k_c, b_vt_c, c_k_c, c_vt_c = ctx
        oa = _attn(aqt, r3(ak), avt, (a_k_c, a_vt_c), diff_args, n_heads=H_A, n_maps=2,
                   dqk=DH_A, dv=2 * DH_A, lam_init=lam_init, name="attn_a_lat")
        ob = _nbr(bqt, r3(bk), bvt, b_k_c, b_vt_c, lw["tb"])
        oc = _attn(qct, r3(kc), vct, (c_k_c, c_vt_c), None, n_heads=H_C, n_maps=1,
                   dqk=NOPE_C + ROPE_C, dv=DV_C, name="attn_c_lat")
    else:
        aqt, ak, av, avt, bqt, bk, bv, bvt, qct, kc, vct, ckvn, ckr, z = outs
        own = (ak, av, bk, bv, ckvn, ckr)
        oa = _attn(aqt, r3(ak), avt, None, diff_args, n_heads=H_A, n_maps=2,
                   dqk=DH_A, dv=2 * DH_A, lam_init=lam_init, name="attn_a_ctx")
        ob = _attn(bqt, r3(bk), bvt, None, None, n_heads=H_B, n_maps=1, dqk=DH_B, dv=DH_B, name="attn_b_ctx")
        oc = _attn(qct, r3(kc), vct, None, None, n_heads=H_C, n_maps=1,
                   dqk=NOPE_C + ROPE_C, dv=DV_C, name="attn_c_ctx")
    od = _conv(r3(z), lw["conv_w"], lw["conv_b"], lw["ln_g"], lw["ln_b"])
    f2 = lambda a: a.reshape(t, a.shape[-1])
    x1, h2, dw = _post(x2d, mods3, mod_row, f2(oa), f2(ob), f2(oc), f2(od), lw)
    x2 = _moe(h2, dw, x1, mods3, mod_row, lw["w1"], lw["w2"], fg, final=(l == n_layers - 1))
    return x2, own


def kernel(x_prompt, x_sample, cache_a_k, cache_a_v, cache_b_k, cache_b_v, cache_c_kv, cache_c_krope, c, c_ctx, norm1_g, norm2_g, w_ada, b_ada, w_in, a_lambda, a_subln_g, a_wo, b_rpb, b_wo, c_q_norm_g, c_kv_norm_g, c_wuq, c_wukv, c_wo, d_conv_w, d_conv_b, d_ln_g, d_ln_b, d_wo, w_gate, b_gate, w_out, moe_w_group, moe_b_group, moe_w_expert, moe_b_expert, moe_w1, moe_w2, final_norm_g):
    p = dict(norm1_g=norm1_g, norm2_g=norm2_g, w_in=w_in, a_lambda=a_lambda, a_subln_g=a_subln_g, a_wo=a_wo,
             b_rpb=b_rpb, b_wo=b_wo, c_q_norm_g=c_q_norm_g, c_kv_norm_g=c_kv_norm_g, c_wuq=c_wuq, c_wukv=c_wukv,
             c_wo=c_wo, d_conv_w=d_conv_w, d_conv_b=d_conv_b, d_ln_g=d_ln_g, d_ln_b=d_ln_b, d_wo=d_wo,
             w_gate=w_gate, b_gate=b_gate, w_out=w_out, moe_w_group=moe_w_group, moe_b_group=moe_b_group,
             moe_w_expert=moe_w_expert, moe_b_expert=moe_b_expert, moe_w1=moe_w1, moe_w2=moe_w2)
    n_layers = w_in.shape[0]
    bc, sc, d = x_prompt.shape
    bl, sl, _ = x_sample.shape
    past = cache_a_k.shape[2]
    assert bl + 1 <= 8 and sl % GRID_W == 0

    cond8 = jnp.zeros((8, d), _F32).at[0].set(c_ctx).at[1:1 + bl].set(c)
    mods3 = _adaln(cond8, w_ada, b_ada).reshape(n_layers * 8, 1, 6 * d)
    fg = final_norm_g[None]

    cos_a, sin_a = _rope_tables(sl, DH_A)
    cos_c, sin_c = _rope_tables(sl, ROPE_C)
    reps_a = HEAD_W // DH_A
    ones_n, zeros_n = jnp.ones((sl, NOPE_C), _F32), jnp.zeros((sl, NOPE_C), _F32)
    tables = (jnp.tile(cos_a, (1, reps_a)), jnp.tile(sin_a, (1, reps_a)),
              jnp.concatenate([ones_n, cos_c] * H_C, axis=1), jnp.concatenate([zeros_n, sin_c] * H_C, axis=1))
    expand = np.zeros((ROPE_C, HEAD_W), np.float32)
    for hh in range(H_C):
        expand[np.arange(ROPE_C), hh * (NOPE_C + ROPE_C) + NOPE_C + np.arange(ROPE_C)] = 1.0
    expand = jnp.asarray(expand, _MM)

    lws = [_layer_weights(l, p, sl // GRID_W) for l in range(n_layers)]

    xp = x_prompt.reshape(bc * sc, d)
    states = []
    for l in range(n_layers):
        xp, own = _layer(xp, mods3, (lambda tok, l=l: l * 8), lws[l], l, n_layers, nb=bc, seq=sc, fg=fg, ctx=None,
                         tables=None)
        states.append(own)
    y_prompt = xp.reshape(bc, sc, d)

    xs = x_sample.reshape(bl * sl, d)
    for l in range(n_layers):
        flat = lambda a: a[:, l].reshape(bl, past, -1)
        c_k_c, c_vt_c = _mla_cache(flat(cache_c_kv), flat(cache_c_krope), lws[l]["wkc"], lws[l]["wvt"], expand)
        ctx = (flat(cache_a_k).astype(_MM), jnp.swapaxes(flat(cache_a_v), 1, 2).astype(_MM),
               flat(cache_b_k).astype(_MM), jnp.swapaxes(flat(cache_b_v), 1, 2).astype(_MM), c_k_c, c_vt_c)
        xs, _ = _layer(xs, mods3, (lambda tok, l=l: l * 8 + 1 + tok // sl), lws[l], l, n_layers,
                       nb=bl, seq=sl, fg=fg, ctx=ctx, tables=tables)
    y_sample = xs.reshape(bl, sl, d)

    st = lambda k, shape: jnp.stack([s[k].reshape(shape) for s in states], axis=1)
    new_a_k = st(0, (bc, sc, H_A, 2 * DH_A))
    new_a_v = st(1, (bc, sc, H_A, 2 * DH_A))
    new_b_k = st(2, (bc, sc, H_B, DH_B))
    new_b_v = st(3, (bc, sc, H_B, DH_B))
    new_c_kv = st(4, (bc, sc, KV_LORA))
    new_c_krope = st(5, (bc, sc, ROPE_C))
    return (y_prompt, y_sample, new_a_k, new_a_v, new_b_k, new_b_v, new_c_kv, new_c_krope)
```

```python
import functools
import math

import numpy as np
import jax
import jax.numpy as jnp
from jax import lax
from jax.experimental import pallas as pl
from jax.experimental.pallas import tpu as pltpu

GRID_W = 64
EPS = 1e-6
ROPE_THETA = 10000.0
H_A, DH_A = 4, 32
H_B, DH_B = 4, 64
WIN_R, WIN_C = 8, 16
H_C, Q_LORA, KV_LORA, NOPE_C, ROPE_C, DV_C = 4, 256, 128, 32, 32, 64
CONV_CH, CONV_K = 256, 31
N_BRANCH = 4
N_GROUPS, EXPERTS_PER_GROUP, N_EXPERTS, D_EXPERT = 4, 4, 16, 256
HEAD_W = 256
LANES = 128
SUBLANES = 8
V7X_VMEM_BYTES = 64 * 1024 * 1024
LOG2E = math.log2(math.e)

_MM = jnp.bfloat16
_F32 = jnp.float32


def _cparams(sem, vmem_mb):
    return pltpu.CompilerParams(dimension_semantics=sem,
                                vmem_limit_bytes=min(vmem_mb * 1024 * 1024, V7X_VMEM_BYTES - 8 * 1024 * 1024))


def _dot(a, b):
    return jnp.dot(a.astype(_MM), b.astype(_MM), preferred_element_type=_F32)


def _dot_nt(a, b):
    return lax.dot_general(a.astype(_MM), b.astype(_MM), (((1,), (1,)), ((), ())),
                           preferred_element_type=_F32)


def _sigmoid(x):
    return 1.0 / (1.0 + jnp.exp(-x))


def _rms(x, g):
    return x * lax.rsqrt(jnp.mean(x * x, axis=-1, keepdims=True) + EPS) * g


def _rope(x, cos, sin):
    w = x.shape[-1]
    lane = lax.broadcasted_iota(jnp.int32, x.shape, 1)
    first = (lane % 16) < 8
    fwd = pltpu.roll(x, w - 8, axis=1)
    bwd = pltpu.roll(x, 8, axis=1)
    return x * cos + jnp.where(first, -fwd, bwd) * sin


def _adaln_kernel(cond_ref, w_ref, b_ref, o_ref):
    c = cond_ref[...]
    a = c * _sigmoid(c)
    o_ref[0] = _dot(a, w_ref[0]) + b_ref[0]


def _adaln(cond8, w_ada, b_ada):
    n_l, d, d6 = w_ada.shape
    tn = 1536
    return pl.pallas_call(
        _adaln_kernel,
        grid=(n_l, d6 // tn),
        in_specs=[pl.BlockSpec((8, d), lambda l, j: (0, 0)),
                  pl.BlockSpec((1, d, tn), lambda l, j: (l, 0, j)),
                  pl.BlockSpec((1, 1, tn), lambda l, j: (l, 0, j))],
        out_specs=pl.BlockSpec((1, 8, tn), lambda l, j: (l, 0, j)),
        out_shape=jax.ShapeDtypeStruct((n_l, 8, d6), _F32),
        compiler_params=_cparams(("arbitrary", "arbitrary"), 40),
        name="adaln",
    )(cond8, w_ada, b_ada.reshape(n_l, 1, d6))


def _pre_kernel(*refs, d, latent, sa, sb, sc):
    (x_ref, mod_ref, g1_ref, wa_ref, wb_ref, wcq_ref, wckv_ref, wckr_ref, wdu_ref,
     cqg_ref, ckvg_ref, wuq_ref, wkc_ref, wv_ref) = refs[:14]
    refs = refs[14:]
    if latent:
        cosa_ref, sina_ref, cosc_ref, sinc_ref = refs[:4]
        refs = refs[4:]
        aqt_ref, ak_ref, avt_ref, bqt_ref, bk_ref, bvt_ref, qct_ref, kc_ref, vct_ref, z_ref = refs
    else:
        (aqt_ref, ak_ref, av_ref, avt_ref, bqt_ref, bk_ref, bv_ref, bvt_ref, qct_ref, kc_ref, vct_ref,
         ckvn_ref, ckr_ref, z_ref) = refs

    x = x_ref[...]
    mod = mod_ref[0]
    shift1, scale1 = mod[:, 0:d], mod[:, d:2 * d]
    h = (_rms(x, g1_ref[...]) * (1.0 + scale1) + shift1).astype(_MM)

    pa = _dot(h, wa_ref[...])
    aq, ak, av = pa[:, 0:HEAD_W], pa[:, HEAD_W:2 * HEAD_W], pa[:, 2 * HEAD_W:3 * HEAD_W]
    if latent:
        cosa, sina = cosa_ref[...], sina_ref[...]
        aq = _rope(aq, cosa, sina)
        ak = _rope(ak, cosa, sina)
    else:
        av_ref[...] = av
    aqt_ref[...] = (aq * (sa * LOG2E)).T.astype(aqt_ref.dtype)
    ak_ref[...] = ak.astype(ak_ref.dtype)
    avt_ref[...] = av.T.astype(avt_ref.dtype)

    pb = _dot(h, wb_ref[...])
    bq, bk, bv = pb[:, 0:HEAD_W], pb[:, HEAD_W:2 * HEAD_W], pb[:, 2 * HEAD_W:3 * HEAD_W]
    bk_ref[...] = bk.astype(bk_ref.dtype)
    if not latent:
        bv_ref[...] = bv
    bqt_ref[...] = (bq * (sb * LOG2E)).T.astype(bqt_ref.dtype)
    bvt_ref[...] = bv.T.astype(bvt_ref.dtype)

    cq = _dot(h, wcq_ref[...])
    qc = _dot(_rms(cq, cqg_ref[...]), wuq_ref[...])
    ckv = _dot(h, wckv_ref[...])
    ckvn = _rms(ckv, ckvg_ref[...])
    kr = _dot(h, wckr_ref[...])
    if latent:
        cosc, sinc = cosc_ref[...], sinc_ref[...]
        qc = _rope(qc, cosc, sinc)
        kr = _rope(kr, cosc, sinc)
    else:
        ckvn_ref[...] = ckvn
        ckr_ref[...] = kr[:, NOPE_C:NOPE_C + ROPE_C]
    qct_ref[...] = (qc * (sc * LOG2E)).T.astype(qct_ref.dtype)
    kc_ref[...] = (_dot(ckvn, wkc_ref[...]) + kr).astype(kc_ref.dtype)
    vct_ref[...] = _dot(ckvn, wv_ref[...]).T.astype(vct_ref.dtype)

    du = _dot(h, wdu_ref[...])
    z_ref[...] = du[:, 0:CONV_CH] * _sigmoid(du[:, CONV_CH:2 * CONV_CH])


def _pre(x2d, mods3, mod_row, lw, *, latent, seq, tables):
    t, d = x2d.shape
    tm = min(512, seq)
    n_seq_tiles = seq // tm
    grid = (t // tm,)
    row = lambda i: (i, 0)
    const = lambda i: (0, 0)
    wnames = ("wa", "wb", "wcq", "wckv", "wckr", "wdu", "cqg", "ckvg", "wuq", "wkc", "wv")
    in_specs = [pl.BlockSpec((tm, d), row),
                pl.BlockSpec((1, 1, 6 * d), lambda i: (mod_row(i * tm), 0, 0)),
                pl.BlockSpec((1, d), const)]
    args = [x2d, mods3, lw["g1"]]
    for n in wnames:
        in_specs.append(pl.BlockSpec(lw[n].shape, const))
        args.append(lw[n])
    if latent:
        for tb in tables:
            in_specs.append(pl.BlockSpec((tm, HEAD_W), lambda i: (i % n_seq_tiles, 0)))
            args.append(tb)
    nat = lambda dt: (jax.ShapeDtypeStruct((t, HEAD_W), dt), pl.BlockSpec((tm, HEAD_W), row))
    tr = (jax.ShapeDtypeStruct((t // seq, HEAD_W, seq), _MM),
          pl.BlockSpec((None, HEAD_W, tm), lambda i: (i // n_seq_tiles, 0, i % n_seq_tiles)))
    if latent:
        outs = [tr, nat(_MM), tr, tr, nat(_MM), tr, tr, nat(_MM), tr]
    else:
        outs = [tr, nat(_F32), nat(_F32), tr, tr, nat(_F32), nat(_F32), tr, tr, nat(_MM), tr,
                (jax.ShapeDtypeStruct((t, KV_LORA), _F32), pl.BlockSpec((tm, KV_LORA), row)),
                (jax.ShapeDtypeStruct((t, ROPE_C), _F32), pl.BlockSpec((tm, ROPE_C), row))]
    outs.append((jax.ShapeDtypeStruct((t, CONV_CH), _F32), pl.BlockSpec((tm, CONV_CH), row)))
    out_shape = [o[0] for o in outs]
    out_specs = [o[1] for o in outs]
    kern = functools.partial(_pre_kernel, d=d, latent=latent, sa=DH_A ** -0.5, sb=DH_B ** -0.5,
                             sc=(NOPE_C + ROPE_C) ** -0.5)
    return pl.pallas_call(kern, grid=grid, in_specs=in_specs, out_specs=out_specs, out_shape=out_shape,
                          compiler_params=_cparams(("arbitrary",), 48),
                          name="pre_lat" if latent else "pre_ctx")(*args)


SCORE_LOOKAHEAD = 6
SUM_ROWS = 16


def _attn_kernel(*refs, n_heads, n_maps, dqk, dv, tk, n_own, unroll, tkc, n_cache, diff, lam_init):
    qt_ref, k_ref, vt_ref = refs[:3]
    refs = refs[3:]
    if n_cache:
        kc_ref, vct_ref = refs[:2]
        refs = refs[2:]
    if diff:
        alam_ref, subg_ref = refs[:2]
        refs = refs[2:]
    o_ref, qz_sc, m_sc, acc_sc, ot_sc = refs
    n_chains = n_heads * n_maps

    qz_sc[...] = jnp.zeros(qz_sc.shape, qz_sc.dtype)
    for c in range(n_chains):
        qz_sc[c, (c * dqk) % LANES:(c * dqk) % LANES + dqk, :] = qt_ref[c * dqk:(c + 1) * dqk, :]
    m_sc[...] = jnp.full(m_sc.shape, -jnp.inf, _F32)
    acc_sc[...] = jnp.zeros(acc_sc.shape, _F32)

    def update(tiles):
        steps = [(ti, c) for ti in range(len(tiles)) for c in range(n_chains)]
        kts, vts = {}, {}

        def scores(ti, c):
            if ti not in kts:
                kts[ti] = tiles[ti][0]().astype(_MM)
            grp = (c * dqk) // LANES
            return jnp.dot(kts[ti][:, grp * LANES:(grp + 1) * LANES], qz_sc[c], preferred_element_type=_F32)

        def values(ti, h):
            if (ti, h) not in vts:
                vt = tiles[ti][1](h).astype(_MM)
                vts[ti, h] = jnp.concatenate([vt, jnp.ones((SUM_ROWS, vt.shape[1]), _MM)], axis=0)
            return vts[ti, h]

        pending = [scores(*st) for st in steps[:SCORE_LOOKAHEAD]]
        for idx, (ti, c) in enumerate(steps):
            s = pending.pop(0)
            if idx + SCORE_LOOKAHEAD < len(steps):
                pending.append(scores(*steps[idx + SCORE_LOOKAHEAD]))
            m_prev = m_sc[c]
            m_new = jnp.maximum(m_prev, jnp.max(s, axis=0, keepdims=True))
            alpha = jnp.exp2(m_prev - m_new)
            p = jnp.exp2(s - m_new).astype(_MM)
            acc_sc[c] = alpha * acc_sc[c] + jnp.dot(values(ti, c // n_maps), p, preferred_element_type=_F32)
            m_sc[c] = m_new

    if n_cache:
        update([(lambda j=j: kc_ref[j * tkc:(j + 1) * tkc, :],
                 lambda h, j=j: vct_ref[h * dv:(h + 1) * dv, j * tkc:(j + 1) * tkc]) for j in range(n_cache)])

    def body(j, carry):
        starts = [pl.multiple_of((j * unroll + u) * tk, tk) for u in range(unroll)]
        update([(lambda st=st: k_ref[pl.ds(st, tk), :],
                 lambda h, st=st: vt_ref[h * dv:(h + 1) * dv, pl.ds(st, tk)]) for st in starts])
        return carry

    lax.fori_loop(0, n_own // unroll, body, 0)

    def normalised(c):
        a = acc_sc[c]
        return a[0:dv] / a[dv:dv + 1]

    if diff:
        lv = alam_ref[...]
        lam = (jnp.exp(jnp.sum(lv[0:1] * lv[1:2], axis=-1, keepdims=True))
               - jnp.exp(jnp.sum(lv[2:3] * lv[3:4], axis=-1, keepdims=True)) + lam_init)
    for h in range(n_heads):
        if diff:
            o = normalised(2 * h) - lam * normalised(2 * h + 1)
            o = o * lax.rsqrt(jnp.mean(o * o, axis=0, keepdims=True) + EPS) * subg_ref[...] * (1.0 - lam_init)
        else:
            o = normalised(h)
        ot_sc[h * dv:(h + 1) * dv, :] = o
    o_ref[...] = ot_sc[...].T.astype(o_ref.dtype)


def _attn(qt, k, vt, cache, diff_args, *, n_heads, n_maps, dqk, dv, lam_init=0.0, name):
    nb, s, _ = k.shape
    tq = min(256, s)
    tk = min(256, s)
    n_own = s // tk
    qmap = lambda b, i: (b, 0, i)
    bmap = lambda b, i: (b, 0, 0)
    in_specs = [pl.BlockSpec((None, HEAD_W, tq), qmap),
                pl.BlockSpec((None, s, HEAD_W), bmap),
                pl.BlockSpec((None, HEAD_W, s), bmap)]
    args = [qt, k, vt]
    tkc, n_cache = tk, 0
    if cache is not None:
        past = cache[0].shape[1]
        tkc = min(256, past)
        n_cache = past // tkc
        in_specs += [pl.BlockSpec((None, past, HEAD_W), bmap), pl.BlockSpec((None, HEAD_W, past), bmap)]
        args += list(cache)
    if diff_args is not None:
        in_specs += [pl.BlockSpec(diff_args[0].shape, lambda b, i: (0, 0)),
                     pl.BlockSpec(diff_args[1].shape, lambda b, i: (0, 0))]
        args += list(diff_args)
    n_chains = n_heads * n_maps
    unroll = 8 if n_own % 8 == 0 else (4 if n_own % 4 == 0 else 1)
    kern = functools.partial(_attn_kernel, n_heads=n_heads, n_maps=n_maps, dqk=dqk, dv=dv, tk=tk, n_own=n_own,
                             unroll=unroll, tkc=tkc, n_cache=n_cache, diff=diff_args is not None, lam_init=lam_init)
    return pl.pallas_call(
        kern, grid=(nb, s // tq), in_specs=in_specs,
        out_specs=pl.BlockSpec((None, tq, HEAD_W), lambda b, i: (b, i, 0)),
        out_shape=jax.ShapeDtypeStruct((nb, s, HEAD_W), _MM),
        scratch_shapes=[pltpu.VMEM((n_chains, LANES, tq), _MM), pltpu.VMEM((n_chains, 1, tq), _F32),
                        pltpu.VMEM((n_chains, dv + SUM_ROWS, tq), _F32), pltpu.VMEM((n_heads * dv, tq), _F32)],
        compiler_params=_cparams(("arbitrary", "arbitrary"), 48),
        name=name,
    )(*args)


NBR_ROWS = 4
NBR_SPAN_ROWS = 12


def _nbr_kernel(qt_ref, k_ref, vt_ref, kc_ref, vct_ref, tb_ref, o_ref, qz_sc, ot_sc, *, rows):
    i = pl.program_id(1)
    ss = jnp.clip(i * NBR_ROWS - WIN_R // 2, 0, rows - NBR_SPAN_ROWS)
    kstart = pl.multiple_of(ss * GRID_W, NBR_ROWS * GRID_W)
    span = NBR_SPAN_ROWS * GRID_W

    qz_sc[...] = jnp.zeros(qz_sc.shape, qz_sc.dtype)
    for h in range(H_B):
        qz_sc[h, (h * DH_B) % LANES:(h * DH_B) % LANES + DH_B, :] = qt_ref[h * DH_B:(h + 1) * DH_B, :]
    ks = k_ref[pl.ds(kstart, span), :].astype(_MM)
    kc = kc_ref[...].astype(_MM)

    def scores(h):
        gsl = slice((h * DH_B) // LANES * LANES, ((h * DH_B) // LANES + 1) * LANES)
        return (jnp.dot(ks[:, gsl], qz_sc[h], preferred_element_type=_F32),
                jnp.dot(kc[:, gsl], qz_sc[h], preferred_element_type=_F32))

    def with_ones(vt):
        return jnp.concatenate([vt.astype(_MM), jnp.ones((SUM_ROWS, vt.shape[1]), _MM)], axis=0)

    pending = scores(0)
    for h in range(H_B):
        s_w, s_c = pending
        if h + 1 < H_B:
            pending = scores(h + 1)
        s_w = s_w + tb_ref[h]
        m = jnp.maximum(jnp.max(s_w, axis=0, keepdims=True), jnp.max(s_c, axis=0, keepdims=True))
        p_w = jnp.exp2(s_w - m).astype(_MM)
        p_c = jnp.exp2(s_c - m).astype(_MM)
        hsl = slice(h * DH_B, (h + 1) * DH_B)
        acc = (jnp.dot(with_ones(vt_ref[hsl, pl.ds(kstart, span)]), p_w, preferred_element_type=_F32)
               + jnp.dot(with_ones(vct_ref[hsl, :]), p_c, preferred_element_type=_F32))
        ot_sc[hsl, :] = acc[0:DH_B] / acc[DH_B:DH_B + 1]
    o_ref[...] = ot_sc[...].T.astype(o_ref.dtype)


def _nbr_bias_table(rpb, rows):
    nh, n_r, n_c = rpb.shape
    c = np.arange(GRID_W)[:, None]
    kc = np.arange(GRID_W)[None, :]
    cidx = np.clip(kc - c + WIN_C - 1, 0, n_c - 1).reshape(1, -1)
    onehot = jnp.asarray((cidx == np.arange(n_c)[:, None]).astype(np.float32))
    toep = jnp.dot(rpb.reshape(nh * n_r, n_c).astype(_F32), onehot, precision=lax.Precision.HIGHEST)
    toep = toep.reshape(nh, n_r, GRID_W, GRID_W)
    cstart = np.clip(c - WIN_C // 2, 0, GRID_W - WIN_C)
    valid = (kc >= cstart) & (kc < cstart + WIN_C)
    toep = jnp.where(valid[None, None], toep, -jnp.inf) * LOG2E
    assert rows % NBR_ROWS == 0 and rows >= NBR_SPAN_ROWS
    r0 = np.array([0, NBR_ROWS, rows - NBR_ROWS])[:, None, None]
    ss = np.clip(r0 - WIN_R // 2, 0, rows - NBR_SPAN_ROWS)
    key_row = ss + np.arange(NBR_SPAN_ROWS)[None, :, None]
    r = r0 + np.arange(NBR_ROWS)[None, None, :]
    rs = np.clip(r - WIN_R // 2, 0, rows - WIN_R)
    in_rows = (key_row >= rs) & (key_row < rs + WIN_R)
    ridx = np.clip(key_row - r + WIN_R - 1, 0, n_r - 1)
    tb = jnp.take(toep, jnp.asarray(ridx.reshape(-1)), axis=1)
    tb = tb.reshape((nh,) + ridx.shape + (GRID_W, GRID_W))
    tb = jnp.where(in_rows[None, :, :, :, None, None], tb, -jnp.inf)
    tb = tb.transpose(1, 0, 2, 5, 3, 4)
    return tb.reshape(3, nh, NBR_SPAN_ROWS * GRID_W, NBR_ROWS * GRID_W)


def _nbr(qt, k, vt, kc, vct, tb):
    nb, s, _ = k.shape
    rows = s // GRID_W
    n_steps = rows // NBR_ROWS
    past = kc.shape[1]
    tq = NBR_ROWS * GRID_W
    bmap = lambda b, i: (b, 0, 0)
    variant = lambda b, i: (jnp.where(i == 0, 0, jnp.where(i == n_steps - 1, 2, 1)), 0, 0, 0)
    return pl.pallas_call(
        functools.partial(_nbr_kernel, rows=rows), grid=(nb, n_steps),
        in_specs=[pl.BlockSpec((None, HEAD_W, tq), lambda b, i: (b, 0, i)),
                  pl.BlockSpec((None, s, HEAD_W), bmap),
                  pl.BlockSpec((None, HEAD_W, s), bmap),
                  pl.BlockSpec((None, past, HEAD_W), bmap),
                  pl.BlockSpec((None, HEAD_W, past), bmap),
                  pl.BlockSpec((None,) + tb.shape[1:], variant)],
        out_specs=pl.BlockSpec((None, tq, HEAD_W), lambda b, i: (b, i, 0)),
        out_shape=jax.ShapeDtypeStruct((nb, s, HEAD_W), _MM),
        scratch_shapes=[pltpu.VMEM((H_B, LANES, tq), _MM), pltpu.VMEM((HEAD_W, tq), _F32)],
        compiler_params=_cparams(("arbitrary", "arbitrary"), 48),
        name="nbr_attn",
    )(qt, k, vt, kc, vct, tb)


def _mla_cache_kernel(ckv_ref, kr_ref, wkc_ref, wvt_ref, e_ref, kc_ref, vct_ref):
    ckv = ckv_ref[...]
    kc_ref[...] = (_dot(ckv, wkc_ref[...]) + _dot(kr_ref[...], e_ref[...])).astype(kc_ref.dtype)
    vct_ref[...] = _dot_nt(wvt_ref[...], ckv).astype(vct_ref.dtype)


def _mla_cache(ckv, kr, wkc, wvt, expand):
    nb, past, _ = ckv.shape
    bmap = lambda b: (b, 0, 0)
    const = lambda b: (0, 0)
    return pl.pallas_call(
        _mla_cache_kernel, grid=(nb,),
        in_specs=[pl.BlockSpec((None, past, KV_LORA), bmap), pl.BlockSpec((None, past, ROPE_C), bmap),
                  pl.BlockSpec(wkc.shape, const), pl.BlockSpec(wvt.shape, const), pl.BlockSpec(expand.shape, const)],
        out_specs=[pl.BlockSpec((None, past, HEAD_W), bmap), pl.BlockSpec((None, HEAD_W, past), bmap)],
        out_shape=[jax.ShapeDtypeStruct((nb, past, HEAD_W), _MM), jax.ShapeDtypeStruct((nb, HEAD_W, past), _MM)],
        compiler_params=_cparams(("arbitrary",), 32),
        name="mla_cache",
    )(ckv, kr, wkc, wvt, expand)


CONV_HALO = 16


def _conv_kernel(z_ref, w_ref, b_ref, g_ref, beta_ref, o_ref, zp_ref, *, seq, rc):
    zp_ref[0:CONV_HALO, :] = jnp.zeros((CONV_HALO, CONV_CH), _F32)
    zp_ref[CONV_HALO:CONV_HALO + seq, :] = z_ref[...]
    zp_ref[CONV_HALO + seq:2 * CONV_HALO + seq, :] = jnp.zeros((CONV_HALO, CONV_CH), _F32)
    w = w_ref[...]
    shift = CONV_HALO - CONV_K // 2

    def chunk(c, carry):
        base = pl.multiple_of(c * rc, rc)
        n_win = rc + 2 * CONV_HALO
        win = zp_ref[pl.ds(base, n_win), :]
        acc = jnp.zeros((rc, CONV_CH), _F32)
        for phase in range(SUBLANES):
            rolled = win if phase == 0 else pltpu.roll(win, n_win - phase, axis=0)
            for j in range(CONV_K):
                if (j + shift) % SUBLANES == phase:
                    a = (j + shift) - phase
                    acc = acc + rolled[a:a + rc, :] * w[j:j + 1, :]
        y = acc + b_ref[...]
        mu = jnp.mean(y, axis=-1, keepdims=True)
        yc = y - mu
        var = jnp.mean(yc * yc, axis=-1, keepdims=True)
        yn = yc * lax.rsqrt(var + EPS) * g_ref[...] + beta_ref[...]
        o_ref[pl.ds(base, rc), :] = (yn * _sigmoid(yn)).astype(o_ref.dtype)
        return carry

    lax.fori_loop(0, seq // rc, chunk, 0)


def _conv(z, w, b, g, beta):
    nb, seq, _ = z.shape
    rc = min(128, seq)
    bmap = lambda i: (i, 0, 0)
    const = lambda i: (0, 0)
    kern = functools.partial(_conv_kernel, seq=seq, rc=rc)
    return pl.pallas_call(
        kern, grid=(nb,),
        in_specs=[pl.BlockSpec((None, seq, CONV_CH), bmap), pl.BlockSpec(w.shape, const),
                  pl.BlockSpec(b.shape, const), pl.BlockSpec(g.shape, const), pl.BlockSpec(beta.shape, const)],
        out_specs=pl.BlockSpec((None, seq, CONV_CH), bmap),
        out_shape=jax.ShapeDtypeStruct((nb, seq, CONV_CH), _MM),
        scratch_shapes=[pltpu.VMEM((seq + 2 * CONV_HALO, CONV_CH), _F32)],
        compiler_params=_cparams(("arbitrary",), 40),
        name="conv",
    )(z, w, b, g, beta)


def _post_kernel(x_ref, mod_ref, oa_ref, ob_ref, oc_ref, od_ref, g1_ref, wg_ref, bg_ref,
                 awo_ref, bwo_ref, cwo_ref, dwo_ref, wout_ref, g2_ref, rhi_ref, rlo_ref, rb_ref,
                 x1_ref, h2_ref, dw_ref, *, d):
    x = x_ref[...]
    mod = mod_ref[0]
    shift1, scale1, gate1 = mod[:, 0:d], mod[:, d:2 * d], mod[:, 2 * d:3 * d]
    shift2, scale2 = mod[:, 3 * d:4 * d], mod[:, 4 * d:5 * d]
    h = (_rms(x, g1_ref[...]) * (1.0 + scale1) + shift1).astype(_MM)

    merged = None
    for bi, (o_ref, wo_ref) in enumerate(((oa_ref, awo_ref), (ob_ref, bwo_ref), (oc_ref, cwo_ref), (od_ref, dwo_ref))):
        gate = _sigmoid(_dot(h, wg_ref[:, bi * d:(bi + 1) * d]) + bg_ref[:, bi * d:(bi + 1) * d])
        term = gate * _dot(o_ref[...], wo_ref[...])
        merged = term if merged is None else merged + term
    x1 = x + gate1 * _dot(merged, wout_ref[...])
    x1_ref[...] = x1

    h2 = _rms(x1, g2_ref[...]) * (1.0 + scale2) + shift2
    h2_hi = h2.astype(_MM)
    h2_lo = (h2 - h2_hi.astype(_F32)).astype(_MM)
    h2_ref[...] = h2_hi
    r = (_dot(h2_hi, rhi_ref[...]) + _dot(h2_lo, rhi_ref[...]) + _dot(h2_hi, rlo_ref[...])) + rb_ref[...]
    lgp, lep = r[:, 0:LANES], r[:, LANES:2 * LANES]
    lane = lax.broadcasted_iota(jnp.int32, lgp.shape, 1)
    neg = -jnp.inf
    lg = jnp.where(lane < N_GROUPS, lgp, neg)
    mg = jnp.max(lg, axis=-1, keepdims=True)
    gi = jnp.min(jnp.where(lg == mg, lane, LANES), axis=-1, keepdims=True)
    gw = 1.0 / jnp.sum(jnp.exp(lg - mg), axis=-1, keepdims=True)
    in_group = (lane < N_EXPERTS) & ((lane // EXPERTS_PER_GROUP) == gi)
    le = jnp.where(in_group, lep, neg)
    t1 = jnp.max(le, axis=-1, keepdims=True)
    i1 = jnp.min(jnp.where(le == t1, lane, LANES), axis=-1, keepdims=True)
    le2 = jnp.where(lane == i1, neg, le)
    t2 = jnp.max(le2, axis=-1, keepdims=True)
    i2 = jnp.min(jnp.where(le2 == t2, lane, LANES), axis=-1, keepdims=True)
    e2 = jnp.exp(t2 - t1)
    w1 = gw / (1.0 + e2)
    w2 = gw * e2 / (1.0 + e2)
    dw_ref[...] = jnp.where(lane == i1, w1, 0.0) + jnp.where(lane == i2, w2, 0.0)


def _post(x2d, mods3, mod_row, oa, ob, oc, od, lw):
    t, d = x2d.shape
    tm = min(512, t)
    row = lambda i: (i, 0)
    const = lambda i: (0, 0)
    wnames = ("g1", "wgate", "bgate", "awo", "bwo", "cwo", "dwo", "wout", "g2", "rhi", "rlo", "rb")
    in_specs = [pl.BlockSpec((tm, d), row), pl.BlockSpec((1, 1, 6 * d), lambda i: (mod_row(i * tm), 0, 0))]
    in_specs += [pl.BlockSpec((tm, HEAD_W), row)] * 4
    in_specs += [pl.BlockSpec(lw[n].shape, const) for n in wnames]
    args = [x2d, mods3, oa, ob, oc, od] + [lw[n] for n in wnames]
    return pl.pallas_call(
        functools.partial(_post_kernel, d=d), grid=(t // tm,), in_specs=in_specs,
        out_specs=[pl.BlockSpec((tm, d), row), pl.BlockSpec((tm, d), row), pl.BlockSpec((tm, LANES), row)],
        out_shape=[jax.ShapeDtypeStruct((t, d), _F32), jax.ShapeDtypeStruct((t, d), _MM),
                   jax.ShapeDtypeStruct((t, LANES), _F32)],
        compiler_params=_cparams(("arbitrary",), 56),
        name="post",
    )(*args)


def _moe_kernel(h2_ref, dw_ref, x1_ref, mod_ref, w1_ref, w2_ref, fg_ref, o_ref, acc_ref, *, d, final):
    g = pl.program_id(1)

    @pl.when(g == 0)
    def _():
        acc_ref[...] = jnp.zeros_like(acc_ref)

    h2 = h2_ref[...]
    dw = dw_ref[...]
    lane = lax.broadcasted_iota(jnp.int32, dw.shape, 1)
    acts = []
    for j in range(EXPERTS_PER_GROUP):
        hid = _dot(h2, w1_ref[j])
        gate, up = hid[:, 0:D_EXPERT], hid[:, D_EXPERT:2 * D_EXPERT]
        we = jnp.sum(jnp.where(lane == g * EXPERTS_PER_GROUP + j, dw, 0.0), axis=-1, keepdims=True)
        acts.append((gate * _sigmoid(gate) * up * we).astype(_MM))
    acc_ref[...] += _dot(jnp.concatenate(acts, axis=-1), w2_ref[...])

    @pl.when(g == pl.num_programs(1) - 1)
    def _():
        gate2 = mod_ref[0][:, 5 * d:6 * d]
        x2 = x1_ref[...] + gate2 * acc_ref[...]
        if final:
            x2 = _rms(x2, fg_ref[...])
        o_ref[...] = x2


def _moe(h2, dw, x1, mods3, mod_row, w1, w2, fg, *, final):
    t, d = x1.shape
    tm = min(1024, t)
    n_e, _, d_hid = w1.shape
    w2g = w2.reshape(N_GROUPS, EXPERTS_PER_GROUP * D_EXPERT, d)
    row = lambda i, g: (i, 0)
    return pl.pallas_call(
        functools.partial(_moe_kernel, d=d, final=final), grid=(t // tm, N_GROUPS),
        in_specs=[pl.BlockSpec((tm, d), row), pl.BlockSpec((tm, LANES), row), pl.BlockSpec((tm, d), row),
                  pl.BlockSpec((1, 1, 6 * d), lambda i, g: (mod_row(i * tm), 0, 0)),
                  pl.BlockSpec((EXPERTS_PER_GROUP, d, d_hid), lambda i, g: (g, 0, 0)),
                  pl.BlockSpec((None,) + w2g.shape[1:], lambda i, g: (g, 0, 0)),
                  pl.BlockSpec((1, d), lambda i, g: (0, 0))],
        out_specs=pl.BlockSpec((tm, d), row),
        out_shape=jax.ShapeDtypeStruct((t, d), _F32),
        scratch_shapes=[pltpu.VMEM((tm, d), _F32)],
        compiler_params=_cparams(("arbitrary", "arbitrary"), 56),
        name="moe_final" if final else "moe",
    )(h2, dw, x1, mods3, w1, w2g, fg)


def _rope_tables(n_tok, dim):
    hq = dim // 4
    tpos = np.arange(n_tok)
    rows = (tpos // GRID_W).astype(np.float32)
    cols = (tpos % GRID_W).astype(np.float32)
    inv = jnp.asarray(ROPE_THETA, _F32) ** (-jnp.arange(hq, dtype=_F32) / hq)
    ar = jnp.asarray(rows)[:, None] * inv[None, :]
    ac = jnp.asarray(cols)[:, None] * inv[None, :]
    ang = jnp.concatenate([ar, ar, ac, ac], axis=-1)
    return jnp.cos(ang), jnp.sin(ang)


def _layer_weights(l, p, grid_rows):
    mm = lambda a: a.astype(_MM)
    w_in = p["w_in"][l]
    pts = np.cumsum([0, 2 * H_A * DH_A, 2 * H_A * DH_A, 2 * H_A * DH_A, H_B * DH_B, H_B * DH_B, H_B * DH_B,
                     Q_LORA, KV_LORA, ROPE_C, 2 * CONV_CH])
    seg = lambda i, j: w_in[:, pts[i]:pts[j]]
    d = w_in.shape[0]
    wckr = seg(8, 9)
    zeros_n = jnp.zeros((d, NOPE_C), w_in.dtype)
    wckr_t = jnp.concatenate([zeros_n, wckr] * H_C, axis=1)
    wukv = p["c_wukv"][l].reshape(KV_LORA, H_C, NOPE_C + DV_C)
    wkc = jnp.concatenate([wukv[:, :, :NOPE_C], jnp.zeros((KV_LORA, H_C, ROPE_C), wukv.dtype)], axis=-1)
    wkc = wkc.reshape(KV_LORA, H_C * (NOPE_C + ROPE_C))
    wv = wukv[:, :, NOPE_C:].reshape(KV_LORA, H_C * DV_C)
    router = jnp.zeros((d, 2 * LANES), _F32)
    router = router.at[:, 0:N_GROUPS].set(p["moe_w_group"][l]).at[:, LANES:LANES + N_EXPERTS].set(p["moe_w_expert"][l])
    rhi = router.astype(_MM)
    rlo = (router - rhi.astype(_F32)).astype(_MM)
    rb = jnp.zeros((1, 2 * LANES), _F32)
    rb = rb.at[0, 0:N_GROUPS].set(p["moe_b_group"][l]).at[0, LANES:LANES + N_EXPERTS].set(p["moe_b_expert"][l])
    conv_w = p["d_conv_w"][l]
    return dict(
        g1=p["norm1_g"][l][None], g2=p["norm2_g"][l][None],
        wa=mm(seg(0, 3)), wb=mm(seg(3, 6)), wcq=mm(seg(6, 7)), wckv=mm(seg(7, 8)), wckr=mm(wckr_t), wdu=mm(seg(9, 10)),
        cqg=p["c_q_norm_g"][l][None], ckvg=p["c_kv_norm_g"][l][None],
        wuq=mm(p["c_wuq"][l]), wkc=mm(wkc), wv=mm(wv), wvt=mm(wv.T),
        alam=p["a_lambda"][l], subg=p["a_subln_g"][l][:, None],
        wgate=mm(p["w_gate"][l]), bgate=p["b_gate"][l][None],
        awo=mm(p["a_wo"][l]), bwo=mm(p["b_wo"][l]), cwo=mm(p["c_wo"][l]), dwo=mm(p["d_wo"][l]),
        wout=mm(p["w_out"][l]), rhi=rhi, rlo=rlo, rb=rb,
        conv_w=conv_w, conv_b=p["d_conv_b"][l][None], ln_g=p["d_ln_g"][l][None], ln_b=p["d_ln_b"][l][None],
        w1=mm(p["moe_w1"][l]), w2=mm(p["moe_w2"][l]),
        tb=_nbr_bias_table(p["b_rpb"][l], grid_rows),
    )


def _layer(x2d, mods3, mod_row, lw, l, n_layers, *, nb, seq, fg, ctx, tables):
    latent = ctx is not None
    t = x2d.shape[0]
    outs = _pre(x2d, mods3, mod_row, lw, latent=latent, seq=seq, tables=tables)
    r3 = lambda a: a.reshape(nb, seq, a.shape[-1])
    lam_init = 0.8 - 0.6 * math.exp(-0.3 * l)
    diff_args = (lw["alam"], lw["subg"])
    if latent:
        aqt, ak, avt, bqt, bk, bvt, qct, kc, vct, z = outs
        own = None
        a_k_c, a_vt_c, b_k_c, b_vt_c, c_k_c, c_vt_c = ctx
        oa = _attn(aqt, r3(ak), avt, (a_k_c, a_vt_c), diff_args, n_heads=H_A, n_maps=2,
                   dqk=DH_A, dv=2 * DH_A, lam_init=lam_init, name="attn_a_lat")
        ob = _nbr(bqt, r3(bk), bvt, b_k_c, b_vt_c, lw["tb"])
        oc = _attn(qct, r3(kc), vct, (c_k_c, c_vt_c), None, n_heads=H_C, n_maps=1,
                   dqk=NOPE_C + ROPE_C, dv=DV_C, name="attn_c_lat")
    else:
        aqt, ak, av, avt, bqt, bk, bv, bvt, qct, kc, vct, ckvn, ckr, z = outs
        own = (ak, av, bk, bv, ckvn, ckr)
        oa = _attn(aqt, r3(ak), avt, None, diff_args, n_heads=H_A, n_maps=2,
                   dqk=DH_A, dv=2 * DH_A, lam_init=lam_init, name="attn_a_ctx")
        ob = _attn(bqt, r3(bk), bvt, None, None, n_heads=H_B, n_maps=1, dqk=DH_B, dv=DH_B, name="attn_b_ctx")
        oc = _attn(qct, r3(kc), vct, None, None, n_heads=H_C, n_maps=1,
                   dqk=NOPE_C + ROPE_C, dv=DV_C, name="attn_c_ctx")
    od = _conv(r3(z), lw["conv_w"], lw["conv_b"], lw["ln_g"], lw["ln_b"])
    f2 = lambda a: a.reshape(t, a.shape[-1])
    x1, h2, dw = _post(x2d, mods3, mod_row, f2(oa), f2(ob), f2(oc), f2(od), lw)
    x2 = _moe(h2, dw, x1, mods3, mod_row, lw["w1"], lw["w2"], fg, final=(l == n_layers - 1))
    return x2, own


def kernel(x_prompt, x_sample, cache_a_k, cache_a_v, cache_b_k, cache_b_v, cache_c_kv, cache_c_krope, c, c_ctx, norm1_g, norm2_g, w_ada, b_ada, w_in, a_lambda, a_subln_g, a_wo, b_rpb, b_wo, c_q_norm_g, c_kv_norm_g, c_wuq, c_wukv, c_wo, d_conv_w, d_conv_b, d_ln_g, d_ln_b, d_wo, w_gate, b_gate, w_out, moe_w_group, moe_b_group, moe_w_expert, moe_b_expert, moe_w1, moe_w2, final_norm_g):
    p = dict(norm1_g=norm1_g, norm2_g=norm2_g, w_in=w_in, a_lambda=a_lambda, a_subln_g=a_subln_g, a_wo=a_wo,
             b_rpb=b_rpb, b_wo=b_wo, c_q_norm_g=c_q_norm_g, c_kv_norm_g=c_kv_norm_g, c_wuq=c_wuq, c_wukv=c_wukv,
             c_wo=c_wo, d_conv_w=d_conv_w, d_conv_b=d_conv_b, d_ln_g=d_ln_g, d_ln_b=d_ln_b, d_wo=d_wo,
             w_gate=w_gate, b_gate=b_gate, w_out=w_out, moe_w_group=moe_w_group, moe_b_group=moe_b_group,
             moe_w_expert=moe_w_expert, moe_b_expert=moe_b_expert, moe_w1=moe_w1, moe_w2=moe_w2)
    n_layers = w_in.shape[0]
    bc, sc, d = x_prompt.shape
    bl, sl, _ = x_sample.shape
    past = cache_a_k.shape[2]
    assert bl + 1 <= 8 and sl % GRID_W == 0

    cond8 = jnp.zeros((8, d), _F32).at[0].set(c_ctx).at[1:1 + bl].set(c)
    mods3 = _adaln(cond8, w_ada, b_ada).reshape(n_layers * 8, 1, 6 * d)
    fg = final_norm_g[None]

    cos_a, sin_a = _rope_tables(sl, DH_A)
    cos_c, sin_c = _rope_tables(sl, ROPE_C)
    reps_a = HEAD_W // DH_A
    ones_n, zeros_n = jnp.ones((sl, NOPE_C), _F32), jnp.zeros((sl, NOPE_C), _F32)
    tables = (jnp.tile(cos_a, (1, reps_a)), jnp.tile(sin_a, (1, reps_a)),
              jnp.concatenate([ones_n, cos_c] * H_C, axis=1), jnp.concatenate([zeros_n, sin_c] * H_C, axis=1))
    expand = np.zeros((ROPE_C, HEAD_W), np.float32)
    for hh in range(H_C):
        expand[np.arange(ROPE_C), hh * (NOPE_C + ROPE_C) + NOPE_C + np.arange(ROPE_C)] = 1.0
    expand = jnp.asarray(expand, _MM)

    lws = [_layer_weights(l, p, sl // GRID_W) for l in range(n_layers)]

    xp = x_prompt.reshape(bc * sc, d)
    states = []
    for l in range(n_layers):
        xp, own = _layer(xp, mods3, (lambda tok, l=l: l * 8), lws[l], l, n_layers, nb=bc, seq=sc, fg=fg, ctx=None,
                         tables=None)
        states.append(own)
    y_prompt = xp.reshape(bc, sc, d)

    xs = x_sample.reshape(bl * sl, d)
    for l in range(n_layers):
        flat = lambda a: a[:, l].reshape(bl, past, -1)
        c_k_c, c_vt_c = _mla_cache(flat(cache_c_kv), flat(cache_c_krope), lws[l]["wkc"], lws[l]["wvt"], expand)
        ctx = (flat(cache_a_k).astype(_MM), jnp.swapaxes(flat(cache_a_v), 1, 2).astype(_MM),
               flat(cache_b_k).astype(_MM), jnp.swapaxes(flat(cache_b_v), 1, 2).astype(_MM), c_k_c, c_vt_c)
        xs, _ = _layer(xs, mods3, (lambda tok, l=l: l * 8 + 1 + tok // sl), lws[l], l, n_layers,
                       nb=bl, seq=sl, fg=fg, ctx=ctx, tables=tables)
    y_sample = xs.reshape(bl, sl, d)

    st = lambda k, shape: jnp.stack([s[k].reshape(shape) for s in states], axis=1)
    new_a_k = st(0, (bc, sc, H_A, 2 * DH_A))
    new_a_v = st(1, (bc, sc, H_A, 2 * DH_A))
    new_b_k = st(2, (bc, sc, H_B, DH_B))
    new_b_v = st(3, (bc, sc, H_B, DH_B))
    new_c_kv = st(4, (bc, sc, KV_LORA))
    new_c_krope = st(5, (bc, sc, ROPE_C))
    return (y_prompt, y_sample, new_a_k, new_a_v, new_b_k, new_b_v, new_c_kv, new_c_krope)
```

```python
import functools
import math

import numpy as np
import jax
import jax.numpy as jnp
from jax import lax
from jax.experimental import pallas as pl
from jax.experimental.pallas import tpu as pltpu

GRID_W = 64
EPS = 1e-6
ROPE_THETA = 10000.0
H_A, DH_A = 4, 32
H_B, DH_B = 4, 64
WIN_R, WIN_C = 8, 16
H_C, Q_LORA, KV_LORA, NOPE_C, ROPE_C, DV_C = 4, 256, 128, 32, 32, 64
CONV_CH, CONV_K = 256, 31
N_BRANCH = 4
N_GROUPS, EXPERTS_PER_GROUP, N_EXPERTS, D_EXPERT = 4, 4, 16, 256
HEAD_W = 256
LANES = 128
SUBLANES = 8
V7X_VMEM_BYTES = 64 * 1024 * 1024
LOG2E = math.log2(math.e)

_MM = jnp.bfloat16
_F32 = jnp.float32


def _cparams(sem, vmem_mb):
    return pltpu.CompilerParams(dimension_semantics=sem,
                                vmem_limit_bytes=min(vmem_mb * 1024 * 1024, V7X_VMEM_BYTES - 8 * 1024 * 1024))


def _dot(a, b):
    return jnp.dot(a.astype(_MM), b.astype(_MM), preferred_element_type=_F32)


def _dot_nt(a, b):
    return lax.dot_general(a.astype(_MM), b.astype(_MM), (((1,), (1,)), ((), ())),
                           preferred_element_type=_F32)


def _sigmoid(x):
    return 1.0 / (1.0 + jnp.exp(-x))


def _rms(x, g):
    return x * lax.rsqrt(jnp.mean(x * x, axis=-1, keepdims=True) + EPS) * g


def _rope(x, cos, sin):
    w = x.shape[-1]
    lane = lax.broadcasted_iota(jnp.int32, x.shape, 1)
    first = (lane % 16) < 8
    fwd = pltpu.roll(x, w - 8, axis=1)
    bwd = pltpu.roll(x, 8, axis=1)
    return x * cos + jnp.where(first, -fwd, bwd) * sin


def _adaln_kernel(cond_ref, w_ref, b_ref, o_ref):
    c = cond_ref[...]
    a = c * _sigmoid(c)
    o_ref[0] = _dot(a, w_ref[0]) + b_ref[0]


def _adaln(cond8, w_ada, b_ada):
    n_l, d, d6 = w_ada.shape
    tn = 1536
    return pl.pallas_call(
        _adaln_kernel,
        grid=(n_l, d6 // tn),
        in_specs=[pl.BlockSpec((8, d), lambda l, j: (0, 0)),
                  pl.BlockSpec((1, d, tn), lambda l, j: (l, 0, j)),
                  pl.BlockSpec((1, 1, tn), lambda l, j: (l, 0, j))],
        out_specs=pl.BlockSpec((1, 8, tn), lambda l, j: (l, 0, j)),
        out_shape=jax.ShapeDtypeStruct((n_l, 8, d6), _F32),
        compiler_params=_cparams(("arbitrary", "arbitrary"), 40),
        name="adaln",
    )(cond8, w_ada, b_ada.reshape(n_l, 1, d6))


PRE_ROW_PARTS = 2


def _pre_kernel(*refs, d, latent, sa, sb, sc):
    (x_ref, mod_ref, g1_ref, wa_ref, wb_ref, wcq_ref, wckv_ref, wckr_ref, wdu_ref,
     cqg_ref, ckvg_ref, wuq_ref, wkc_ref, wv_ref) = refs[:14]
    refs = refs[14:]
    if latent:
        cosa_ref, sina_ref, cosc_ref, sinc_ref = refs[:4]
        refs = refs[4:]
        aqt_ref, ak_ref, avt_ref, bqt_ref, bk_ref, bvt_ref, qct_ref, kc_ref, vct_ref, z_ref = refs
    else:
        (aqt_ref, ak_ref, av_ref, avt_ref, bqt_ref, bk_ref, bv_ref, bvt_ref, qct_ref, kc_ref, vct_ref,
         ckvn_ref, ckr_ref, z_ref) = refs

    mod = mod_ref[0]
    shift1, scale1 = mod[:, 0:d], mod[:, d:2 * d]
    tm = x_ref.shape[0]
    n_parts = PRE_ROW_PARTS if tm % (PRE_ROW_PARTS * LANES) == 0 else 1
    parts = [slice(k * tm // n_parts, (k + 1) * tm // n_parts) for k in range(n_parts)]

    proj = []
    for rs in parts:
        h = (_rms(x_ref[rs, :], g1_ref[...]) * (1.0 + scale1) + shift1).astype(_MM)
        proj.append([_dot(h, w_ref[...]) for w_ref in (wa_ref, wb_ref, wcq_ref, wckv_ref, wckr_ref, wdu_ref)])

    for rs, (pa, pb, cq, ckv, kr, du) in zip(parts, proj):
        aq, ak, av = pa[:, 0:HEAD_W], pa[:, HEAD_W:2 * HEAD_W], pa[:, 2 * HEAD_W:3 * HEAD_W]
        if latent:
            cosa, sina = cosa_ref[rs, :], sina_ref[rs, :]
            aq = _rope(aq, cosa, sina)
            ak = _rope(ak, cosa, sina)
        else:
            av_ref[rs, :] = av
        aqt_ref[:, rs] = (aq * (sa * LOG2E)).T.astype(aqt_ref.dtype)
        ak_ref[rs, :] = ak.astype(ak_ref.dtype)
        avt_ref[:, rs] = av.T.astype(avt_ref.dtype)

        bq, bk, bv = pb[:, 0:HEAD_W], pb[:, HEAD_W:2 * HEAD_W], pb[:, 2 * HEAD_W:3 * HEAD_W]
        bk_ref[rs, :] = bk.astype(bk_ref.dtype)
        if not latent:
            bv_ref[rs, :] = bv
        bqt_ref[:, rs] = (bq * (sb * LOG2E)).T.astype(bqt_ref.dtype)
        bvt_ref[:, rs] = bv.T.astype(bvt_ref.dtype)

        qc = _dot(_rms(cq, cqg_ref[...]), wuq_ref[...])
        ckvn = _rms(ckv, ckvg_ref[...])
        if latent:
            cosc, sinc = cosc_ref[rs, :], sinc_ref[rs, :]
            qc = _rope(qc, cosc, sinc)
            kr = _rope(kr, cosc, sinc)
        else:
            ckvn_ref[rs, :] = ckvn
            ckr_ref[rs, :] = kr[:, NOPE_C:NOPE_C + ROPE_C]
        qct_ref[:, rs] = (qc * (sc * LOG2E)).T.astype(qct_ref.dtype)
        kc_ref[rs, :] = (_dot(ckvn, wkc_ref[...]) + kr).astype(kc_ref.dtype)
        vct_ref[:, rs] = _dot(ckvn, wv_ref[...]).T.astype(vct_ref.dtype)
        z_ref[rs, :] = du[:, 0:CONV_CH] * _sigmoid(du[:, CONV_CH:2 * CONV_CH])


def _pre(x2d, mods3, mod_row, lw, *, latent, seq, tables):
    t, d = x2d.shape
    tm = min(512, seq)
    n_seq_tiles = seq // tm
    grid = (t // tm,)
    row = lambda i: (i, 0)
    const = lambda i: (0, 0)
    wnames = ("wa", "wb", "wcq", "wckv", "wckr", "wdu", "cqg", "ckvg", "wuq", "wkc", "wv")
    in_specs = [pl.BlockSpec((tm, d), row),
                pl.BlockSpec((1, 1, 6 * d), lambda i: (mod_row(i * tm), 0, 0)),
                pl.BlockSpec((1, d), const)]
    args = [x2d, mods3, lw["g1"]]
    for n in wnames:
        in_specs.append(pl.BlockSpec(lw[n].shape, const))
        args.append(lw[n])
    if latent:
        for tb in tables:
            in_specs.append(pl.BlockSpec((tm, HEAD_W), lambda i: (i % n_seq_tiles, 0)))
            args.append(tb)
    nat = lambda dt: (jax.ShapeDtypeStruct((t, HEAD_W), dt), pl.BlockSpec((tm, HEAD_W), row))
    tr = (jax.ShapeDtypeStruct((t // seq, HEAD_W, seq), _MM),
          pl.BlockSpec((None, HEAD_W, tm), lambda i: (i // n_seq_tiles, 0, i % n_seq_tiles)))
    if latent:
        outs = [tr, nat(_MM), tr, tr, nat(_MM), tr, tr, nat(_MM), tr]
    else:
        outs = [tr, nat(_F32), nat(_F32), tr, tr, nat(_F32), nat(_F32), tr, tr, nat(_MM), tr,
                (jax.ShapeDtypeStruct((t, KV_LORA), _F32), pl.BlockSpec((tm, KV_LORA), row)),
                (jax.ShapeDtypeStruct((t, ROPE_C), _F32), pl.BlockSpec((tm, ROPE_C), row))]
    outs.append((jax.ShapeDtypeStruct((t, CONV_CH), _F32), pl.BlockSpec((tm, CONV_CH), row)))
    out_shape = [o[0] for o in outs]
    out_specs = [o[1] for o in outs]
    kern = functools.partial(_pre_kernel, d=d, latent=latent, sa=DH_A ** -0.5, sb=DH_B ** -0.5,
                             sc=(NOPE_C + ROPE_C) ** -0.5)
    return pl.pallas_call(kern, grid=grid, in_specs=in_specs, out_specs=out_specs, out_shape=out_shape,
                          compiler_params=_cparams(("arbitrary",), 48),
                          name="pre_lat" if latent else "pre_ctx")(*args)


SCORE_LOOKAHEAD = 6
SUM_ROWS = 16
MAX_UNROLLED_KEY_TILES = 16


def _attn_kernel(*refs, n_heads, n_maps, dqk, dv, tk, n_own, unroll, tkc, n_cache, diff, lam_init):
    qt_ref, k_ref, vt_ref = refs[:3]
    refs = refs[3:]
    if n_cache:
        kc_ref, vct_ref = refs[:2]
        refs = refs[2:]
    if diff:
        alam_ref, subg_ref = refs[:2]
        refs = refs[2:]
    o_ref, qz_sc, m_sc, acc_sc, ot_sc = refs
    n_chains = n_heads * n_maps

    qz_sc[...] = jnp.zeros(qz_sc.shape, qz_sc.dtype)
    for c in range(n_chains):
        qz_sc[c, (c * dqk) % LANES:(c * dqk) % LANES + dqk, :] = qt_ref[c * dqk:(c + 1) * dqk, :]
    m_sc[...] = jnp.full(m_sc.shape, -jnp.inf, _F32)
    acc_sc[...] = jnp.zeros(acc_sc.shape, _F32)

    def update(tiles):
        steps = [(ti, c) for ti in range(len(tiles)) for c in range(n_chains)]
        kts, vts = {}, {}

        def scores(ti, c):
            if ti not in kts:
                kts[ti] = tiles[ti][0]().astype(_MM)
            grp = (c * dqk) // LANES
            return jnp.dot(kts[ti][:, grp * LANES:(grp + 1) * LANES], qz_sc[c], preferred_element_type=_F32)

        def values(ti, h):
            if (ti, h) not in vts:
                vt = tiles[ti][1](h).astype(_MM)
                vts[ti, h] = jnp.concatenate([vt, jnp.ones((SUM_ROWS, vt.shape[1]), _MM)], axis=0)
            return vts[ti, h]

        pending = [scores(*st) for st in steps[:SCORE_LOOKAHEAD]]
        for idx, (ti, c) in enumerate(steps):
            s = pending.pop(0)
            if idx + SCORE_LOOKAHEAD < len(steps):
                pending.append(scores(*steps[idx + SCORE_LOOKAHEAD]))
            m_prev = m_sc[c]
            m_new = jnp.maximum(m_prev, jnp.max(s, axis=0, keepdims=True))
            alpha = jnp.exp2(m_prev - m_new)
            p = jnp.exp2(s - m_new).astype(_MM)
            acc_sc[c] = alpha * acc_sc[c] + jnp.dot(values(ti, c // n_maps), p, preferred_element_type=_F32)
            m_sc[c] = m_new

    cache_tiles = [(lambda j=j: kc_ref[j * tkc:(j + 1) * tkc, :],
                    lambda h, j=j: vct_ref[h * dv:(h + 1) * dv, j * tkc:(j + 1) * tkc]) for j in range(n_cache)]

    def own_tiles(starts):
        return [(lambda st=st: k_ref[pl.ds(st, tk), :],
                 lambda h, st=st: vt_ref[h * dv:(h + 1) * dv, pl.ds(st, tk)]) for st in starts]

    if unroll == n_own:
        update(cache_tiles + own_tiles([u * tk for u in range(n_own)]))
    else:
        if n_cache:
            update(cache_tiles)

        def body(j, carry):
            update(own_tiles([pl.multiple_of((j * unroll + u) * tk, tk) for u in range(unroll)]))
            return carry

        lax.fori_loop(0, n_own // unroll, body, 0)

    def normalised(c):
        a = acc_sc[c]
        return a[0:dv] / a[dv:dv + 1]

    if diff:
        lv = alam_ref[...]
        lam = (jnp.exp(jnp.sum(lv[0:1] * lv[1:2], axis=-1, keepdims=True))
               - jnp.exp(jnp.sum(lv[2:3] * lv[3:4], axis=-1, keepdims=True)) + lam_init)
    for h in range(n_heads):
        if diff:
            o = normalised(2 * h) - lam * normalised(2 * h + 1)
            o = o * lax.rsqrt(jnp.mean(o * o, axis=0, keepdims=True) + EPS) * subg_ref[...] * (1.0 - lam_init)
        else:
            o = normalised(h)
        ot_sc[h * dv:(h + 1) * dv, :] = o
    o_ref[...] = ot_sc[...].T.astype(o_ref.dtype)


def _attn(qt, k, vt, cache, diff_args, *, n_heads, n_maps, dqk, dv, lam_init=0.0, name):
    nb, s, _ = k.shape
    tq = min(256, s)
    tk = min(256, s)
    n_own = s // tk
    qmap = lambda b, i: (b, 0, i)
    bmap = lambda b, i: (b, 0, 0)
    in_specs = [pl.BlockSpec((None, HEAD_W, tq), qmap),
                pl.BlockSpec((None, s, HEAD_W), bmap),
                pl.BlockSpec((None, HEAD_W, s), bmap)]
    args = [qt, k, vt]
    tkc, n_cache = tk, 0
    if cache is not None:
        past = cache[0].shape[1]
        tkc = min(256, past)
        n_cache = past // tkc
        in_specs += [pl.BlockSpec((None, past, HEAD_W), bmap), pl.BlockSpec((None, HEAD_W, past), bmap)]
        args += list(cache)
    if diff_args is not None:
        in_specs += [pl.BlockSpec(diff_args[0].shape, lambda b, i: (0, 0)),
                     pl.BlockSpec(diff_args[1].shape, lambda b, i: (0, 0))]
        args += list(diff_args)
    n_chains = n_heads * n_maps
    unroll = n_own if n_own <= MAX_UNROLLED_KEY_TILES else (8 if n_own % 8 == 0 else 1)
    kern = functools.partial(_attn_kernel, n_heads=n_heads, n_maps=n_maps, dqk=dqk, dv=dv, tk=tk, n_own=n_own,
                             unroll=unroll, tkc=tkc, n_cache=n_cache, diff=diff_args is not None, lam_init=lam_init)
    return pl.pallas_call(
        kern, grid=(nb, s // tq), in_specs=in_specs,
        out_specs=pl.BlockSpec((None, tq, HEAD_W), lambda b, i: (b, i, 0)),
        out_shape=jax.ShapeDtypeStruct((nb, s, HEAD_W), _MM),
        scratch_shapes=[pltpu.VMEM((n_chains, LANES, tq), _MM), pltpu.VMEM((n_chains, 1, tq), _F32),
                        pltpu.VMEM((n_chains, dv + SUM_ROWS, tq), _F32), pltpu.VMEM((n_heads * dv, tq), _F32)],
        compiler_params=_cparams(("arbitrary", "arbitrary"), 48),
        name=name,
    )(*args)


NBR_ROWS = 4
NBR_SPAN_ROWS = 12


def _nbr_kernel(qt_ref, k_ref, vt_ref, kc_ref, vct_ref, tb_ref, o_ref, qz_sc, ot_sc, *, rows):
    i = pl.program_id(1)
    ss = jnp.clip(i * NBR_ROWS - WIN_R // 2, 0, rows - NBR_SPAN_ROWS)
    kstart = pl.multiple_of(ss * GRID_W, NBR_ROWS * GRID_W)
    span = NBR_SPAN_ROWS * GRID_W

    qz_sc[...] = jnp.zeros(qz_sc.shape, qz_sc.dtype)
    for h in range(H_B):
        qz_sc[h, (h * DH_B) % LANES:(h * DH_B) % LANES + DH_B, :] = qt_ref[h * DH_B:(h + 1) * DH_B, :]
    ks = k_ref[pl.ds(kstart, span), :].astype(_MM)
    kc = kc_ref[...].astype(_MM)

    def scores(h):
        gsl = slice((h * DH_B) // LANES * LANES, ((h * DH_B) // LANES + 1) * LANES)
        return (jnp.dot(ks[:, gsl], qz_sc[h], preferred_element_type=_F32),
                jnp.dot(kc[:, gsl], qz_sc[h], preferred_element_type=_F32))

    def with_ones(vt):
        return jnp.concatenate([vt.astype(_MM), jnp.ones((SUM_ROWS, vt.shape[1]), _MM)], axis=0)

    pending = scores(0)
    for h in range(H_B):
        s_w, s_c = pending
        if h + 1 < H_B:
            pending = scores(h + 1)
        s_w = s_w + tb_ref[h]
        m = jnp.maximum(jnp.max(s_w, axis=0, keepdims=True), jnp.max(s_c, axis=0, keepdims=True))
        p_w = jnp.exp2(s_w - m).astype(_MM)
        p_c = jnp.exp2(s_c - m).astype(_MM)
        hsl = slice(h * DH_B, (h + 1) * DH_B)
        acc = (jnp.dot(with_ones(vt_ref[hsl, pl.ds(kstart, span)]), p_w, preferred_element_type=_F32)
               + jnp.dot(with_ones(vct_ref[hsl, :]), p_c, preferred_element_type=_F32))
        ot_sc[hsl, :] = acc[0:DH_B] / acc[DH_B:DH_B + 1]
    o_ref[...] = ot_sc[...].T.astype(o_ref.dtype)


def _nbr_bias_table(rpb, rows):
    nh, n_r, n_c = rpb.shape
    c = np.arange(GRID_W)[:, None]
    kc = np.arange(GRID_W)[None, :]
    cidx = np.clip(kc - c + WIN_C - 1, 0, n_c - 1).reshape(1, -1)
    onehot = jnp.asarray((cidx == np.arange(n_c)[:, None]).astype(np.float32))
    toep = jnp.dot(rpb.reshape(nh * n_r, n_c).astype(_F32), onehot, precision=lax.Precision.HIGHEST)
    toep = toep.reshape(nh, n_r, GRID_W, GRID_W)
    cstart = np.clip(c - WIN_C // 2, 0, GRID_W - WIN_C)
    valid = (kc >= cstart) & (kc < cstart + WIN_C)
    toep = jnp.where(valid[None, None], toep, -jnp.inf) * LOG2E
    assert rows % NBR_ROWS == 0 and rows >= NBR_SPAN_ROWS
    r0 = np.array([0, NBR_ROWS, rows - NBR_ROWS])[:, None, None]
    ss = np.clip(r0 - WIN_R // 2, 0, rows - NBR_SPAN_ROWS)
    key_row = ss + np.arange(NBR_SPAN_ROWS)[None, :, None]
    r = r0 + np.arange(NBR_ROWS)[None, None, :]
    rs = np.clip(r - WIN_R // 2, 0, rows - WIN_R)
    in_rows = (key_row >= rs) & (key_row < rs + WIN_R)
    ridx = np.clip(key_row - r + WIN_R - 1, 0, n_r - 1)
    tb = jnp.take(toep, jnp.asarray(ridx.reshape(-1)), axis=1)
    tb = tb.reshape((nh,) + ridx.shape + (GRID_W, GRID_W))
    tb = jnp.where(in_rows[None, :, :, :, None, None], tb, -jnp.inf)
    tb = tb.transpose(1, 0, 2, 5, 3, 4)
    return tb.reshape(3, nh, NBR_SPAN_ROWS * GRID_W, NBR_ROWS * GRID_W)


def _nbr(qt, k, vt, kc, vct, tb):
    nb, s, _ = k.shape
    rows = s // GRID_W
    n_steps = rows // NBR_ROWS
    past = kc.shape[1]
    tq = NBR_ROWS * GRID_W
    bmap = lambda b, i: (b, 0, 0)
    variant = lambda b, i: (jnp.where(i == 0, 0, jnp.where(i == n_steps - 1, 2, 1)), 0, 0, 0)
    return pl.pallas_call(
        functools.partial(_nbr_kernel, rows=rows), grid=(nb, n_steps),
        in_specs=[pl.BlockSpec((None, HEAD_W, tq), lambda b, i: (b, 0, i)),
                  pl.BlockSpec((None, s, HEAD_W), bmap),
                  pl.BlockSpec((None, HEAD_W, s), bmap),
                  pl.BlockSpec((None, past, HEAD_W), bmap),
                  pl.BlockSpec((None, HEAD_W, past), bmap),
                  pl.BlockSpec((None,) + tb.shape[1:], variant)],
        out_specs=pl.BlockSpec((None, tq, HEAD_W), lambda b, i: (b, i, 0)),
        out_shape=jax.ShapeDtypeStruct((nb, s, HEAD_W), _MM),
        scratch_shapes=[pltpu.VMEM((H_B, LANES, tq), _MM), pltpu.VMEM((HEAD_W, tq), _F32)],
        compiler_params=_cparams(("arbitrary", "arbitrary"), 48),
        name="nbr_attn",
    )(qt, k, vt, kc, vct, tb)


def _mla_cache_kernel(ckv_ref, kr_ref, wkc_ref, wvt_ref, e_ref, kc_ref, vct_ref):
    ckv = ckv_ref[...]
    kc_ref[...] = (_dot(ckv, wkc_ref[...]) + _dot(kr_ref[...], e_ref[...])).astype(kc_ref.dtype)
    vct_ref[...] = _dot_nt(wvt_ref[...], ckv).astype(vct_ref.dtype)


def _mla_cache(ckv, kr, wkc, wvt, expand):
    nb, past, _ = ckv.shape
    bmap = lambda b: (b, 0, 0)
    const = lambda b: (0, 0)
    return pl.pallas_call(
        _mla_cache_kernel, grid=(nb,),
        in_specs=[pl.BlockSpec((None, past, KV_LORA), bmap), pl.BlockSpec((None, past, ROPE_C), bmap),
                  pl.BlockSpec(wkc.shape, const), pl.BlockSpec(wvt.shape, const), pl.BlockSpec(expand.shape, const)],
        out_specs=[pl.BlockSpec((None, past, HEAD_W), bmap), pl.BlockSpec((None, HEAD_W, past), bmap)],
        out_shape=[jax.ShapeDtypeStruct((nb, past, HEAD_W), _MM), jax.ShapeDtypeStruct((nb, HEAD_W, past), _MM)],
        compiler_params=_cparams(("arbitrary",), 32),
        name="mla_cache",
    )(ckv, kr, wkc, wvt, expand)


CONV_HALO = 16


def _conv_kernel(z_ref, w_ref, b_ref, g_ref, beta_ref, o_ref, zp_ref, *, seq, rc):
    zp_ref[0:CONV_HALO, :] = jnp.zeros((CONV_HALO, CONV_CH), _F32)
    zp_ref[CONV_HALO:CONV_HALO + seq, :] = z_ref[...]
    zp_ref[CONV_HALO + seq:2 * CONV_HALO + seq, :] = jnp.zeros((CONV_HALO, CONV_CH), _F32)
    w = w_ref[...]
    shift = CONV_HALO - CONV_K // 2

    def chunk(c, carry):
        base = pl.multiple_of(c * rc, rc)
        n_win = rc + 2 * CONV_HALO
        win = zp_ref[pl.ds(base, n_win), :]
        acc = jnp.zeros((rc, CONV_CH), _F32)
        for phase in range(SUBLANES):
            rolled = win if phase == 0 else pltpu.roll(win, n_win - phase, axis=0)
            for j in range(CONV_K):
                if (j + shift) % SUBLANES == phase:
                    a = (j + shift) - phase
                    acc = acc + rolled[a:a + rc, :] * w[j:j + 1, :]
        y = acc + b_ref[...]
        mu = jnp.mean(y, axis=-1, keepdims=True)
        yc = y - mu
        var = jnp.mean(yc * yc, axis=-1, keepdims=True)
        yn = yc * lax.rsqrt(var + EPS) * g_ref[...] + beta_ref[...]
        o_ref[pl.ds(base, rc), :] = (yn * _sigmoid(yn)).astype(o_ref.dtype)
        return carry

    lax.fori_loop(0, seq // rc, chunk, 0)


def _conv(z, w, b, g, beta):
    nb, seq, _ = z.shape
    rc = min(128, seq)
    bmap = lambda i: (i, 0, 0)
    const = lambda i: (0, 0)
    kern = functools.partial(_conv_kernel, seq=seq, rc=rc)
    return pl.pallas_call(
        kern, grid=(nb,),
        in_specs=[pl.BlockSpec((None, seq, CONV_CH), bmap), pl.BlockSpec(w.shape, const),
                  pl.BlockSpec(b.shape, const), pl.BlockSpec(g.shape, const), pl.BlockSpec(beta.shape, const)],
        out_specs=pl.BlockSpec((None, seq, CONV_CH), bmap),
        out_shape=jax.ShapeDtypeStruct((nb, seq, CONV_CH), _MM),
        scratch_shapes=[pltpu.VMEM((seq + 2 * CONV_HALO, CONV_CH), _F32)],
        compiler_params=_cparams(("arbitrary",), 40),
        name="conv",
    )(z, w, b, g, beta)


POST_ROW_PARTS = 2

def _post_kernel(x_ref, mod_ref, oa_ref, ob_ref, oc_ref, od_ref, g1_ref, wg_ref, bg_ref,
                 awo_ref, bwo_ref, cwo_ref, dwo_ref, wout_ref, g2_ref, rw_ref, rb_ref,
                 x1_ref, h2_ref, dw_ref, *, d):
    mod = mod_ref[0]
    shift1, scale1, gate1 = mod[:, 0:d], mod[:, d:2 * d], mod[:, 2 * d:3 * d]
    shift2, scale2 = mod[:, 3 * d:4 * d], mod[:, 4 * d:5 * d]
    branches = ((oa_ref, awo_ref), (ob_ref, bwo_ref), (oc_ref, cwo_ref), (od_ref, dwo_ref))
    tm = x_ref.shape[0]
    parts = [slice(k * tm // POST_ROW_PARTS, (k + 1) * tm // POST_ROW_PARTS) for k in range(POST_ROW_PARTS)]

    xs, branch_mm = [], []
    for rs in parts:
        x = x_ref[rs, :]
        h = (_rms(x, g1_ref[...]) * (1.0 + scale1) + shift1).astype(_MM)
        xs.append(x)
        branch_mm.append([(_dot(h, wg_ref[:, bi * d:(bi + 1) * d]), _dot(o_ref[rs, :], wo_ref[...]))
                          for bi, (o_ref, wo_ref) in enumerate(branches)])
    mixed = []
    for pairs in branch_mm:
        merged = None
        for bi, (gate_logits, out) in enumerate(pairs):
            term = _sigmoid(gate_logits + bg_ref[:, bi * d:(bi + 1) * d]) * out
            merged = term if merged is None else merged + term
        mixed.append(_dot(merged, wout_ref[...]))
    logits = []
    for rs, x, y in zip(parts, xs, mixed):
        x1 = x + gate1 * y
        x1_ref[rs, :] = x1
        h2_mm = (_rms(x1, g2_ref[...]) * (1.0 + scale2) + shift2).astype(_MM)
        h2_ref[rs, :] = h2_mm
        logits.append(_dot(h2_mm, rw_ref[...]) + rb_ref[...])
    for rs, r in zip(parts, logits):
        lgp, lep = r[:, 0:LANES], r[:, LANES:2 * LANES]
        lane = lax.broadcasted_iota(jnp.int32, lgp.shape, 1)
        neg = -jnp.inf
        lg = jnp.where(lane < N_GROUPS, lgp, neg)
        mg = jnp.max(lg, axis=-1, keepdims=True)
        gi = jnp.min(jnp.where(lg == mg, lane, LANES), axis=-1, keepdims=True)
        gw = 1.0 / jnp.sum(jnp.exp(lg - mg), axis=-1, keepdims=True)
        in_group = (lane < N_EXPERTS) & ((lane // EXPERTS_PER_GROUP) == gi)
        le = jnp.where(in_group, lep, neg)
        t1 = jnp.max(le, axis=-1, keepdims=True)
        i1 = jnp.min(jnp.where(le == t1, lane, LANES), axis=-1, keepdims=True)
        le2 = jnp.where(lane == i1, neg, le)
        t2 = jnp.max(le2, axis=-1, keepdims=True)
        i2 = jnp.min(jnp.where(le2 == t2, lane, LANES), axis=-1, keepdims=True)
        e2 = jnp.exp(t2 - t1)
        w1 = gw / (1.0 + e2)
        w2 = gw * e2 / (1.0 + e2)
        dw_ref[rs, :] = jnp.where(lane == i1, w1, 0.0) + jnp.where(lane == i2, w2, 0.0)


def _post(x2d, mods3, mod_row, oa, ob, oc, od, lw):
    t, d = x2d.shape
    tm = min(512, t)
    row = lambda i: (i, 0)
    const = lambda i: (0, 0)
    wnames = ("g1", "wgate", "bgate", "awo", "bwo", "cwo", "dwo", "wout", "g2", "rw", "rb")
    in_specs = [pl.BlockSpec((tm, d), row), pl.BlockSpec((1, 1, 6 * d), lambda i: (mod_row(i * tm), 0, 0))]
    in_specs += [pl.BlockSpec((tm, HEAD_W), row)] * 4
    in_specs += [pl.BlockSpec(lw[n].shape, const) for n in wnames]
    args = [x2d, mods3, oa, ob, oc, od] + [lw[n] for n in wnames]
    return pl.pallas_call(
        functools.partial(_post_kernel, d=d), grid=(t // tm,), in_specs=in_specs,
        out_specs=[pl.BlockSpec((tm, d), row), pl.BlockSpec((tm, d), row), pl.BlockSpec((tm, LANES), row)],
        out_shape=[jax.ShapeDtypeStruct((t, d), _F32), jax.ShapeDtypeStruct((t, d), _MM),
                   jax.ShapeDtypeStruct((t, LANES), _F32)],
        compiler_params=_cparams(("arbitrary",), 56),
        name="post",
    )(*args)


def _moe_kernel(h2_ref, dw_ref, x1_ref, mod_ref, w1_ref, w2_ref, fg_ref, o_ref, acc_ref, *, d, final):
    g = pl.program_id(1)

    @pl.when(g == 0)
    def _():
        acc_ref[...] = jnp.zeros_like(acc_ref)

    h2 = h2_ref[...]
    dw = dw_ref[...]
    lane = lax.broadcasted_iota(jnp.int32, dw.shape, 1)
    acts = []
    for j in range(EXPERTS_PER_GROUP):
        hid = _dot(h2, w1_ref[j])
        gate, up = hid[:, 0:D_EXPERT], hid[:, D_EXPERT:2 * D_EXPERT]
        we = jnp.sum(jnp.where(lane == g * EXPERTS_PER_GROUP + j, dw, 0.0), axis=-1, keepdims=True)
        acts.append((gate * _sigmoid(gate) * up * we).astype(_MM))
    acc_ref[...] += _dot(jnp.concatenate(acts, axis=-1), w2_ref[...])

    @pl.when(g == pl.num_programs(1) - 1)
    def _():
        gate2 = mod_ref[0][:, 5 * d:6 * d]
        x2 = x1_ref[...] + gate2 * acc_ref[...]
        if final:
            x2 = _rms(x2, fg_ref[...])
        o_ref[...] = x2


def _moe(h2, dw, x1, mods3, mod_row, w1, w2, fg, *, final):
    t, d = x1.shape
    tm = min(1024, t)
    n_e, _, d_hid = w1.shape
    w2g = w2.reshape(N_GROUPS, EXPERTS_PER_GROUP * D_EXPERT, d)
    row = lambda i, g: (i, 0)
    return pl.pallas_call(
        functools.partial(_moe_kernel, d=d, final=final), grid=(t // tm, N_GROUPS),
        in_specs=[pl.BlockSpec((tm, d), row), pl.BlockSpec((tm, LANES), row), pl.BlockSpec((tm, d), row),
                  pl.BlockSpec((1, 1, 6 * d), lambda i, g: (mod_row(i * tm), 0, 0)),
                  pl.BlockSpec((EXPERTS_PER_GROUP, d, d_hid), lambda i, g: (g, 0, 0)),
                  pl.BlockSpec((None,) + w2g.shape[1:], lambda i, g: (g, 0, 0)),
                  pl.BlockSpec((1, d), lambda i, g: (0, 0))],
        out_specs=pl.BlockSpec((tm, d), row),
        out_shape=jax.ShapeDtypeStruct((t, d), _F32),
        scratch_shapes=[pltpu.VMEM((tm, d), _F32)],
        compiler_params=_cparams(("arbitrary", "arbitrary"), 56),
        name="moe_final" if final else "moe",
    )(h2, dw, x1, mods3, w1, w2g, fg)


def _rope_tables(n_tok, dim):
    hq = dim // 4
    tpos = np.arange(n_tok)
    rows = (tpos // GRID_W).astype(np.float32)
    cols = (tpos % GRID_W).astype(np.float32)
    inv = jnp.asarray(ROPE_THETA, _F32) ** (-jnp.arange(hq, dtype=_F32) / hq)
    ar = jnp.asarray(rows)[:, None] * inv[None, :]
    ac = jnp.asarray(cols)[:, None] * inv[None, :]
    ang = jnp.concatenate([ar, ar, ac, ac], axis=-1)
    return jnp.cos(ang), jnp.sin(ang)


def _layer_weights(l, p, grid_rows):
    mm = lambda a: a.astype(_MM)
    w_in = p["w_in"][l]
    pts = np.cumsum([0, 2 * H_A * DH_A, 2 * H_A * DH_A, 2 * H_A * DH_A, H_B * DH_B, H_B * DH_B, H_B * DH_B,
                     Q_LORA, KV_LORA, ROPE_C, 2 * CONV_CH])
    seg = lambda i, j: w_in[:, pts[i]:pts[j]]
    d = w_in.shape[0]
    wckr = seg(8, 9)
    zeros_n = jnp.zeros((d, NOPE_C), w_in.dtype)
    wckr_t = jnp.concatenate([zeros_n, wckr] * H_C, axis=1)
    wukv = p["c_wukv"][l].reshape(KV_LORA, H_C, NOPE_C + DV_C)
    wkc = jnp.concatenate([wukv[:, :, :NOPE_C], jnp.zeros((KV_LORA, H_C, ROPE_C), wukv.dtype)], axis=-1)
    wkc = wkc.reshape(KV_LORA, H_C * (NOPE_C + ROPE_C))
    wv = wukv[:, :, NOPE_C:].reshape(KV_LORA, H_C * DV_C)
    router = jnp.zeros((d, 2 * LANES), _F32)
    router = router.at[:, 0:N_GROUPS].set(p["moe_w_group"][l]).at[:, LANES:LANES + N_EXPERTS].set(p["moe_w_expert"][l])
    rb = jnp.zeros((1, 2 * LANES), _F32)
    rb = rb.at[0, 0:N_GROUPS].set(p["moe_b_group"][l]).at[0, LANES:LANES + N_EXPERTS].set(p["moe_b_expert"][l])
    conv_w = p["d_conv_w"][l]
    return dict(
        g1=p["norm1_g"][l][None], g2=p["norm2_g"][l][None],
        wa=mm(seg(0, 3)), wb=mm(seg(3, 6)), wcq=mm(seg(6, 7)), wckv=mm(seg(7, 8)), wckr=mm(wckr_t), wdu=mm(seg(9, 10)),
        cqg=p["c_q_norm_g"][l][None], ckvg=p["c_kv_norm_g"][l][None],
        wuq=mm(p["c_wuq"][l]), wkc=mm(wkc), wv=mm(wv), wvt=mm(wv.T),
        alam=p["a_lambda"][l], subg=p["a_subln_g"][l][:, None],
        wgate=mm(p["w_gate"][l]), bgate=p["b_gate"][l][None],
        awo=mm(p["a_wo"][l]), bwo=mm(p["b_wo"][l]), cwo=mm(p["c_wo"][l]), dwo=mm(p["d_wo"][l]),
        wout=mm(p["w_out"][l]), rw=mm(router), rb=rb,
        conv_w=conv_w, conv_b=p["d_conv_b"][l][None], ln_g=p["d_ln_g"][l][None], ln_b=p["d_ln_b"][l][None],
        w1=mm(p["moe_w1"][l]), w2=mm(p["moe_w2"][l]),
        tb=_nbr_bias_table(p["b_rpb"][l], grid_rows),
    )


def _layer(x2d, mods3, mod_row, lw, l, n_layers, *, nb, seq, fg, ctx, tables):
    latent = ctx is not None
    t = x2d.shape[0]
    outs = _pre(x2d, mods3, mod_row, lw, latent=latent, seq=seq, tables=tables)
    r3 = lambda a: a.reshape(nb, seq, a.shape[-1])
    lam_init = 0.8 - 0.6 * math.exp(-0.3 * l)
    diff_args = (lw["alam"], lw["subg"])
    if latent:
        aqt, ak, avt, bqt, bk, bvt, qct, kc, vct, z = outs
        own = None
        a_k_c, a_vt_c, b_k_c, b_vt_c, c_k_c, c_vt_c = ctx
        oa = _attn(aqt, r3(ak), avt, (a_k_c, a_vt_c), diff_args, n_heads=H_A, n_maps=2,
                   dqk=DH_A, dv=2 * DH_A, lam_init=lam_init, name="attn_a_lat")
        ob = _nbr(bqt, r3(bk), bvt, b_k_c, b_vt_c, lw["tb"])
        oc = _attn(qct, r3(kc), vct, (c_k_c, c_vt_c), None, n_heads=H_C, n_maps=1,
                   dqk=NOPE_C + ROPE_C, dv=DV_C, name="attn_c_lat")
    else:
        aqt, ak, av, avt, bqt, bk, bv, bvt, qct, kc, vct, ckvn, ckr, z = outs
        own = (ak, av, bk, bv, ckvn, ckr)
        oa = _attn(aqt, r3(ak), avt, None, diff_args, n_heads=H_A, n_maps=2,
                   dqk=DH_A, dv=2 * DH_A, lam_init=lam_init, name="attn_a_ctx")
        ob = _attn(bqt, r3(bk), bvt, None, None, n_heads=H_B, n_maps=1, dqk=DH_B, dv=DH_B, name="attn_b_ctx")
        oc = _attn(qct, r3(kc), vct, None, None, n_heads=H_C, n_maps=1,
                   dqk=NOPE_C + ROPE_C, dv=DV_C, name="attn_c_ctx")
    od = _conv(r3(z), lw["conv_w"], lw["conv_b"], lw["ln_g"], lw["ln_b"])
    f2 = lambda a: a.reshape(t, a.shape[-1])
    x1, h2, dw = _post(x2d, mods3, mod_row, f2(oa), f2(ob), f2(oc), f2(od), lw)
    x2 = _moe(h2, dw, x1, mods3, mod_row, lw["w1"], lw["w2"], fg, final=(l == n_layers - 1))
    return x2, own


def kernel(x_prompt, x_sample, cache_a_k, cache_a_v, cache_b_k, cache_b_v, cache_c_kv, cache_c_krope, c, c_ctx, norm1_g, norm2_g, w_ada, b_ada, w_in, a_lambda, a_subln_g, a_wo, b_rpb, b_wo, c_q_norm_g, c_kv_norm_g, c_wuq, c_wukv, c_wo, d_conv_w, d_conv_b, d_ln_g, d_ln_b, d_wo, w_gate, b_gate, w_out, moe_w_group, moe_b_group, moe_w_expert, moe_b_expert, moe_w1, moe_w2, final_norm_g):
    p = dict(norm1_g=norm1_g, norm2_g=norm2_g, w_in=w_in, a_lambda=a_lambda, a_subln_g=a_subln_g, a_wo=a_wo,
             b_rpb=b_rpb, b_wo=b_wo, c_q_norm_g=c_q_norm_g, c_kv_norm_g=c_kv_norm_g, c_wuq=c_wuq, c_wukv=c_wukv,
             c_wo=c_wo, d_conv_w=d_conv_w, d_conv_b=d_conv_b, d_ln_g=d_ln_g, d_ln_b=d_ln_b, d_wo=d_wo,
             w_gate=w_gate, b_gate=b_gate, w_out=w_out, moe_w_group=moe_w_group, moe_b_group=moe_b_group,
             moe_w_expert=moe_w_expert, moe_b_expert=moe_b_expert, moe_w1=moe_w1, moe_w2=moe_w2)
    n_layers = w_in.shape[0]
    bc, sc, d = x_prompt.shape
    bl, sl, _ = x_sample.shape
    past = cache_a_k.shape[2]
    assert bl + 1 <= 8 and sl % GRID_W == 0

    cond8 = jnp.zeros((8, d), _F32).at[0].set(c_ctx).at[1:1 + bl].set(c)
    mods3 = _adaln(cond8, w_ada, b_ada).reshape(n_layers * 8, 1, 6 * d)
    fg = final_norm_g[None]

    cos_a, sin_a = _rope_tables(sl, DH_A)
    cos_c, sin_c = _rope_tables(sl, ROPE_C)
    reps_a = HEAD_W // DH_A
    ones_n, zeros_n = jnp.ones((sl, NOPE_C), _F32), jnp.zeros((sl, NOPE_C), _F32)
    tables = (jnp.tile(cos_a, (1, reps_a)), jnp.tile(sin_a, (1, reps_a)),
              jnp.concatenate([ones_n, cos_c] * H_C, axis=1), jnp.concatenate([zeros_n, sin_c] * H_C, axis=1))
    expand = np.zeros((ROPE_C, HEAD_W), np.float32)
    for hh in range(H_C):
        expand[np.arange(ROPE_C), hh * (NOPE_C + ROPE_C) + NOPE_C + np.arange(ROPE_C)] = 1.0
    expand = jnp.asarray(expand, _MM)

    lws = [_layer_weights(l, p, sl // GRID_W) for l in range(n_layers)]

    xp = x_prompt.reshape(bc * sc, d)
    states = []
    for l in range(n_layers):
        xp, own = _layer(xp, mods3, (lambda tok, l=l: l * 8), lws[l], l, n_layers, nb=bc, seq=sc, fg=fg, ctx=None,
                         tables=None)
        states.append(own)
    y_prompt = xp.reshape(bc, sc, d)

    xs = x_sample.reshape(bl * sl, d)
    for l in range(n_layers):
        flat = lambda a: a[:, l].reshape(bl, past, -1)
        c_k_c, c_vt_c = _mla_cache(flat(cache_c_kv), flat(cache_c_krope), lws[l]["wkc"], lws[l]["wvt"], expand)
        ctx = (flat(cache_a_k).astype(_MM), jnp.swapaxes(flat(cache_a_v), 1, 2).astype(_MM),
               flat(cache_b_k).astype(_MM), jnp.swapaxes(flat(cache_b_v), 1, 2).astype(_MM), c_k_c, c_vt_c)
        xs, _ = _layer(xs, mods3, (lambda tok, l=l: l * 8 + 1 + tok // sl), lws[l], l, n_layers,
                       nb=bl, seq=sl, fg=fg, ctx=ctx, tables=tables)
    y_sample = xs.reshape(bl, sl, d)

    st = lambda k, shape: jnp.stack([s[k].reshape(shape) for s in states], axis=1)
    new_a_k = st(0, (bc, sc, H_A, 2 * DH_A))
    new_a_v = st(1, (bc, sc, H_A, 2 * DH_A))
    new_b_k = st(2, (bc, sc, H_B, DH_B))
    new_b_v = st(3, (bc, sc, H_B, DH_B))
    new_c_kv = st(4, (bc, sc, KV_LORA))
    new_c_krope = st(5, (bc, sc, ROPE_C))
    return (y_prompt, y_sample, new_a_k, new_a_v, new_b_k, new_b_v, new_c_kv, new_c_krope)
```

```python
import functools
import math

import numpy as np
import jax
import jax.numpy as jnp
from jax import lax
from jax.experimental import pallas as pl
from jax.experimental.pallas import tpu as pltpu

GRID_W = 64
EPS = 1e-6
ROPE_THETA = 10000.0
H_A, DH_A = 4, 32
H_B, DH_B = 4, 64
WIN_R, WIN_C = 8, 16
H_C, Q_LORA, KV_LORA, NOPE_C, ROPE_C, DV_C = 4, 256, 128, 32, 32, 64
CONV_CH, CONV_K = 256, 31
N_BRANCH = 4
N_GROUPS, EXPERTS_PER_GROUP, N_EXPERTS, D_EXPERT = 4, 4, 16, 256
HEAD_W = 256
LANES = 128
SUBLANES = 8
V7X_VMEM_BYTES = 64 * 1024 * 1024
LOG2E = math.log2(math.e)

_MM = jnp.bfloat16
_F32 = jnp.float32


def _cparams(sem, vmem_mb):
    return pltpu.CompilerParams(dimension_semantics=sem,
                                vmem_limit_bytes=min(vmem_mb * 1024 * 1024, V7X_VMEM_BYTES - 8 * 1024 * 1024))


def _dot(a, b):
    return jnp.dot(a.astype(_MM), b.astype(_MM), preferred_element_type=_F32)


def _dot_nt(a, b):
    return lax.dot_general(a.astype(_MM), b.astype(_MM), (((1,), (1,)), ((), ())),
                           preferred_element_type=_F32)


def _sigmoid(x):
    return 1.0 / (1.0 + jnp.exp(-x))


def _rms(x, g):
    return x * lax.rsqrt(jnp.mean(x * x, axis=-1, keepdims=True) + EPS) * g


def _rope(x, cos, sin):
    w = x.shape[-1]
    lane = lax.broadcasted_iota(jnp.int32, x.shape, 1)
    first = (lane % 16) < 8
    fwd = pltpu.roll(x, w - 8, axis=1)
    bwd = pltpu.roll(x, 8, axis=1)
    return x * cos + jnp.where(first, -fwd, bwd) * sin


def _adaln_kernel(cond_ref, w_ref, b_ref, o_ref):
    c = cond_ref[...]
    a = c * _sigmoid(c)
    o_ref[0] = _dot(a, w_ref[0]) + b_ref[0]


def _adaln(cond8, w_ada, b_ada):
    n_l, d, d6 = w_ada.shape
    tn = 1536
    return pl.pallas_call(
        _adaln_kernel,
        grid=(n_l, d6 // tn),
        in_specs=[pl.BlockSpec((8, d), lambda l, j: (0, 0)),
                  pl.BlockSpec((1, d, tn), lambda l, j: (l, 0, j)),
                  pl.BlockSpec((1, 1, tn), lambda l, j: (l, 0, j))],
        out_specs=pl.BlockSpec((1, 8, tn), lambda l, j: (l, 0, j)),
        out_shape=jax.ShapeDtypeStruct((n_l, 8, d6), _F32),
        compiler_params=_cparams(("arbitrary", "arbitrary"), 40),
        name="adaln",
    )(cond8, w_ada, b_ada.reshape(n_l, 1, d6))


PRE_ROW_PARTS = 2


def _pre_kernel(*refs, d, latent, sa, sb, sc):
    (x_ref, mod_ref, g1_ref, wa_ref, wb_ref, wcq_ref, wckv_ref, wckr_ref, wdu_ref,
     cqg_ref, ckvg_ref, wuq_ref, wkc_ref, wv_ref) = refs[:14]
    refs = refs[14:]
    if latent:
        cosa_ref, sina_ref, cosc_ref, sinc_ref = refs[:4]
        refs = refs[4:]
        aqt_ref, ak_ref, avt_ref, bqt_ref, bk_ref, bvt_ref, qct_ref, kc_ref, vct_ref, z_ref = refs
    else:
        (aqt_ref, ak_ref, av_ref, avt_ref, bqt_ref, bk_ref, bv_ref, bvt_ref, qct_ref, kc_ref, vct_ref,
         ckvn_ref, ckr_ref, z_ref) = refs

    mod = mod_ref[0]
    shift1, scale1 = mod[:, 0:d], mod[:, d:2 * d]
    tm = x_ref.shape[0]
    n_parts = PRE_ROW_PARTS if tm % (PRE_ROW_PARTS * LANES) == 0 else 1
    parts = [slice(k * tm // n_parts, (k + 1) * tm // n_parts) for k in range(n_parts)]

    proj = []
    for rs in parts:
        h = (_rms(x_ref[rs, :], g1_ref[...]) * (1.0 + scale1) + shift1).astype(_MM)
        proj.append([_dot(h, w_ref[...]) for w_ref in (wa_ref, wb_ref, wcq_ref, wckv_ref, wckr_ref, wdu_ref)])

    for rs, (pa, pb, cq, ckv, kr, du) in zip(parts, proj):
        aq, ak, av = pa[:, 0:HEAD_W], pa[:, HEAD_W:2 * HEAD_W], pa[:, 2 * HEAD_W:3 * HEAD_W]
        if latent:
            cosa, sina = cosa_ref[rs, :], sina_ref[rs, :]
            aq = _rope(aq, cosa, sina)
            ak = _rope(ak, cosa, sina)
        else:
            av_ref[rs, :] = av
        aqt_ref[:, rs] = (aq * (sa * LOG2E)).T.astype(aqt_ref.dtype)
        ak_ref[rs, :] = ak.astype(ak_ref.dtype)
        avt_ref[:, rs] = av.T.astype(avt_ref.dtype)

        bq, bk, bv = pb[:, 0:HEAD_W], pb[:, HEAD_W:2 * HEAD_W], pb[:, 2 * HEAD_W:3 * HEAD_W]
        bk_ref[rs, :] = bk.astype(bk_ref.dtype)
        if not latent:
            bv_ref[rs, :] = bv
        bqt_ref[:, rs] = (bq * (sb * LOG2E)).T.astype(bqt_ref.dtype)
        bvt_ref[:, rs] = bv.T.astype(bvt_ref.dtype)

        qc = _dot(_rms(cq, cqg_ref[...]), wuq_ref[...])
        ckvn = _rms(ckv, ckvg_ref[...])
        if latent:
            cosc, sinc = cosc_ref[rs, :], sinc_ref[rs, :]
            qc = _rope(qc, cosc, sinc)
            kr = _rope(kr, cosc, sinc)
        else:
            ckvn_ref[rs, :] = ckvn
            ckr_ref[rs, :] = kr[:, NOPE_C:NOPE_C + ROPE_C]
        qct_ref[:, rs] = (qc * (sc * LOG2E)).T.astype(qct_ref.dtype)
        kc_ref[rs, :] = (_dot(ckvn, wkc_ref[...]) + kr).astype(kc_ref.dtype)
        vct_ref[:, rs] = _dot(ckvn, wv_ref[...]).T.astype(vct_ref.dtype)
        z_ref[rs, :] = du[:, 0:CONV_CH] * _sigmoid(du[:, CONV_CH:2 * CONV_CH])


def _pre(x2d, mods3, mod_row, lw, *, latent, seq, tables):
    t, d = x2d.shape
    tm = min(512, seq)
    n_seq_tiles = seq // tm
    grid = (t // tm,)
    row = lambda i: (i, 0)
    const = lambda i: (0, 0)
    wnames = ("wa", "wb", "wcq", "wckv", "wckr", "wdu", "cqg", "ckvg", "wuq", "wkc", "wv")
    in_specs = [pl.BlockSpec((tm, d), row),
                pl.BlockSpec((1, 1, 6 * d), lambda i: (mod_row(i * tm), 0, 0)),
                pl.BlockSpec((1, d), const)]
    args = [x2d, mods3, lw["g1"]]
    for n in wnames:
        in_specs.append(pl.BlockSpec(lw[n].shape, const))
        args.append(lw[n])
    if latent:
        for tb in tables:
            in_specs.append(pl.BlockSpec((tm, HEAD_W), lambda i: (i % n_seq_tiles, 0)))
            args.append(tb)
    nat = lambda dt: (jax.ShapeDtypeStruct((t, HEAD_W), dt), pl.BlockSpec((tm, HEAD_W), row))
    tr = (jax.ShapeDtypeStruct((t // seq, HEAD_W, seq), _MM),
          pl.BlockSpec((None, HEAD_W, tm), lambda i: (i // n_seq_tiles, 0, i % n_seq_tiles)))
    if latent:
        outs = [tr, nat(_MM), tr, tr, nat(_MM), tr, tr, nat(_MM), tr]
    else:
        outs = [tr, nat(_F32), nat(_F32), tr, tr, nat(_F32), nat(_F32), tr, tr, nat(_MM), tr,
                (jax.ShapeDtypeStruct((t, KV_LORA), _F32), pl.BlockSpec((tm, KV_LORA), row)),
                (jax.ShapeDtypeStruct((t, ROPE_C), _F32), pl.BlockSpec((tm, ROPE_C), row))]
    outs.append((jax.ShapeDtypeStruct((t, CONV_CH), _F32), pl.BlockSpec((tm, CONV_CH), row)))
    out_shape = [o[0] for o in outs]
    out_specs = [o[1] for o in outs]
    kern = functools.partial(_pre_kernel, d=d, latent=latent, sa=DH_A ** -0.5, sb=DH_B ** -0.5,
                             sc=(NOPE_C + ROPE_C) ** -0.5)
    return pl.pallas_call(kern, grid=grid, in_specs=in_specs, out_specs=out_specs, out_shape=out_shape,
                          compiler_params=_cparams(("arbitrary",), 48),
                          name="pre_lat" if latent else "pre_ctx")(*args)


SCORE_LOOKAHEAD = 6
SUM_ROWS = 16
MAX_UNROLLED_KEY_TILES = 16


def _softmax_init(qt_ref, qz_sc, m_sc, acc_sc, n_chains, dqk):
    qz_sc[...] = jnp.zeros(qz_sc.shape, qz_sc.dtype)
    for c in range(n_chains):
        qz_sc[c, (c * dqk) % LANES:(c * dqk) % LANES + dqk, :] = qt_ref[c * dqk:(c + 1) * dqk, :]
    m_sc[...] = jnp.full(m_sc.shape, -jnp.inf, _F32)
    acc_sc[...] = jnp.zeros(acc_sc.shape, _F32)


def _softmax_steps(tiles, *, qz_sc, m_sc, acc_sc, n_chains, n_maps, dqk):
    steps = [(ti, c) for ti in range(len(tiles)) for c in range(n_chains)]
    kts, vts = {}, {}

    def scores(ti, c):
        if ti not in kts:
            kts[ti] = tiles[ti][0]().astype(_MM)
        grp = (c * dqk) // LANES
        s = jnp.dot(kts[ti][:, grp * LANES:(grp + 1) * LANES], qz_sc[c], preferred_element_type=_F32)
        return s if tiles[ti][2] is None else s + tiles[ti][2](c)

    def values(ti, h):
        if (ti, h) not in vts:
            vt = tiles[ti][1](h).astype(_MM)
            vts[ti, h] = jnp.concatenate([vt, jnp.ones((SUM_ROWS, vt.shape[1]), _MM)], axis=0)
        return vts[ti, h]

    pending = [scores(*st) for st in steps[:SCORE_LOOKAHEAD]]
    for idx, (ti, c) in enumerate(steps):
        s = pending.pop(0)
        if idx + SCORE_LOOKAHEAD < len(steps):
            pending.append(scores(*steps[idx + SCORE_LOOKAHEAD]))
        m_prev = m_sc[c]
        m_new = jnp.maximum(m_prev, jnp.max(s, axis=0, keepdims=True))
        alpha = jnp.exp2(m_prev - m_new)
        p = jnp.exp2(s - m_new).astype(_MM)
        acc_sc[c] = alpha * acc_sc[c] + jnp.dot(values(ti, c // n_maps), p, preferred_element_type=_F32)
        m_sc[c] = m_new


def _attn_kernel(*refs, n_heads, n_maps, dqk, dv, tk, n_own, unroll, tkc, n_cache, diff, lam_init):
    qt_ref, k_ref, vt_ref = refs[:3]
    refs = refs[3:]
    if n_cache:
        kc_ref, vct_ref = refs[:2]
        refs = refs[2:]
    if diff:
        alam_ref, subg_ref = refs[:2]
        refs = refs[2:]
    o_ref, qz_sc, m_sc, acc_sc, ot_sc = refs
    n_chains = n_heads * n_maps
    _softmax_init(qt_ref, qz_sc, m_sc, acc_sc, n_chains, dqk)
    update = functools.partial(_softmax_steps, qz_sc=qz_sc, m_sc=m_sc, acc_sc=acc_sc, n_chains=n_chains,
                               n_maps=n_maps, dqk=dqk)

    cache_tiles = [(lambda j=j: kc_ref[j * tkc:(j + 1) * tkc, :],
                    lambda h, j=j: vct_ref[h * dv:(h + 1) * dv, j * tkc:(j + 1) * tkc], None) for j in range(n_cache)]

    def own_tiles(starts):
        return [(lambda st=st: k_ref[pl.ds(st, tk), :],
                 lambda h, st=st: vt_ref[h * dv:(h + 1) * dv, pl.ds(st, tk)], None) for st in starts]

    if unroll == n_own:
        update(cache_tiles + own_tiles([u * tk for u in range(n_own)]))
    else:
        if n_cache:
            update(cache_tiles)

        def body(j, carry):
            update(own_tiles([pl.multiple_of((j * unroll + u) * tk, tk) for u in range(unroll)]))
            return carry

        lax.fori_loop(0, n_own // unroll, body, 0)

    def normalised(c):
        a = acc_sc[c]
        return a[0:dv] / a[dv:dv + 1]

    if diff:
        lv = alam_ref[...]
        lam = (jnp.exp(jnp.sum(lv[0:1] * lv[1:2], axis=-1, keepdims=True))
               - jnp.exp(jnp.sum(lv[2:3] * lv[3:4], axis=-1, keepdims=True)) + lam_init)
    for h in range(n_heads):
        if diff:
            o = normalised(2 * h) - lam * normalised(2 * h + 1)
            o = o * lax.rsqrt(jnp.mean(o * o, axis=0, keepdims=True) + EPS) * subg_ref[...] * (1.0 - lam_init)
        else:
            o = normalised(h)
        ot_sc[h * dv:(h + 1) * dv, :] = o
    o_ref[...] = ot_sc[...].T.astype(o_ref.dtype)


def _attn(qt, k, vt, cache, diff_args, *, n_heads, n_maps, dqk, dv, lam_init=0.0, name):
    nb, s, _ = k.shape
    tq = min(256, s)
    tk = min(256, s)
    n_own = s // tk
    qmap = lambda b, i: (b, 0, i)
    bmap = lambda b, i: (b, 0, 0)
    in_specs = [pl.BlockSpec((None, HEAD_W, tq), qmap),
                pl.BlockSpec((None, s, HEAD_W), bmap),
                pl.BlockSpec((None, HEAD_W, s), bmap)]
    args = [qt, k, vt]
    tkc, n_cache = tk, 0
    if cache is not None:
        past = cache[0].shape[1]
        tkc = min(256, past)
        n_cache = past // tkc
        in_specs += [pl.BlockSpec((None, past, HEAD_W), bmap), pl.BlockSpec((None, HEAD_W, past), bmap)]
        args += list(cache)
    if diff_args is not None:
        in_specs += [pl.BlockSpec(diff_args[0].shape, lambda b, i: (0, 0)),
                     pl.BlockSpec(diff_args[1].shape, lambda b, i: (0, 0))]
        args += list(diff_args)
    n_chains = n_heads * n_maps
    unroll = n_own if n_own <= MAX_UNROLLED_KEY_TILES else (8 if n_own % 8 == 0 else 1)
    kern = functools.partial(_attn_kernel, n_heads=n_heads, n_maps=n_maps, dqk=dqk, dv=dv, tk=tk, n_own=n_own,
                             unroll=unroll, tkc=tkc, n_cache=n_cache, diff=diff_args is not None, lam_init=lam_init)
    return pl.pallas_call(
        kern, grid=(nb, s // tq), in_specs=in_specs,
        out_specs=pl.BlockSpec((None, tq, HEAD_W), lambda b, i: (b, i, 0)),
        out_shape=jax.ShapeDtypeStruct((nb, s, HEAD_W), _MM),
        scratch_shapes=[pltpu.VMEM((n_chains, LANES, tq), _MM), pltpu.VMEM((n_chains, 1, tq), _F32),
                        pltpu.VMEM((n_chains, dv + SUM_ROWS, tq), _F32), pltpu.VMEM((n_heads * dv, tq), _F32)],
        compiler_params=_cparams(("arbitrary", "arbitrary"), 48),
        name=name,
    )(*args)


NBR_ROWS = 4
NBR_SPAN_ROWS = 12


def _nbr_kernel(qt_ref, k_ref, vt_ref, kc_ref, vct_ref, tb_ref, o_ref, qz_sc, m_sc, acc_sc, ot_sc, *, rows, tkc, n_cache):
    i = pl.program_id(1)
    ss = jnp.clip(i * NBR_ROWS - WIN_R // 2, 0, rows - NBR_SPAN_ROWS)
    tk = NBR_ROWS * GRID_W
    kstart = pl.multiple_of(ss * GRID_W, tk)
    hsl = lambda h: slice(h * DH_B, (h + 1) * DH_B)

    _softmax_init(qt_ref, qz_sc, m_sc, acc_sc, H_B, DH_B)
    tiles = [(lambda j=j: kc_ref[j * tkc:(j + 1) * tkc, :],
              lambda h, j=j: vct_ref[hsl(h), j * tkc:(j + 1) * tkc], None) for j in range(n_cache)]
    starts = [pl.multiple_of(kstart + u * tk, tk) for u in range(NBR_SPAN_ROWS // NBR_ROWS)]
    tiles += [(lambda st=st: k_ref[pl.ds(st, tk), :],
               lambda h, st=st: vt_ref[hsl(h), pl.ds(st, tk)],
               lambda c, u=u: tb_ref[c, u * tk:(u + 1) * tk, :]) for u, st in enumerate(starts)]
    _softmax_steps(tiles, qz_sc=qz_sc, m_sc=m_sc, acc_sc=acc_sc, n_chains=H_B, n_maps=1, dqk=DH_B)
    for h in range(H_B):
        a = acc_sc[h]
        ot_sc[hsl(h), :] = a[0:DH_B] / a[DH_B:DH_B + 1]
    o_ref[...] = ot_sc[...].T.astype(o_ref.dtype)


def _nbr_bias_table(rpb, rows):
    nh, n_r, n_c = rpb.shape
    c = np.arange(GRID_W)[:, None]
    kc = np.arange(GRID_W)[None, :]
    cidx = np.clip(kc - c + WIN_C - 1, 0, n_c - 1).reshape(1, -1)
    onehot = jnp.asarray((cidx == np.arange(n_c)[:, None]).astype(np.float32))
    toep = jnp.dot(rpb.reshape(nh * n_r, n_c).astype(_F32), onehot, precision=lax.Precision.HIGHEST)
    toep = toep.reshape(nh, n_r, GRID_W, GRID_W)
    cstart = np.clip(c - WIN_C // 2, 0, GRID_W - WIN_C)
    valid = (kc >= cstart) & (kc < cstart + WIN_C)
    toep = jnp.where(valid[None, None], toep, -jnp.inf) * LOG2E
    assert rows % NBR_ROWS == 0 and rows >= NBR_SPAN_ROWS
    r0 = np.array([0, NBR_ROWS, rows - NBR_ROWS])[:, None, None]
    ss = np.clip(r0 - WIN_R // 2, 0, rows - NBR_SPAN_ROWS)
    key_row = ss + np.arange(NBR_SPAN_ROWS)[None, :, None]
    r = r0 + np.arange(NBR_ROWS)[None, None, :]
    rs = np.clip(r - WIN_R // 2, 0, rows - WIN_R)
    in_rows = (key_row >= rs) & (key_row < rs + WIN_R)
    ridx = np.clip(key_row - r + WIN_R - 1, 0, n_r - 1)
    tb = jnp.take(toep, jnp.asarray(ridx.reshape(-1)), axis=1)
    tb = tb.reshape((nh,) + ridx.shape + (GRID_W, GRID_W))
    tb = jnp.where(in_rows[None, :, :, :, None, None], tb, -jnp.inf)
    tb = tb.transpose(1, 0, 2, 5, 3, 4)
    return tb.reshape(3, nh, NBR_SPAN_ROWS * GRID_W, NBR_ROWS * GRID_W)


def _nbr(qt, k, vt, kc, vct, tb):
    nb, s, _ = k.shape
    rows = s // GRID_W
    n_steps = rows // NBR_ROWS
    past = kc.shape[1]
    tq = NBR_ROWS * GRID_W
    bmap = lambda b, i: (b, 0, 0)
    variant = lambda b, i: (jnp.where(i == 0, 0, jnp.where(i == n_steps - 1, 2, 1)), 0, 0, 0)
    tkc = min(256, past)
    return pl.pallas_call(
        functools.partial(_nbr_kernel, rows=rows, tkc=tkc, n_cache=past // tkc), grid=(nb, n_steps),
        in_specs=[pl.BlockSpec((None, HEAD_W, tq), lambda b, i: (b, 0, i)),
                  pl.BlockSpec((None, s, HEAD_W), bmap),
                  pl.BlockSpec((None, HEAD_W, s), bmap),
                  pl.BlockSpec((None, past, HEAD_W), bmap),
                  pl.BlockSpec((None, HEAD_W, past), bmap),
                  pl.BlockSpec((None,) + tb.shape[1:], variant)],
        out_specs=pl.BlockSpec((None, tq, HEAD_W), lambda b, i: (b, i, 0)),
        out_shape=jax.ShapeDtypeStruct((nb, s, HEAD_W), _MM),
        scratch_shapes=[pltpu.VMEM((H_B, LANES, tq), _MM), pltpu.VMEM((H_B, 1, tq), _F32),
                        pltpu.VMEM((H_B, DH_B + SUM_ROWS, tq), _F32), pltpu.VMEM((HEAD_W, tq), _F32)],
        compiler_params=_cparams(("arbitrary", "arbitrary"), 48),
        name="nbr_attn",
    )(qt, k, vt, kc, vct, tb)


def _mla_cache_kernel(ckv_ref, kr_ref, wkc_ref, wvt_ref, e_ref, kc_ref, vct_ref):
    ckv = ckv_ref[...]
    kc_ref[...] = (_dot(ckv, wkc_ref[...]) + _dot(kr_ref[...], e_ref[...])).astype(kc_ref.dtype)
    vct_ref[...] = _dot_nt(wvt_ref[...], ckv).astype(vct_ref.dtype)


def _mla_cache(ckv, kr, wkc, wvt, expand):
    nb, past, _ = ckv.shape
    bmap = lambda b: (b, 0, 0)
    const = lambda b: (0, 0)
    return pl.pallas_call(
        _mla_cache_kernel, grid=(nb,),
        in_specs=[pl.BlockSpec((None, past, KV_LORA), bmap), pl.BlockSpec((None, past, ROPE_C), bmap),
                  pl.BlockSpec(wkc.shape, const), pl.BlockSpec(wvt.shape, const), pl.BlockSpec(expand.shape, const)],
        out_specs=[pl.BlockSpec((None, past, HEAD_W), bmap), pl.BlockSpec((None, HEAD_W, past), bmap)],
        out_shape=[jax.ShapeDtypeStruct((nb, past, HEAD_W), _MM), jax.ShapeDtypeStruct((nb, HEAD_W, past), _MM)],
        compiler_params=_cparams(("arbitrary",), 32),
        name="mla_cache",
    )(ckv, kr, wkc, wvt, expand)


CONV_HALO = 16


def _conv_kernel(z_ref, w_ref, b_ref, g_ref, beta_ref, o_ref, zp_ref, *, seq, rc):
    zp_ref[0:CONV_HALO, :] = jnp.zeros((CONV_HALO, CONV_CH), _F32)
    zp_ref[CONV_HALO:CONV_HALO + seq, :] = z_ref[...]
    zp_ref[CONV_HALO + seq:2 * CONV_HALO + seq, :] = jnp.zeros((CONV_HALO, CONV_CH), _F32)
    w = w_ref[...]
    shift = CONV_HALO - CONV_K // 2

    def chunk(c, carry):
        base = pl.multiple_of(c * rc, rc)
        n_win = rc + 2 * CONV_HALO
        win = zp_ref[pl.ds(base, n_win), :]
        acc = jnp.zeros((rc, CONV_CH), _F32)
        for phase in range(SUBLANES):
            rolled = win if phase == 0 else pltpu.roll(win, n_win - phase, axis=0)
            for j in range(CONV_K):
                if (j + shift) % SUBLANES == phase:
                    a = (j + shift) - phase
                    acc = acc + rolled[a:a + rc, :] * w[j:j + 1, :]
        y = acc + b_ref[...]
        mu = jnp.mean(y, axis=-1, keepdims=True)
        yc = y - mu
        var = jnp.mean(yc * yc, axis=-1, keepdims=True)
        yn = yc * lax.rsqrt(var + EPS) * g_ref[...] + beta_ref[...]
        o_ref[pl.ds(base, rc), :] = (yn * _sigmoid(yn)).astype(o_ref.dtype)
        return carry

    lax.fori_loop(0, seq // rc, chunk, 0)


def _conv(z, w, b, g, beta):
    nb, seq, _ = z.shape
    rc = min(128, seq)
    bmap = lambda i: (i, 0, 0)
    const = lambda i: (0, 0)
    kern = functools.partial(_conv_kernel, seq=seq, rc=rc)
    return pl.pallas_call(
        kern, grid=(nb,),
        in_specs=[pl.BlockSpec((None, seq, CONV_CH), bmap), pl.BlockSpec(w.shape, const),
                  pl.BlockSpec(b.shape, const), pl.BlockSpec(g.shape, const), pl.BlockSpec(beta.shape, const)],
        out_specs=pl.BlockSpec((None, seq, CONV_CH), bmap),
        out_shape=jax.ShapeDtypeStruct((nb, seq, CONV_CH), _MM),
        scratch_shapes=[pltpu.VMEM((seq + 2 * CONV_HALO, CONV_CH), _F32)],
        compiler_params=_cparams(("arbitrary",), 40),
        name="conv",
    )(z, w, b, g, beta)


POST_ROW_PARTS = 2

def _post_kernel(x_ref, mod_ref, oa_ref, ob_ref, oc_ref, od_ref, g1_ref, wg_ref, bg_ref,
                 awo_ref, bwo_ref, cwo_ref, dwo_ref, wout_ref, g2_ref, rw_ref, rb_ref,
                 x1_ref, h2_ref, dw_ref, *, d):
    mod = mod_ref[0]
    shift1, scale1, gate1 = mod[:, 0:d], mod[:, d:2 * d], mod[:, 2 * d:3 * d]
    shift2, scale2 = mod[:, 3 * d:4 * d], mod[:, 4 * d:5 * d]
    branches = ((oa_ref, awo_ref), (ob_ref, bwo_ref), (oc_ref, cwo_ref), (od_ref, dwo_ref))
    tm = x_ref.shape[0]
    parts = [slice(k * tm // POST_ROW_PARTS, (k + 1) * tm // POST_ROW_PARTS) for k in range(POST_ROW_PARTS)]

    xs, branch_mm = [], []
    for rs in parts:
        x = x_ref[rs, :]
        h = (_rms(x, g1_ref[...]) * (1.0 + scale1) + shift1).astype(_MM)
        xs.append(x)
        branch_mm.append([(_dot(h, wg_ref[:, bi * d:(bi + 1) * d]), _dot(o_ref[rs, :], wo_ref[...]))
                          for bi, (o_ref, wo_ref) in enumerate(branches)])
    mixed = []
    for pairs in branch_mm:
        merged = None
        for bi, (gate_logits, out) in enumerate(pairs):
            term = _sigmoid(gate_logits + bg_ref[:, bi * d:(bi + 1) * d]) * out
            merged = term if merged is None else merged + term
        mixed.append(_dot(merged, wout_ref[...]))
    logits = []
    for rs, x, y in zip(parts, xs, mixed):
        x1 = x + gate1 * y
        x1_ref[rs, :] = x1
        h2_mm = (_rms(x1, g2_ref[...]) * (1.0 + scale2) + shift2).astype(_MM)
        h2_ref[rs, :] = h2_mm
        logits.append(_dot(h2_mm, rw_ref[...]) + rb_ref[...])
    for rs, r in zip(parts, logits):
        lgp, lep = r[:, 0:LANES], r[:, LANES:2 * LANES]
        lane = lax.broadcasted_iota(jnp.int32, lgp.shape, 1)
        neg = -jnp.inf
        lg = jnp.where(lane < N_GROUPS, lgp, neg)
        mg = jnp.max(lg, axis=-1, keepdims=True)
        gi = jnp.min(jnp.where(lg == mg, lane, LANES), axis=-1, keepdims=True)
        gw = 1.0 / jnp.sum(jnp.exp(lg - mg), axis=-1, keepdims=True)
        in_group = (lane < N_EXPERTS) & ((lane // EXPERTS_PER_GROUP) == gi)
        le = jnp.where(in_group, lep, neg)
        t1 = jnp.max(le, axis=-1, keepdims=True)
        i1 = jnp.min(jnp.where(le == t1, lane, LANES), axis=-1, keepdims=True)
        le2 = jnp.where(lane == i1, neg, le)
        t2 = jnp.max(le2, axis=-1, keepdims=True)
        i2 = jnp.min(jnp.where(le2 == t2, lane, LANES), axis=-1, keepdims=True)
        e2 = jnp.exp(t2 - t1)
        w1 = gw / (1.0 + e2)
        w2 = gw * e2 / (1.0 + e2)
        dw_ref[rs, :] = jnp.where(lane == i1, w1, 0.0) + jnp.where(lane == i2, w2, 0.0)


def _post(x2d, mods3, mod_row, oa, ob, oc, od, lw):
    t, d = x2d.shape
    tm = min(512, t)
    row = lambda i: (i, 0)
    const = lambda i: (0, 0)
    wnames = ("g1", "wgate", "bgate", "awo", "bwo", "cwo", "dwo", "wout", "g2", "rw", "rb")
    in_specs = [pl.BlockSpec((tm, d), row), pl.BlockSpec((1, 1, 6 * d), lambda i: (mod_row(i * tm), 0, 0))]
    in_specs += [pl.BlockSpec((tm, HEAD_W), row)] * 4
    in_specs += [pl.BlockSpec(lw[n].shape, const) for n in wnames]
    args = [x2d, mods3, oa, ob, oc, od] + [lw[n] for n in wnames]
    return pl.pallas_call(
        functools.partial(_post_kernel, d=d), grid=(t // tm,), in_specs=in_specs,
        out_specs=[pl.BlockSpec((tm, d), row), pl.BlockSpec((tm, d), row), pl.BlockSpec((tm, LANES), row)],
        out_shape=[jax.ShapeDtypeStruct((t, d), _F32), jax.ShapeDtypeStruct((t, d), _MM),
                   jax.ShapeDtypeStruct((t, LANES), _F32)],
        compiler_params=_cparams(("arbitrary",), 56),
        name="post",
    )(*args)


def _moe_kernel(h2_ref, dw_ref, x1_ref, mod_ref, w1_ref, w2_ref, fg_ref, o_ref, acc_ref, *, d, final):
    g = pl.program_id(1)

    @pl.when(g == 0)
    def _():
        acc_ref[...] = jnp.zeros_like(acc_ref)

    h2 = h2_ref[...]
    dw = dw_ref[...]
    lane = lax.broadcasted_iota(jnp.int32, dw.shape, 1)
    acts = []
    for j in range(EXPERTS_PER_GROUP):
        hid = _dot(h2, w1_ref[j])
        gate, up = hid[:, 0:D_EXPERT], hid[:, D_EXPERT:2 * D_EXPERT]
        we = jnp.sum(jnp.where(lane == g * EXPERTS_PER_GROUP + j, dw, 0.0), axis=-1, keepdims=True)
        acts.append((gate * _sigmoid(gate) * up * we).astype(_MM))
    acc_ref[...] += _dot(jnp.concatenate(acts, axis=-1), w2_ref[...])

    @pl.when(g == pl.num_programs(1) - 1)
    def _():
        gate2 = mod_ref[0][:, 5 * d:6 * d]
        x2 = x1_ref[...] + gate2 * acc_ref[...]
        if final:
            x2 = _rms(x2, fg_ref[...])
        o_ref[...] = x2


def _moe(h2, dw, x1, mods3, mod_row, w1, w2, fg, *, final):
    t, d = x1.shape
    tm = min(1024, t)
    n_e, _, d_hid = w1.shape
    w2g = w2.reshape(N_GROUPS, EXPERTS_PER_GROUP * D_EXPERT, d)
    row = lambda i, g: (i, 0)
    return pl.pallas_call(
        functools.partial(_moe_kernel, d=d, final=final), grid=(t // tm, N_GROUPS),
        in_specs=[pl.BlockSpec((tm, d), row), pl.BlockSpec((tm, LANES), row), pl.BlockSpec((tm, d), row),
                  pl.BlockSpec((1, 1, 6 * d), lambda i, g: (mod_row(i * tm), 0, 0)),
                  pl.BlockSpec((EXPERTS_PER_GROUP, d, d_hid), lambda i, g: (g, 0, 0)),
                  pl.BlockSpec((None,) + w2g.shape[1:], lambda i, g: (g, 0, 0)),
                  pl.BlockSpec((1, d), lambda i, g: (0, 0))],
        out_specs=pl.BlockSpec((tm, d), row),
        out_shape=jax.ShapeDtypeStruct((t, d), _F32),
        scratch_shapes=[pltpu.VMEM((tm, d), _F32)],
        compiler_params=_cparams(("arbitrary", "arbitrary"), 56),
        name="moe_final" if final else "moe",
    )(h2, dw, x1, mods3, w1, w2g, fg)


def _rope_tables(n_tok, dim):
    hq = dim // 4
    tpos = np.arange(n_tok)
    rows = (tpos // GRID_W).astype(np.float32)
    cols = (tpos % GRID_W).astype(np.float32)
    inv = jnp.asarray(ROPE_THETA, _F32) ** (-jnp.arange(hq, dtype=_F32) / hq)
    ar = jnp.asarray(rows)[:, None] * inv[None, :]
    ac = jnp.asarray(cols)[:, None] * inv[None, :]
    ang = jnp.concatenate([ar, ar, ac, ac], axis=-1)
    return jnp.cos(ang), jnp.sin(ang)


def _layer_weights(l, p, grid_rows):
    mm = lambda a: a.astype(_MM)
    w_in = p["w_in"][l]
    pts = np.cumsum([0, 2 * H_A * DH_A, 2 * H_A * DH_A, 2 * H_A * DH_A, H_B * DH_B, H_B * DH_B, H_B * DH_B,
                     Q_LORA, KV_LORA, ROPE_C, 2 * CONV_CH])
    seg = lambda i, j: w_in[:, pts[i]:pts[j]]
    d = w_in.shape[0]
    wckr = seg(8, 9)
    zeros_n = jnp.zeros((d, NOPE_C), w_in.dtype)
    wckr_t = jnp.concatenate([zeros_n, wckr] * H_C, axis=1)
    wukv = p["c_wukv"][l].reshape(KV_LORA, H_C, NOPE_C + DV_C)
    wkc = jnp.concatenate([wukv[:, :, :NOPE_C], jnp.zeros((KV_LORA, H_C, ROPE_C), wukv.dtype)], axis=-1)
    wkc = wkc.reshape(KV_LORA, H_C * (NOPE_C + ROPE_C))
    wv = wukv[:, :, NOPE_C:].reshape(KV_LORA, H_C * DV_C)
    router = jnp.zeros((d, 2 * LANES), _F32)
    router = router.at[:, 0:N_GROUPS].set(p["moe_w_group"][l]).at[:, LANES:LANES + N_EXPERTS].set(p["moe_w_expert"][l])
    rb = jnp.zeros((1, 2 * LANES), _F32)
    rb = rb.at[0, 0:N_GROUPS].set(p["moe_b_group"][l]).at[0, LANES:LANES + N_EXPERTS].set(p["moe_b_expert"][l])
    conv_w = p["d_conv_w"][l]
    return dict(
        g1=p["norm1_g"][l][None], g2=p["norm2_g"][l][None],
        wa=mm(seg(0, 3)), wb=mm(seg(3, 6)), wcq=mm(seg(6, 7)), wckv=mm(seg(7, 8)), wckr=mm(wckr_t), wdu=mm(seg(9, 10)),
        cqg=p["c_q_norm_g"][l][None], ckvg=p["c_kv_norm_g"][l][None],
        wuq=mm(p["c_wuq"][l]), wkc=mm(wkc), wv=mm(wv), wvt=mm(wv.T),
        alam=p["a_lambda"][l], subg=p["a_subln_g"][l][:, None],
        wgate=mm(p["w_gate"][l]), bgate=p["b_gate"][l][None],
        awo=mm(p["a_wo"][l]), bwo=mm(p["b_wo"][l]), cwo=mm(p["c_wo"][l]), dwo=mm(p["d_wo"][l]),
        wout=mm(p["w_out"][l]), rw=mm(router), rb=rb,
        conv_w=conv_w, conv_b=p["d_conv_b"][l][None], ln_g=p["d_ln_g"][l][None], ln_b=p["d_ln_b"][l][None],
        w1=mm(p["moe_w1"][l]), w2=mm(p["moe_w2"][l]),
        tb=_nbr_bias_table(p["b_rpb"][l], grid_rows),
    )


def _layer(x2d, mods3, mod_row, lw, l, n_layers, *, nb, seq, fg, ctx, tables):
    latent = ctx is not None
    t = x2d.shape[0]
    outs = _pre(x2d, mods3, mod_row, lw, latent=latent, seq=seq, tables=tables)
    r3 = lambda a: a.reshape(nb, seq, a.shape[-1])
    lam_init = 0.8 - 0.6 * math.exp(-0.3 * l)
    diff_args = (lw["alam"], lw["subg"])
    if latent:
        aqt, ak, avt, bqt, bk, bvt, qct, kc, vct, z = outs
        own = None
        a_k_c, a_vt_c, b_k_c, b_vt_c, c_k_c, c_vt_c = ctx
        oa = _attn(aqt, r3(ak), avt, (a_k_c, a_vt_c), diff_args, n_heads=H_A, n_maps=2,
                   dqk=DH_A, dv=2 * DH_A, lam_init=lam_init, name="attn_a_lat")
        ob = _nbr(bqt, r3(bk), bvt, b_k_c, b_vt_c, lw["tb"])
        oc = _attn(qct, r3(kc), vct, (c_k_c, c_vt_c), None, n_heads=H_C, n_maps=1,
                   dqk=NOPE_C + ROPE_C, dv=DV_C, name="attn_c_lat")
    else:
        aqt, ak, av, avt, bqt, bk, bv, bvt, qct, kc, vct, ckvn, ckr, z = outs
        own = (ak, av, bk, bv, ckvn, ckr)
        oa = _attn(aqt, r3(ak), avt, None, diff_args, n_heads=H_A, n_maps=2,
                   dqk=DH_A, dv=2 * DH_A, lam_init=lam_init, name="attn_a_ctx")
        ob = _attn(bqt, r3(bk), bvt, None, None, n_heads=H_B, n_maps=1, dqk=DH_B, dv=DH_B, name="attn_b_ctx")
        oc = _attn(qct, r3(kc), vct, None, None, n_heads=H_C, n_maps=1,
                   dqk=NOPE_C + ROPE_C, dv=DV_C, name="attn_c_ctx")
    od = _conv(r3(z), lw["conv_w"], lw["conv_b"], lw["ln_g"], lw["ln_b"])
    f2 = lambda a: a.reshape(t, a.shape[-1])
    x1, h2, dw = _post(x2d, mods3, mod_row, f2(oa), f2(ob), f2(oc), f2(od), lw)
    x2 = _moe(h2, dw, x1, mods3, mod_row, lw["w1"], lw["w2"], fg, final=(l == n_layers - 1))
    return x2, own


def kernel(x_prompt, x_sample, cache_a_k, cache_a_v, cache_b_k, cache_b_v, cache_c_kv, cache_c_krope, c, c_ctx, norm1_g, norm2_g, w_ada, b_ada, w_in, a_lambda, a_subln_g, a_wo, b_rpb, b_wo, c_q_norm_g, c_kv_norm_g, c_wuq, c_wukv, c_wo, d_conv_w, d_conv_b, d_ln_g, d_ln_b, d_wo, w_gate, b_gate, w_out, moe_w_group, moe_b_group, moe_w_expert, moe_b_expert, moe_w1, moe_w2, final_norm_g):
    p = dict(norm1_g=norm1_g, norm2_g=norm2_g, w_in=w_in, a_lambda=a_lambda, a_subln_g=a_subln_g, a_wo=a_wo,
             b_rpb=b_rpb, b_wo=b_wo, c_q_norm_g=c_q_norm_g, c_kv_norm_g=c_kv_norm_g, c_wuq=c_wuq, c_wukv=c_wukv,
             c_wo=c_wo, d_conv_w=d_conv_w, d_conv_b=d_conv_b, d_ln_g=d_ln_g, d_ln_b=d_ln_b, d_wo=d_wo,
             w_gate=w_gate, b_gate=b_gate, w_out=w_out, moe_w_group=moe_w_group, moe_b_group=moe_b_group,
             moe_w_expert=moe_w_expert, moe_b_expert=moe_b_expert, moe_w1=moe_w1, moe_w2=moe_w2)
    n_layers = w_in.shape[0]
    bc, sc, d = x_prompt.shape
    bl, sl, _ = x_sample.shape
    past = cache_a_k.shape[2]
    assert bl + 1 <= 8 and sl % GRID_W == 0

    cond8 = jnp.zeros((8, d), _F32).at[0].set(c_ctx).at[1:1 + bl].set(c)
    mods3 = _adaln(cond8, w_ada, b_ada).reshape(n_layers * 8, 1, 6 * d)
    fg = final_norm_g[None]

    cos_a, sin_a = _rope_tables(sl, DH_A)
    cos_c, sin_c = _rope_tables(sl, ROPE_C)
    reps_a = HEAD_W // DH_A
    ones_n, zeros_n = jnp.ones((sl, NOPE_C), _F32), jnp.zeros((sl, NOPE_C), _F32)
    tables = (jnp.tile(cos_a, (1, reps_a)), jnp.tile(sin_a, (1, reps_a)),
              jnp.concatenate([ones_n, cos_c] * H_C, axis=1), jnp.concatenate([zeros_n, sin_c] * H_C, axis=1))
    expand = np.zeros((ROPE_C, HEAD_W), np.float32)
    for hh in range(H_C):
        expand[np.arange(ROPE_C), hh * (NOPE_C + ROPE_C) + NOPE_C + np.arange(ROPE_C)] = 1.0
    expand = jnp.asarray(expand, _MM)

    lws = [_layer_weights(l, p, sl // GRID_W) for l in range(n_layers)]

    xp = x_prompt.reshape(bc * sc, d)
    states = []
    for l in range(n_layers):
        xp, own = _layer(xp, mods3, (lambda tok, l=l: l * 8), lws[l], l, n_layers, nb=bc, seq=sc, fg=fg, ctx=None,
                         tables=None)
        states.append(own)
    y_prompt = xp.reshape(bc, sc, d)

    xs = x_sample.reshape(bl * sl, d)
    for l in range(n_layers):
        flat = lambda a: a[:, l].reshape(bl, past, -1)
        c_k_c, c_vt_c = _mla_cache(flat(cache_c_kv), flat(cache_c_krope), lws[l]["wkc"], lws[l]["wvt"], expand)
        ctx = (flat(cache_a_k).astype(_MM), jnp.swapaxes(flat(cache_a_v), 1, 2).astype(_MM),
               flat(cache_b_k).astype(_MM), jnp.swapaxes(flat(cache_b_v), 1, 2).astype(_MM), c_k_c, c_vt_c)
        xs, _ = _layer(xs, mods3, (lambda tok, l=l: l * 8 + 1 + tok // sl), lws[l], l, n_layers,
                       nb=bl, seq=sl, fg=fg, ctx=ctx, tables=tables)
    y_sample = xs.reshape(bl, sl, d)

    st = lambda k, shape: jnp.stack([s[k].reshape(shape) for s in states], axis=1)
    new_a_k = st(0, (bc, sc, H_A, 2 * DH_A))
    new_a_v = st(1, (bc, sc, H_A, 2 * DH_A))
    new_b_k = st(2, (bc, sc, H_B, DH_B))
    new_b_v = st(3, (bc, sc, H_B, DH_B))
    new_c_kv = st(4, (bc, sc, KV_LORA))
    new_c_krope = st(5, (bc, sc, ROPE_C))
    return (y_prompt, y_sample, new_a_k, new_a_v, new_b_k, new_b_v, new_c_kv, new_c_krope)
```

```python
import functools
import math

import numpy as np
import jax
import jax.numpy as jnp
from jax import lax
from jax.experimental import pallas as pl
from jax.experimental.pallas import tpu as pltpu

GRID_W = 64
EPS = 1e-6
ROPE_THETA = 10000.0
H_A, DH_A = 4, 32
H_B, DH_B = 4, 64
WIN_R, WIN_C = 8, 16
H_C, Q_LORA, KV_LORA, NOPE_C, ROPE_C, DV_C = 4, 256, 128, 32, 32, 64
CONV_CH, CONV_K = 256, 31
N_BRANCH = 4
N_GROUPS, EXPERTS_PER_GROUP, N_EXPERTS, D_EXPERT = 4, 4, 16, 256
HEAD_W = 256
LANES = 128
SUBLANES = 8
V7X_VMEM_BYTES = 64 * 1024 * 1024
LOG2E = math.log2(math.e)

PRE_TILE = 512
POST_TILE = 512
MOE_TILE = 1024
ATTN_TILE = 256
CONV_CHUNK = 128
ADALN_COLS = 1536
COND_ROWS = SUBLANES

_MM = jnp.bfloat16
_F32 = jnp.float32


def _cparams(sem, vmem_mb):
    return pltpu.CompilerParams(dimension_semantics=sem,
                                vmem_limit_bytes=min(vmem_mb * 1024 * 1024, V7X_VMEM_BYTES - 8 * 1024 * 1024))


def _dot(a, b):
    return jnp.dot(a.astype(_MM), b.astype(_MM), preferred_element_type=_F32)


def _dot_nt(a, b):
    return lax.dot_general(a.astype(_MM), b.astype(_MM), (((1,), (1,)), ((), ())),
                           preferred_element_type=_F32)


def _sigmoid(x):
    return 1.0 / (1.0 + jnp.exp(-x))


def _rms(x, g):
    return x * lax.rsqrt(jnp.mean(x * x, axis=-1, keepdims=True) + EPS) * g


def _rope(x, cos, sin):
    w = x.shape[-1]
    lane = lax.broadcasted_iota(jnp.int32, x.shape, 1)
    first = (lane % 16) < 8
    fwd = pltpu.roll(x, w - 8, axis=1)
    bwd = pltpu.roll(x, 8, axis=1)
    return x * cos + jnp.where(first, -fwd, bwd) * sin


def _adaln_kernel(cond_ref, w_ref, b_ref, o_ref):
    c = cond_ref[...]
    a = c * _sigmoid(c)
    o_ref[0] = _dot(a, w_ref[0]) + b_ref[0]


def _adaln(cond8, w_ada, b_ada):
    n_l, d, d6 = w_ada.shape
    tn = ADALN_COLS
    return pl.pallas_call(
        _adaln_kernel,
        grid=(n_l, d6 // tn),
        in_specs=[pl.BlockSpec((COND_ROWS, d), lambda l, j: (0, 0)),
                  pl.BlockSpec((1, d, tn), lambda l, j: (l, 0, j)),
                  pl.BlockSpec((1, 1, tn), lambda l, j: (l, 0, j))],
        out_specs=pl.BlockSpec((1, COND_ROWS, tn), lambda l, j: (l, 0, j)),
        out_shape=jax.ShapeDtypeStruct((n_l, COND_ROWS, d6), _F32),
        compiler_params=_cparams(("arbitrary", "arbitrary"), 40),
        name="adaln",
    )(cond8, w_ada, b_ada.reshape(n_l, 1, d6))


PRE_ROW_PARTS = 2


def _pre_kernel(*refs, d, latent, sa, sb, sc):
    (x_ref, mod_ref, g1_ref, wa_ref, wb_ref, wcq_ref, wckv_ref, wckr_ref, wdu_ref,
     cqg_ref, ckvg_ref, wuq_ref, wkc_ref, wv_ref) = refs[:14]
    refs = refs[14:]
    if latent:
        cosa_ref, sina_ref, cosc_ref, sinc_ref = refs[:4]
        refs = refs[4:]
        aqt_ref, ak_ref, avt_ref, bqt_ref, bk_ref, bvt_ref, qct_ref, kc_ref, vct_ref, z_ref = refs
    else:
        (aqt_ref, ak_ref, av_ref, avt_ref, bqt_ref, bk_ref, bv_ref, bvt_ref, qct_ref, kc_ref, vct_ref,
         ckvn_ref, ckr_ref, z_ref) = refs

    mod = mod_ref[0]
    shift1, scale1 = mod[:, 0:d], mod[:, d:2 * d]
    tm = x_ref.shape[0]
    n_parts = PRE_ROW_PARTS if tm % (PRE_ROW_PARTS * LANES) == 0 else 1
    parts = [slice(k * tm // n_parts, (k + 1) * tm // n_parts) for k in range(n_parts)]

    proj = []
    for rs in parts:
        h = (_rms(x_ref[rs, :], g1_ref[...]) * (1.0 + scale1) + shift1).astype(_MM)
        proj.append([_dot(h, w_ref[...]) for w_ref in (wa_ref, wb_ref, wcq_ref, wckv_ref, wckr_ref, wdu_ref)])

    for rs, (pa, pb, cq, ckv, kr, du) in zip(parts, proj):
        aq, ak, av = pa[:, 0:HEAD_W], pa[:, HEAD_W:2 * HEAD_W], pa[:, 2 * HEAD_W:3 * HEAD_W]
        if latent:
            cosa, sina = cosa_ref[rs, :], sina_ref[rs, :]
            aq = _rope(aq, cosa, sina)
            ak = _rope(ak, cosa, sina)
        else:
            av_ref[rs, :] = av
        aqt_ref[:, rs] = (aq * (sa * LOG2E)).T.astype(aqt_ref.dtype)
        ak_ref[rs, :] = ak.astype(ak_ref.dtype)
        avt_ref[:, rs] = av.T.astype(avt_ref.dtype)

        bq, bk, bv = pb[:, 0:HEAD_W], pb[:, HEAD_W:2 * HEAD_W], pb[:, 2 * HEAD_W:3 * HEAD_W]
        bk_ref[rs, :] = bk.astype(bk_ref.dtype)
        if not latent:
            bv_ref[rs, :] = bv
        bqt_ref[:, rs] = (bq * (sb * LOG2E)).T.astype(bqt_ref.dtype)
        bvt_ref[:, rs] = bv.T.astype(bvt_ref.dtype)

        qc = _dot(_rms(cq, cqg_ref[...]), wuq_ref[...])
        ckvn = _rms(ckv, ckvg_ref[...])
        if latent:
            cosc, sinc = cosc_ref[rs, :], sinc_ref[rs, :]
            qc = _rope(qc, cosc, sinc)
            kr = _rope(kr, cosc, sinc)
        else:
            ckvn_ref[rs, :] = ckvn
            ckr_ref[rs, :] = kr[:, NOPE_C:NOPE_C + ROPE_C]
        qct_ref[:, rs] = (qc * (sc * LOG2E)).T.astype(qct_ref.dtype)
        kc_ref[rs, :] = (_dot(ckvn, wkc_ref[...]) + kr).astype(kc_ref.dtype)
        vct_ref[:, rs] = _dot(ckvn, wv_ref[...]).T.astype(vct_ref.dtype)
        z_ref[rs, :] = du[:, 0:CONV_CH] * _sigmoid(du[:, CONV_CH:2 * CONV_CH])


def _pre(x2d, mods3, mod_row, lw, *, latent, seq, tables):
    t, d = x2d.shape
    tm = min(PRE_TILE, seq)
    n_seq_tiles = seq // tm
    grid = (t // tm,)
    row = lambda i: (i, 0)
    const = lambda i: (0, 0)
    wnames = ("wa", "wb", "wcq", "wckv", "wckr", "wdu", "cqg", "ckvg", "wuq", "wkc", "wv")
    in_specs = [pl.BlockSpec((tm, d), row),
                pl.BlockSpec((1, 1, 6 * d), lambda i: (mod_row(i * tm), 0, 0)),
                pl.BlockSpec((1, d), const)]
    args = [x2d, mods3, lw["g1"]]
    for n in wnames:
        in_specs.append(pl.BlockSpec(lw[n].shape, const))
        args.append(lw[n])
    if latent:
        for tb in tables:
            in_specs.append(pl.BlockSpec((tm, HEAD_W), lambda i: (i % n_seq_tiles, 0)))
            args.append(tb)
    nat = lambda dt: (jax.ShapeDtypeStruct((t, HEAD_W), dt), pl.BlockSpec((tm, HEAD_W), row))
    tr = (jax.ShapeDtypeStruct((t // seq, HEAD_W, seq), _MM),
          pl.BlockSpec((None, HEAD_W, tm), lambda i: (i // n_seq_tiles, 0, i % n_seq_tiles)))
    if latent:
        outs = [tr, nat(_MM), tr, tr, nat(_MM), tr, tr, nat(_MM), tr]
    else:
        outs = [tr, nat(_F32), nat(_F32), tr, tr, nat(_F32), nat(_F32), tr, tr, nat(_MM), tr,
                (jax.ShapeDtypeStruct((t, KV_LORA), _F32), pl.BlockSpec((tm, KV_LORA), row)),
                (jax.ShapeDtypeStruct((t, ROPE_C), _F32), pl.BlockSpec((tm, ROPE_C), row))]
    outs.append((jax.ShapeDtypeStruct((t, CONV_CH), _F32), pl.BlockSpec((tm, CONV_CH), row)))
    out_shape = [o[0] for o in outs]
    out_specs = [o[1] for o in outs]
    kern = functools.partial(_pre_kernel, d=d, latent=latent, sa=DH_A ** -0.5, sb=DH_B ** -0.5,
                             sc=(NOPE_C + ROPE_C) ** -0.5)
    return pl.pallas_call(kern, grid=grid, in_specs=in_specs, out_specs=out_specs, out_shape=out_shape,
                          compiler_params=_cparams(("arbitrary",), 48),
                          name="pre_lat" if latent else "pre_ctx")(*args)


SCORE_LOOKAHEAD = 5
SUM_ROWS = 16
MAX_UNROLLED_KEY_TILES = 16


def _softmax_init(qt_ref, qz_sc, m_sc, acc_sc, n_chains, dqk):
    qz_sc[...] = jnp.zeros(qz_sc.shape, qz_sc.dtype)
    for c in range(n_chains):
        qz_sc[c, (c * dqk) % LANES:(c * dqk) % LANES + dqk, :] = qt_ref[c * dqk:(c + 1) * dqk, :]
    m_sc[...] = jnp.full(m_sc.shape, -jnp.inf, _F32)
    acc_sc[...] = jnp.zeros(acc_sc.shape, _F32)


def _softmax_steps(tiles, *, qz_sc, m_sc, acc_sc, n_chains, n_maps, dqk):
    steps = [(ti, c) for ti in range(len(tiles)) for c in range(n_chains)]
    kts, vts = {}, {}

    def scores(ti, c):
        if ti not in kts:
            kts[ti] = tiles[ti][0]().astype(_MM)
        grp = (c * dqk) // LANES
        s = jnp.dot(kts[ti][:, grp * LANES:(grp + 1) * LANES], qz_sc[c], preferred_element_type=_F32)
        return s if tiles[ti][2] is None else s + tiles[ti][2](c)

    def values(ti, h):
        if (ti, h) not in vts:
            vt = tiles[ti][1](h).astype(_MM)
            vts[ti, h] = jnp.concatenate([vt, jnp.ones((SUM_ROWS, vt.shape[1]), _MM)], axis=0)
        return vts[ti, h]

    pending = [scores(*st) for st in steps[:SCORE_LOOKAHEAD]]
    for idx, (ti, c) in enumerate(steps):
        s = pending.pop(0)
        if idx + SCORE_LOOKAHEAD < len(steps):
            pending.append(scores(*steps[idx + SCORE_LOOKAHEAD]))
        m_prev = m_sc[c]
        m_new = jnp.maximum(m_prev, jnp.max(s, axis=0, keepdims=True))
        alpha = jnp.exp2(m_prev - m_new)
        p = jnp.exp2(s - m_new).astype(_MM)
        acc_sc[c] = alpha * acc_sc[c] + jnp.dot(values(ti, c // n_maps), p, preferred_element_type=_F32)
        m_sc[c] = m_new


def _attn_kernel(*refs, n_heads, n_maps, dqk, dv, tk, n_own, unroll, tkc, n_cache, diff, lam_init):
    qt_ref, k_ref, vt_ref = refs[:3]
    refs = refs[3:]
    if n_cache:
        kc_ref, vct_ref = refs[:2]
        refs = refs[2:]
    if diff:
        alam_ref, subg_ref = refs[:2]
        refs = refs[2:]
    o_ref, qz_sc, m_sc, acc_sc, ot_sc = refs
    n_chains = n_heads * n_maps
    _softmax_init(qt_ref, qz_sc, m_sc, acc_sc, n_chains, dqk)
    update = functools.partial(_softmax_steps, qz_sc=qz_sc, m_sc=m_sc, acc_sc=acc_sc, n_chains=n_chains,
                               n_maps=n_maps, dqk=dqk)

    cache_tiles = [(lambda j=j: kc_ref[j * tkc:(j + 1) * tkc, :],
                    lambda h, j=j: vct_ref[h * dv:(h + 1) * dv, j * tkc:(j + 1) * tkc], None) for j in range(n_cache)]

    def own_tiles(starts):
        return [(lambda st=st: k_ref[pl.ds(st, tk), :],
                 lambda h, st=st: vt_ref[h * dv:(h + 1) * dv, pl.ds(st, tk)], None) for st in starts]

    if unroll == n_own:
        update(cache_tiles + own_tiles([u * tk for u in range(n_own)]))
    else:
        if n_cache:
            update(cache_tiles)

        def body(j, carry):
            update(own_tiles([pl.multiple_of((j * unroll + u) * tk, tk) for u in range(unroll)]))
            return carry

        lax.fori_loop(0, n_own // unroll, body, 0)

    def normalised(c):
        a = acc_sc[c]
        return a[0:dv] / a[dv:dv + 1]

    if diff:
        lv = alam_ref[...]
        lam = (jnp.exp(jnp.sum(lv[0:1] * lv[1:2], axis=-1, keepdims=True))
               - jnp.exp(jnp.sum(lv[2:3] * lv[3:4], axis=-1, keepdims=True)) + lam_init)
    for h in range(n_heads):
        if diff:
            o = normalised(2 * h) - lam * normalised(2 * h + 1)
            o = o * lax.rsqrt(jnp.mean(o * o, axis=0, keepdims=True) + EPS) * subg_ref[...] * (1.0 - lam_init)
        else:
            o = normalised(h)
        ot_sc[h * dv:(h + 1) * dv, :] = o
    o_ref[...] = ot_sc[...].T.astype(o_ref.dtype)


def _attn(qt, k, vt, cache, diff_args, *, n_heads, n_maps, dqk, dv, lam_init=0.0, name):
    nb, s, _ = k.shape
    tq = min(ATTN_TILE, s)
    tk = min(ATTN_TILE, s)
    n_own = s // tk
    qmap = lambda b, i: (b, 0, i)
    bmap = lambda b, i: (b, 0, 0)
    in_specs = [pl.BlockSpec((None, HEAD_W, tq), qmap),
                pl.BlockSpec((None, s, HEAD_W), bmap),
                pl.BlockSpec((None, HEAD_W, s), bmap)]
    args = [qt, k, vt]
    tkc, n_cache = tk, 0
    if cache is not None:
        past = cache[0].shape[1]
        tkc = min(ATTN_TILE, past)
        n_cache = past // tkc
        in_specs += [pl.BlockSpec((None, past, HEAD_W), bmap), pl.BlockSpec((None, HEAD_W, past), bmap)]
        args += list(cache)
    if diff_args is not None:
        in_specs += [pl.BlockSpec(diff_args[0].shape, lambda b, i: (0, 0)),
                     pl.BlockSpec(diff_args[1].shape, lambda b, i: (0, 0))]
        args += list(diff_args)
    n_chains = n_heads * n_maps
    unroll = n_own if n_own <= MAX_UNROLLED_KEY_TILES else (8 if n_own % 8 == 0 else 1)
    kern = functools.partial(_attn_kernel, n_heads=n_heads, n_maps=n_maps, dqk=dqk, dv=dv, tk=tk, n_own=n_own,
                             unroll=unroll, tkc=tkc, n_cache=n_cache, diff=diff_args is not None, lam_init=lam_init)
    return pl.pallas_call(
        kern, grid=(nb, s // tq), in_specs=in_specs,
        out_specs=pl.BlockSpec((None, tq, HEAD_W), lambda b, i: (b, i, 0)),
        out_shape=jax.ShapeDtypeStruct((nb, s, HEAD_W), _MM),
        scratch_shapes=[pltpu.VMEM((n_chains, LANES, tq), _MM), pltpu.VMEM((n_chains, 1, tq), _F32),
                        pltpu.VMEM((n_chains, dv + SUM_ROWS, tq), _F32), pltpu.VMEM((n_heads * dv, tq), _F32)],
        compiler_params=_cparams(("arbitrary", "arbitrary"), 48),
        name=name,
    )(*args)


NBR_ROWS = 4
NBR_SPAN_ROWS = 12


def _nbr_kernel(qt_ref, k_ref, vt_ref, kc_ref, vct_ref, tb_ref, o_ref, qz_sc, m_sc, acc_sc, ot_sc, *, rows, tkc, n_cache):
    i = pl.program_id(1)
    ss = jnp.clip(i * NBR_ROWS - WIN_R // 2, 0, rows - NBR_SPAN_ROWS)
    tk = NBR_ROWS * GRID_W
    kstart = pl.multiple_of(ss * GRID_W, tk)
    hsl = lambda h: slice(h * DH_B, (h + 1) * DH_B)

    _softmax_init(qt_ref, qz_sc, m_sc, acc_sc, H_B, DH_B)
    tiles = [(lambda j=j: kc_ref[j * tkc:(j + 1) * tkc, :],
              lambda h, j=j: vct_ref[hsl(h), j * tkc:(j + 1) * tkc], None) for j in range(n_cache)]
    starts = [pl.multiple_of(kstart + u * tk, tk) for u in range(NBR_SPAN_ROWS // NBR_ROWS)]
    tiles += [(lambda st=st: k_ref[pl.ds(st, tk), :],
               lambda h, st=st: vt_ref[hsl(h), pl.ds(st, tk)],
               lambda c, u=u: tb_ref[c, u * tk:(u + 1) * tk, :]) for u, st in enumerate(starts)]
    _softmax_steps(tiles, qz_sc=qz_sc, m_sc=m_sc, acc_sc=acc_sc, n_chains=H_B, n_maps=1, dqk=DH_B)
    for h in range(H_B):
        a = acc_sc[h]
        ot_sc[hsl(h), :] = a[0:DH_B] / a[DH_B:DH_B + 1]
    o_ref[...] = ot_sc[...].T.astype(o_ref.dtype)


def _nbr_bias_table(rpb, rows):
    nh, n_r, n_c = rpb.shape
    c = np.arange(GRID_W)[:, None]
    kc = np.arange(GRID_W)[None, :]
    cidx = np.clip(kc - c + WIN_C - 1, 0, n_c - 1).reshape(1, -1)
    onehot = jnp.asarray((cidx == np.arange(n_c)[:, None]).astype(np.float32))
    toep = jnp.dot(rpb.reshape(nh * n_r, n_c).astype(_F32), onehot, precision=lax.Precision.HIGHEST)
    toep = toep.reshape(nh, n_r, GRID_W, GRID_W)
    cstart = np.clip(c - WIN_C // 2, 0, GRID_W - WIN_C)
    valid = (kc >= cstart) & (kc < cstart + WIN_C)
    toep = jnp.where(valid[None, None], toep, -jnp.inf) * LOG2E
    assert rows % NBR_ROWS == 0 and rows >= NBR_SPAN_ROWS
    r0 = np.array([0, NBR_ROWS, rows - NBR_ROWS])[:, None, None]
    ss = np.clip(r0 - WIN_R // 2, 0, rows - NBR_SPAN_ROWS)
    key_row = ss + np.arange(NBR_SPAN_ROWS)[None, :, None]
    r = r0 + np.arange(NBR_ROWS)[None, None, :]
    rs = np.clip(r - WIN_R // 2, 0, rows - WIN_R)
    in_rows = (key_row >= rs) & (key_row < rs + WIN_R)
    ridx = np.clip(key_row - r + WIN_R - 1, 0, n_r - 1)
    tb = jnp.take(toep, jnp.asarray(ridx.reshape(-1)), axis=1)
    tb = tb.reshape((nh,) + ridx.shape + (GRID_W, GRID_W))
    tb = jnp.where(in_rows[None, :, :, :, None, None], tb, -jnp.inf)
    tb = tb.transpose(1, 0, 2, 5, 3, 4)
    return tb.reshape(3, nh, NBR_SPAN_ROWS * GRID_W, NBR_ROWS * GRID_W)


def _nbr(qt, k, vt, kc, vct, tb):
    nb, s, _ = k.shape
    rows = s // GRID_W
    n_steps = rows // NBR_ROWS
    past = kc.shape[1]
    tq = NBR_ROWS * GRID_W
    bmap = lambda b, i: (b, 0, 0)
    variant = lambda b, i: (jnp.where(i == 0, 0, jnp.where(i == n_steps - 1, 2, 1)), 0, 0, 0)
    tkc = min(ATTN_TILE, past)
    return pl.pallas_call(
        functools.partial(_nbr_kernel, rows=rows, tkc=tkc, n_cache=past // tkc), grid=(nb, n_steps),
        in_specs=[pl.BlockSpec((None, HEAD_W, tq), lambda b, i: (b, 0, i)),
                  pl.BlockSpec((None, s, HEAD_W), bmap),
                  pl.BlockSpec((None, HEAD_W, s), bmap),
                  pl.BlockSpec((None, past, HEAD_W), bmap),
                  pl.BlockSpec((None, HEAD_W, past), bmap),
                  pl.BlockSpec((None,) + tb.shape[1:], variant)],
        out_specs=pl.BlockSpec((None, tq, HEAD_W), lambda b, i: (b, i, 0)),
        out_shape=jax.ShapeDtypeStruct((nb, s, HEAD_W), _MM),
        scratch_shapes=[pltpu.VMEM((H_B, LANES, tq), _MM), pltpu.VMEM((H_B, 1, tq), _F32),
                        pltpu.VMEM((H_B, DH_B + SUM_ROWS, tq), _F32), pltpu.VMEM((HEAD_W, tq), _F32)],
        compiler_params=_cparams(("arbitrary", "arbitrary"), 48),
        name="nbr_attn",
    )(qt, k, vt, kc, vct, tb)


def _mla_cache_kernel(ckv_ref, kr_ref, wkc_ref, wvt_ref, e_ref, kc_ref, vct_ref):
    ckv = ckv_ref[...]
    kc_ref[...] = (_dot(ckv, wkc_ref[...]) + _dot(kr_ref[...], e_ref[...])).astype(kc_ref.dtype)
    vct_ref[...] = _dot_nt(wvt_ref[...], ckv).astype(vct_ref.dtype)


def _mla_cache(ckv, kr, wkc, wvt, expand):
    nb, past, _ = ckv.shape
    bmap = lambda b: (b, 0, 0)
    const = lambda b: (0, 0)
    return pl.pallas_call(
        _mla_cache_kernel, grid=(nb,),
        in_specs=[pl.BlockSpec((None, past, KV_LORA), bmap), pl.BlockSpec((None, past, ROPE_C), bmap),
                  pl.BlockSpec(wkc.shape, const), pl.BlockSpec(wvt.shape, const), pl.BlockSpec(expand.shape, const)],
        out_specs=[pl.BlockSpec((None, past, HEAD_W), bmap), pl.BlockSpec((None, HEAD_W, past), bmap)],
        out_shape=[jax.ShapeDtypeStruct((nb, past, HEAD_W), _MM), jax.ShapeDtypeStruct((nb, HEAD_W, past), _MM)],
        compiler_params=_cparams(("arbitrary",), 32),
        name="mla_cache",
    )(ckv, kr, wkc, wvt, expand)


CONV_HALO = 16


def _conv_kernel(z_ref, w_ref, b_ref, g_ref, beta_ref, o_ref, zp_ref, *, seq, rc):
    zp_ref[0:CONV_HALO, :] = jnp.zeros((CONV_HALO, CONV_CH), _F32)
    zp_ref[CONV_HALO:CONV_HALO + seq, :] = z_ref[...]
    zp_ref[CONV_HALO + seq:2 * CONV_HALO + seq, :] = jnp.zeros((CONV_HALO, CONV_CH), _F32)
    w = w_ref[...]
    shift = CONV_HALO - CONV_K // 2

    def chunk(c, carry):
        base = pl.multiple_of(c * rc, rc)
        n_win = rc + 2 * CONV_HALO
        win = zp_ref[pl.ds(base, n_win), :]
        acc = jnp.zeros((rc, CONV_CH), _F32)
        for phase in range(SUBLANES):
            rolled = win if phase == 0 else pltpu.roll(win, n_win - phase, axis=0)
            for j in range(CONV_K):
                if (j + shift) % SUBLANES == phase:
                    a = (j + shift) - phase
                    acc = acc + rolled[a:a + rc, :] * w[j:j + 1, :]
        y = acc + b_ref[...]
        mu = jnp.mean(y, axis=-1, keepdims=True)
        yc = y - mu
        var = jnp.mean(yc * yc, axis=-1, keepdims=True)
        yn = yc * lax.rsqrt(var + EPS) * g_ref[...] + beta_ref[...]
        o_ref[pl.ds(base, rc), :] = (yn * _sigmoid(yn)).astype(o_ref.dtype)
        return carry

    lax.fori_loop(0, seq // rc, chunk, 0)


def _conv(z, w, b, g, beta):
    nb, seq, _ = z.shape
    rc = min(CONV_CHUNK, seq)
    bmap = lambda i: (i, 0, 0)
    const = lambda i: (0, 0)
    kern = functools.partial(_conv_kernel, seq=seq, rc=rc)
    return pl.pallas_call(
        kern, grid=(nb,),
        in_specs=[pl.BlockSpec((None, seq, CONV_CH), bmap), pl.BlockSpec(w.shape, const),
                  pl.BlockSpec(b.shape, const), pl.BlockSpec(g.shape, const), pl.BlockSpec(beta.shape, const)],
        out_specs=pl.BlockSpec((None, seq, CONV_CH), bmap),
        out_shape=jax.ShapeDtypeStruct((nb, seq, CONV_CH), _MM),
        scratch_shapes=[pltpu.VMEM((seq + 2 * CONV_HALO, CONV_CH), _F32)],
        compiler_params=_cparams(("arbitrary",), 40),
        name="conv",
    )(z, w, b, g, beta)


POST_ROW_PARTS = 2

def _post_kernel(x_ref, mod_ref, oa_ref, ob_ref, oc_ref, od_ref, g1_ref, wg_ref, bg_ref,
                 awo_ref, bwo_ref, cwo_ref, dwo_ref, wout_ref, g2_ref, rw_ref, rb_ref,
                 x1_ref, h2_ref, dw_ref, *, d):
    mod = mod_ref[0]
    shift1, scale1, gate1 = mod[:, 0:d], mod[:, d:2 * d], mod[:, 2 * d:3 * d]
    shift2, scale2 = mod[:, 3 * d:4 * d], mod[:, 4 * d:5 * d]
    branches = ((oa_ref, awo_ref), (ob_ref, bwo_ref), (oc_ref, cwo_ref), (od_ref, dwo_ref))
    tm = x_ref.shape[0]
    parts = [slice(k * tm // POST_ROW_PARTS, (k + 1) * tm // POST_ROW_PARTS) for k in range(POST_ROW_PARTS)]

    xs, branch_mm = [], []
    for rs in parts:
        x = x_ref[rs, :]
        h = (_rms(x, g1_ref[...]) * (1.0 + scale1) + shift1).astype(_MM)
        xs.append(x)
        branch_mm.append([(_dot(h, wg_ref[:, bi * d:(bi + 1) * d]), _dot(o_ref[rs, :], wo_ref[...]))
                          for bi, (o_ref, wo_ref) in enumerate(branches)])
    mixed = []
    for pairs in branch_mm:
        merged = None
        for bi, (gate_logits, out) in enumerate(pairs):
            term = _sigmoid(gate_logits + bg_ref[:, bi * d:(bi + 1) * d]) * out
            merged = term if merged is None else merged + term
        mixed.append(_dot(merged, wout_ref[...]))
    logits = []
    for rs, x, y in zip(parts, xs, mixed):
        x1 = x + gate1 * y
        x1_ref[rs, :] = x1
        h2_mm = (_rms(x1, g2_ref[...]) * (1.0 + scale2) + shift2).astype(_MM)
        h2_ref[rs, :] = h2_mm
        logits.append(_dot(h2_mm, rw_ref[...]) + rb_ref[...])
    for rs, r in zip(parts, logits):
        lgp, lep = r[:, 0:LANES], r[:, LANES:2 * LANES]
        lane = lax.broadcasted_iota(jnp.int32, lgp.shape, 1)
        neg = -jnp.inf
        lg = jnp.where(lane < N_GROUPS, lgp, neg)
        mg = jnp.max(lg, axis=-1, keepdims=True)
        gi = jnp.min(jnp.where(lg == mg, lane, LANES), axis=-1, keepdims=True)
        gw = 1.0 / jnp.sum(jnp.exp(lg - mg), axis=-1, keepdims=True)
        in_group = (lane < N_EXPERTS) & ((lane // EXPERTS_PER_GROUP) == gi)
        le = jnp.where(in_group, lep, neg)
        t1 = jnp.max(le, axis=-1, keepdims=True)
        i1 = jnp.min(jnp.where(le == t1, lane, LANES), axis=-1, keepdims=True)
        le2 = jnp.where(lane == i1, neg, le)
        t2 = jnp.max(le2, axis=-1, keepdims=True)
        i2 = jnp.min(jnp.where(le2 == t2, lane, LANES), axis=-1, keepdims=True)
        e2 = jnp.exp(t2 - t1)
        w1 = gw / (1.0 + e2)
        w2 = gw * e2 / (1.0 + e2)
        dw_ref[rs, :] = jnp.where(lane == i1, w1, 0.0) + jnp.where(lane == i2, w2, 0.0)


def _post(x2d, mods3, mod_row, oa, ob, oc, od, lw):
    t, d = x2d.shape
    tm = min(POST_TILE, t)
    row = lambda i: (i, 0)
    const = lambda i: (0, 0)
    wnames = ("g1", "wgate", "bgate", "awo", "bwo", "cwo", "dwo", "wout", "g2", "rw", "rb")
    in_specs = [pl.BlockSpec((tm, d), row), pl.BlockSpec((1, 1, 6 * d), lambda i: (mod_row(i * tm), 0, 0))]
    in_specs += [pl.BlockSpec((tm, HEAD_W), row)] * 4
    in_specs += [pl.BlockSpec(lw[n].shape, const) for n in wnames]
    args = [x2d, mods3, oa, ob, oc, od] + [lw[n] for n in wnames]
    return pl.pallas_call(
        functools.partial(_post_kernel, d=d), grid=(t // tm,), in_specs=in_specs,
        out_specs=[pl.BlockSpec((tm, d), row), pl.BlockSpec((tm, d), row), pl.BlockSpec((tm, LANES), row)],
        out_shape=[jax.ShapeDtypeStruct((t, d), _F32), jax.ShapeDtypeStruct((t, d), _MM),
                   jax.ShapeDtypeStruct((t, LANES), _F32)],
        compiler_params=_cparams(("arbitrary",), 56),
        name="post",
    )(*args)


def _moe_kernel(h2_ref, dw_ref, x1_ref, mod_ref, w1_ref, w2_ref, fg_ref, o_ref, acc_ref, *, d, final):
    g = pl.program_id(1)

    @pl.when(g == 0)
    def _():
        acc_ref[...] = jnp.zeros_like(acc_ref)

    h2 = h2_ref[...]
    dw = dw_ref[...]
    lane = lax.broadcasted_iota(jnp.int32, dw.shape, 1)
    acts = []
    for j in range(EXPERTS_PER_GROUP):
        hid = _dot(h2, w1_ref[j])
        gate, up = hid[:, 0:D_EXPERT], hid[:, D_EXPERT:2 * D_EXPERT]
        we = jnp.sum(jnp.where(lane == g * EXPERTS_PER_GROUP + j, dw, 0.0), axis=-1, keepdims=True)
        acts.append((gate * _sigmoid(gate) * up * we).astype(_MM))
    acc_ref[...] += _dot(jnp.concatenate(acts, axis=-1), w2_ref[...])

    @pl.when(g == pl.num_programs(1) - 1)
    def _():
        gate2 = mod_ref[0][:, 5 * d:6 * d]
        x2 = x1_ref[...] + gate2 * acc_ref[...]
        if final:
            x2 = _rms(x2, fg_ref[...])
        o_ref[...] = x2


def _moe(h2, dw, x1, mods3, mod_row, w1, w2, fg, *, final):
    t, d = x1.shape
    tm = min(MOE_TILE, t)
    n_e, _, d_hid = w1.shape
    w2g = w2.reshape(N_GROUPS, EXPERTS_PER_GROUP * D_EXPERT, d)
    row = lambda i, g: (i, 0)
    return pl.pallas_call(
        functools.partial(_moe_kernel, d=d, final=final), grid=(t // tm, N_GROUPS),
        in_specs=[pl.BlockSpec((tm, d), row), pl.BlockSpec((tm, LANES), row), pl.BlockSpec((tm, d), row),
                  pl.BlockSpec((1, 1, 6 * d), lambda i, g: (mod_row(i * tm), 0, 0)),
                  pl.BlockSpec((EXPERTS_PER_GROUP, d, d_hid), lambda i, g: (g, 0, 0)),
                  pl.BlockSpec((None,) + w2g.shape[1:], lambda i, g: (g, 0, 0)),
                  pl.BlockSpec((1, d), lambda i, g: (0, 0))],
        out_specs=pl.BlockSpec((tm, d), row),
        out_shape=jax.ShapeDtypeStruct((t, d), _F32),
        scratch_shapes=[pltpu.VMEM((tm, d), _F32)],
        compiler_params=_cparams(("arbitrary", "arbitrary"), 56),
        name="moe_final" if final else "moe",
    )(h2, dw, x1, mods3, w1, w2g, fg)


def _rope_tables(n_tok, dim):
    hq = dim // 4
    tpos = np.arange(n_tok)
    rows = (tpos // GRID_W).astype(np.float32)
    cols = (tpos % GRID_W).astype(np.float32)
    inv = jnp.asarray(ROPE_THETA, _F32) ** (-jnp.arange(hq, dtype=_F32) / hq)
    ar = jnp.asarray(rows)[:, None] * inv[None, :]
    ac = jnp.asarray(cols)[:, None] * inv[None, :]
    ang = jnp.concatenate([ar, ar, ac, ac], axis=-1)
    return jnp.cos(ang), jnp.sin(ang)


def _layer_weights(l, p, grid_rows):
    mm = lambda a: a.astype(_MM)
    w_in = p["w_in"][l]
    pts = np.cumsum([0, 2 * H_A * DH_A, 2 * H_A * DH_A, 2 * H_A * DH_A, H_B * DH_B, H_B * DH_B, H_B * DH_B,
                     Q_LORA, KV_LORA, ROPE_C, 2 * CONV_CH])
    seg = lambda i, j: w_in[:, pts[i]:pts[j]]
    d = w_in.shape[0]
    wckr = seg(8, 9)
    zeros_n = jnp.zeros((d, NOPE_C), w_in.dtype)
    wckr_t = jnp.concatenate([zeros_n, wckr] * H_C, axis=1)
    wukv = p["c_wukv"][l].reshape(KV_LORA, H_C, NOPE_C + DV_C)
    wkc = jnp.concatenate([wukv[:, :, :NOPE_C], jnp.zeros((KV_LORA, H_C, ROPE_C), wukv.dtype)], axis=-1)
    wkc = wkc.reshape(KV_LORA, H_C * (NOPE_C + ROPE_C))
    wv = wukv[:, :, NOPE_C:].reshape(KV_LORA, H_C * DV_C)
    router = jnp.zeros((d, 2 * LANES), _F32)
    router = router.at[:, 0:N_GROUPS].set(p["moe_w_group"][l]).at[:, LANES:LANES + N_EXPERTS].set(p["moe_w_expert"][l])
    rb = jnp.zeros((1, 2 * LANES), _F32)
    rb = rb.at[0, 0:N_GROUPS].set(p["moe_b_group"][l]).at[0, LANES:LANES + N_EXPERTS].set(p["moe_b_expert"][l])
    conv_w = p["d_conv_w"][l]
    return dict(
        g1=p["norm1_g"][l][None], g2=p["norm2_g"][l][None],
        wa=mm(seg(0, 3)), wb=mm(seg(3, 6)), wcq=mm(seg(6, 7)), wckv=mm(seg(7, 8)), wckr=mm(wckr_t), wdu=mm(seg(9, 10)),
        cqg=p["c_q_norm_g"][l][None], ckvg=p["c_kv_norm_g"][l][None],
        wuq=mm(p["c_wuq"][l]), wkc=mm(wkc), wv=mm(wv), wvt=mm(wv.T),
        alam=p["a_lambda"][l], subg=p["a_subln_g"][l][:, None],
        wgate=mm(p["w_gate"][l]), bgate=p["b_gate"][l][None],
        awo=mm(p["a_wo"][l]), bwo=mm(p["b_wo"][l]), cwo=mm(p["c_wo"][l]), dwo=mm(p["d_wo"][l]),
        wout=mm(p["w_out"][l]), rw=mm(router), rb=rb,
        conv_w=conv_w, conv_b=p["d_conv_b"][l][None], ln_g=p["d_ln_g"][l][None], ln_b=p["d_ln_b"][l][None],
        w1=mm(p["moe_w1"][l]), w2=mm(p["moe_w2"][l]),
        tb=_nbr_bias_table(p["b_rpb"][l], grid_rows),
    )


def _layer(x2d, mods3, mod_row, lw, l, n_layers, *, nb, seq, fg, ctx, tables):
    latent = ctx is not None
    t = x2d.shape[0]
    outs = _pre(x2d, mods3, mod_row, lw, latent=latent, seq=seq, tables=tables)
    r3 = lambda a: a.reshape(nb, seq, a.shape[-1])
    lam_init = 0.8 - 0.6 * math.exp(-0.3 * l)
    diff_args = (lw["alam"], lw["subg"])
    if latent:
        aqt, ak, avt, bqt, bk, bvt, qct, kc, vct, z = outs
        own = None
        a_k_c, a_vt_c, b_k_c, b_vt_c, c_k_c, c_vt_c = ctx
        oa = _attn(aqt, r3(ak), avt, (a_k_c, a_vt_c), diff_args, n_heads=H_A, n_maps=2,
                   dqk=DH_A, dv=2 * DH_A, lam_init=lam_init, name="attn_a_lat")
        ob = _nbr(bqt, r3(bk), bvt, b_k_c, b_vt_c, lw["tb"])
        oc = _attn(qct, r3(kc), vct, (c_k_c, c_vt_c), None, n_heads=H_C, n_maps=1,
                   dqk=NOPE_C + ROPE_C, dv=DV_C, name="attn_c_lat")
    else:
        aqt, ak, av, avt, bqt, bk, bv, bvt, qct, kc, vct, ckvn, ckr, z = outs
        own = (ak, av, bk, bv, ckvn, ckr)
        oa = _attn(aqt, r3(ak), avt, None, diff_args, n_heads=H_A, n_maps=2,
                   dqk=DH_A, dv=2 * DH_A, lam_init=lam_init, name="attn_a_ctx")
        ob = _attn(bqt, r3(bk), bvt, None, None, n_heads=H_B, n_maps=1, dqk=DH_B, dv=DH_B, name="attn_b_ctx")
        oc = _attn(qct, r3(kc), vct, None, None, n_heads=H_C, n_maps=1,
                   dqk=NOPE_C + ROPE_C, dv=DV_C, name="attn_c_ctx")
    od = _conv(r3(z), lw["conv_w"], lw["conv_b"], lw["ln_g"], lw["ln_b"])
    f2 = lambda a: a.reshape(t, a.shape[-1])
    x1, h2, dw = _post(x2d, mods3, mod_row, f2(oa), f2(ob), f2(oc), f2(od), lw)
    x2 = _moe(h2, dw, x1, mods3, mod_row, lw["w1"], lw["w2"], fg, final=(l == n_layers - 1))
    return x2, own


def kernel(x_prompt, x_sample, cache_a_k, cache_a_v, cache_b_k, cache_b_v, cache_c_kv, cache_c_krope, c, c_ctx, norm1_g, norm2_g, w_ada, b_ada, w_in, a_lambda, a_subln_g, a_wo, b_rpb, b_wo, c_q_norm_g, c_kv_norm_g, c_wuq, c_wukv, c_wo, d_conv_w, d_conv_b, d_ln_g, d_ln_b, d_wo, w_gate, b_gate, w_out, moe_w_group, moe_b_group, moe_w_expert, moe_b_expert, moe_w1, moe_w2, final_norm_g):
    p = dict(norm1_g=norm1_g, norm2_g=norm2_g, w_in=w_in, a_lambda=a_lambda, a_subln_g=a_subln_g, a_wo=a_wo,
             b_rpb=b_rpb, b_wo=b_wo, c_q_norm_g=c_q_norm_g, c_kv_norm_g=c_kv_norm_g, c_wuq=c_wuq, c_wukv=c_wukv,
             c_wo=c_wo, d_conv_w=d_conv_w, d_conv_b=d_conv_b, d_ln_g=d_ln_g, d_ln_b=d_ln_b, d_wo=d_wo,
             w_gate=w_gate, b_gate=b_gate, w_out=w_out, moe_w_group=moe_w_group, moe_b_group=moe_b_group,
             moe_w_expert=moe_w_expert, moe_b_expert=moe_b_expert, moe_w1=moe_w1, moe_w2=moe_w2)
    n_layers = w_in.shape[0]
    bc, sc, d = x_prompt.shape
    bl, sl, _ = x_sample.shape
    past = cache_a_k.shape[2]
    assert bl + 1 <= COND_ROWS and sl % GRID_W == 0

    cond8 = jnp.zeros((COND_ROWS, d), _F32).at[0].set(c_ctx).at[1:1 + bl].set(c)
    mods3 = _adaln(cond8, w_ada, b_ada).reshape(n_layers * COND_ROWS, 1, 6 * d)
    fg = final_norm_g[None]

    cos_a, sin_a = _rope_tables(sl, DH_A)
    cos_c, sin_c = _rope_tables(sl, ROPE_C)
    reps_a = HEAD_W // DH_A
    ones_n, zeros_n = jnp.ones((sl, NOPE_C), _F32), jnp.zeros((sl, NOPE_C), _F32)
    tables = (jnp.tile(cos_a, (1, reps_a)), jnp.tile(sin_a, (1, reps_a)),
              jnp.concatenate([ones_n, cos_c] * H_C, axis=1), jnp.concatenate([zeros_n, sin_c] * H_C, axis=1))
    expand = np.zeros((ROPE_C, HEAD_W), np.float32)
    for hh in range(H_C):
        expand[np.arange(ROPE_C), hh * (NOPE_C + ROPE_C) + NOPE_C + np.arange(ROPE_C)] = 1.0
    expand = jnp.asarray(expand, _MM)

    lws = [_layer_weights(l, p, sl // GRID_W) for l in range(n_layers)]

    xp = x_prompt.reshape(bc * sc, d)
    states = []
    for l in range(n_layers):
        xp, own = _layer(xp, mods3, (lambda tok, l=l: l * COND_ROWS), lws[l], l, n_layers, nb=bc, seq=sc, fg=fg, ctx=None,
                         tables=None)
        states.append(own)
    y_prompt = xp.reshape(bc, sc, d)

    xs = x_sample.reshape(bl * sl, d)
    for l in range(n_layers):
        flat = lambda a: a[:, l].reshape(bl, past, -1)
        c_k_c, c_vt_c = _mla_cache(flat(cache_c_kv), flat(cache_c_krope), lws[l]["wkc"], lws[l]["wvt"], expand)
        ctx = (flat(cache_a_k).astype(_MM), jnp.swapaxes(flat(cache_a_v), 1, 2).astype(_MM),
               flat(cache_b_k).astype(_MM), jnp.swapaxes(flat(cache_b_v), 1, 2).astype(_MM), c_k_c, c_vt_c)
        xs, _ = _layer(xs, mods3, (lambda tok, l=l: l * COND_ROWS + 1 + tok // sl), lws[l], l, n_layers,
                       nb=bl, seq=sl, fg=fg, ctx=ctx, tables=tables)
    y_sample = xs.reshape(bl, sl, d)

    st = lambda k, shape: jnp.stack([s[k].reshape(shape) for s in states], axis=1)
    new_a_k = st(0, (bc, sc, H_A, 2 * DH_A))
    new_a_v = st(1, (bc, sc, H_A, 2 * DH_A))
    new_b_k = st(2, (bc, sc, H_B, DH_B))
    new_b_v = st(3, (bc, sc, H_B, DH_B))
    new_c_kv = st(4, (bc, sc, KV_LORA))
    new_c_krope = st(5, (bc, sc, ROPE_C))
    return (y_prompt, y_sample, new_a_k, new_a_v, new_b_k, new_b_v, new_c_kv, new_c_krope)
```

```python
import functools
import math

import numpy as np
import jax
import jax.numpy as jnp
from jax import lax
from jax.experimental import pallas as pl
from jax.experimental.pallas import tpu as pltpu

GRID_W = 64
EPS = 1e-6
ROPE_THETA = 10000.0
H_A, DH_A = 4, 32
H_B, DH_B = 4, 64
WIN_R, WIN_C = 8, 16
H_C, Q_LORA, KV_LORA, NOPE_C, ROPE_C, DV_C = 4, 256, 128, 32, 32, 64
CONV_CH, CONV_K = 256, 31
N_BRANCH = 4
N_GROUPS, EXPERTS_PER_GROUP, N_EXPERTS, D_EXPERT = 4, 4, 16, 256
HEAD_W = 256
LANES = 128
SUBLANES = 8
V7X_VMEM_BYTES = 64 * 1024 * 1024
LOG2E = math.log2(math.e)

PRE_TILE = 512
POST_TILE = 512
MOE_TILE = 1024
ATTN_TILE = 256
CONV_CHUNK = 128
ADALN_COLS = 1536
COND_ROWS = SUBLANES

_MM = jnp.bfloat16
_F32 = jnp.float32


def _cparams(sem, vmem_mb):
    return pltpu.CompilerParams(dimension_semantics=sem,
                                vmem_limit_bytes=min(vmem_mb * 1024 * 1024, V7X_VMEM_BYTES - 8 * 1024 * 1024))


def _dot(a, b):
    return jnp.dot(a.astype(_MM), b.astype(_MM), preferred_element_type=_F32)


def _dot_nt(a, b):
    return lax.dot_general(a.astype(_MM), b.astype(_MM), (((1,), (1,)), ((), ())),
                           preferred_element_type=_F32)


def _sigmoid(x):
    return 1.0 / (1.0 + jnp.exp(-x))


def _rms(x, g):
    return x * lax.rsqrt(jnp.mean(x * x, axis=-1, keepdims=True) + EPS) * g


def _rope(x, cos, sin):
    w = x.shape[-1]
    lane = lax.broadcasted_iota(jnp.int32, x.shape, 1)
    first = (lane % 16) < 8
    fwd = pltpu.roll(x, w - 8, axis=1)
    bwd = pltpu.roll(x, 8, axis=1)
    return x * cos + jnp.where(first, -fwd, bwd) * sin


def _adaln_kernel(cond_ref, w_ref, b_ref, o_ref):
    c = cond_ref[...]
    a = c * _sigmoid(c)
    o_ref[0] = _dot(a, w_ref[0]) + b_ref[0]


def _adaln(cond8, w_ada, b_ada):
    n_l, d, d6 = w_ada.shape
    tn = ADALN_COLS
    return pl.pallas_call(
        _adaln_kernel,
        grid=(n_l, d6 // tn),
        in_specs=[pl.BlockSpec((COND_ROWS, d), lambda l, j: (0, 0)),
                  pl.BlockSpec((1, d, tn), lambda l, j: (l, 0, j)),
                  pl.BlockSpec((1, 1, tn), lambda l, j: (l, 0, j))],
        out_specs=pl.BlockSpec((1, COND_ROWS, tn), lambda l, j: (l, 0, j)),
        out_shape=jax.ShapeDtypeStruct((n_l, COND_ROWS, d6), _F32),
        compiler_params=_cparams(("arbitrary", "arbitrary"), 40),
        name="adaln",
    )(cond8, w_ada, b_ada.reshape(n_l, 1, d6))


PRE_ROW_PARTS = 2


def _pre_kernel(*refs, d, latent, sa, sb, sc):
    (x_ref, mod_ref, g1_ref, wa_ref, wb_ref, wcq_ref, wckv_ref, wckr_ref, wdu_ref,
     cqg_ref, ckvg_ref, wuq_ref, wkc_ref, wv_ref) = refs[:14]
    refs = refs[14:]
    if latent:
        cosa_ref, sina_ref, cosc_ref, sinc_ref = refs[:4]
        refs = refs[4:]
        aqt_ref, ak_ref, avt_ref, bqt_ref, bk_ref, bvt_ref, qct_ref, kc_ref, vct_ref, z_ref = refs
    else:
        (aqt_ref, ak_ref, av_ref, avt_ref, bqt_ref, bk_ref, bv_ref, bvt_ref, qct_ref, kc_ref, vct_ref,
         ckvn_ref, ckr_ref, z_ref) = refs

    mod = mod_ref[0]
    shift1, scale1 = mod[:, 0:d], mod[:, d:2 * d]
    tm = x_ref.shape[0]
    n_parts = PRE_ROW_PARTS if tm % (PRE_ROW_PARTS * LANES) == 0 else 1
    parts = [slice(k * tm // n_parts, (k + 1) * tm // n_parts) for k in range(n_parts)]

    proj = []
    for rs in parts:
        h = (_rms(x_ref[rs, :], g1_ref[...]) * (1.0 + scale1) + shift1).astype(_MM)
        proj.append([_dot(h, w_ref[...]) for w_ref in (wa_ref, wb_ref, wcq_ref, wckv_ref, wckr_ref, wdu_ref)])

    for rs, (pa, pb, cq, ckv, kr, du) in zip(parts, proj):
        aq, ak, av = pa[:, 0:HEAD_W], pa[:, HEAD_W:2 * HEAD_W], pa[:, 2 * HEAD_W:3 * HEAD_W]
        if latent:
            cosa, sina = cosa_ref[rs, :], sina_ref[rs, :]
            aq = _rope(aq, cosa, sina)
            ak = _rope(ak, cosa, sina)
        else:
            av_ref[rs, :] = av
        aqt_ref[:, rs] = (aq * (sa * LOG2E)).T.astype(aqt_ref.dtype)
        ak_ref[rs, :] = ak.astype(ak_ref.dtype)
        avt_ref[:, rs] = av.T.astype(avt_ref.dtype)

        bq, bk, bv = pb[:, 0:HEAD_W], pb[:, HEAD_W:2 * HEAD_W], pb[:, 2 * HEAD_W:3 * HEAD_W]
        bk_ref[rs, :] = bk.astype(bk_ref.dtype)
        if not latent:
            bv_ref[rs, :] = bv
        bqt_ref[:, rs] = (bq * (sb * LOG2E)).T.astype(bqt_ref.dtype)
        bvt_ref[:, rs] = bv.T.astype(bvt_ref.dtype)

        qc = _dot(_rms(cq, cqg_ref[...]), wuq_ref[...])
        ckvn = _rms(ckv, ckvg_ref[...])
        if latent:
            cosc, sinc = cosc_ref[rs, :], sinc_ref[rs, :]
            qc = _rope(qc, cosc, sinc)
            kr = _rope(kr, cosc, sinc)
        else:
            ckvn_ref[rs, :] = ckvn
            ckr_ref[rs, :] = kr[:, NOPE_C:NOPE_C + ROPE_C]
        qct_ref[:, rs] = (qc * (sc * LOG2E)).T.astype(qct_ref.dtype)
        kc_ref[rs, :] = (_dot(ckvn, wkc_ref[...]) + kr).astype(kc_ref.dtype)
        vct_ref[:, rs] = _dot(ckvn, wv_ref[...]).T.astype(vct_ref.dtype)
        z_ref[rs, :] = du[:, 0:CONV_CH] * _sigmoid(du[:, CONV_CH:2 * CONV_CH])


def _pre(x2d, mods3, mod_row, lw, *, latent, seq, tables):
    t, d = x2d.shape
    tm = min(PRE_TILE, seq)
    n_seq_tiles = seq // tm
    grid = (t // tm,)
    row = lambda i: (i, 0)
    const = lambda i: (0, 0)
    wnames = ("wa", "wb", "wcq", "wckv", "wckr", "wdu", "cqg", "ckvg", "wuq", "wkc", "wv")
    in_specs = [pl.BlockSpec((tm, d), row),
                pl.BlockSpec((1, 1, 6 * d), lambda i: (mod_row(i * tm), 0, 0)),
                pl.BlockSpec((1, d), const)]
    args = [x2d, mods3, lw["g1"]]
    for n in wnames:
        in_specs.append(pl.BlockSpec(lw[n].shape, const))
        args.append(lw[n])
    if latent:
        for tb in tables:
            in_specs.append(pl.BlockSpec((tm, HEAD_W), lambda i: (i % n_seq_tiles, 0)))
            args.append(tb)
    nat = lambda dt: (jax.ShapeDtypeStruct((t, HEAD_W), dt), pl.BlockSpec((tm, HEAD_W), row))
    tr = (jax.ShapeDtypeStruct((t // seq, HEAD_W, seq), _MM),
          pl.BlockSpec((None, HEAD_W, tm), lambda i: (i // n_seq_tiles, 0, i % n_seq_tiles)))
    if latent:
        outs = [tr, nat(_MM), tr, tr, nat(_MM), tr, tr, nat(_MM), tr]
    else:
        outs = [tr, nat(_F32), nat(_F32), tr, tr, nat(_F32), nat(_F32), tr, tr, nat(_MM), tr,
                (jax.ShapeDtypeStruct((t, KV_LORA), _F32), pl.BlockSpec((tm, KV_LORA), row)),
                (jax.ShapeDtypeStruct((t, ROPE_C), _F32), pl.BlockSpec((tm, ROPE_C), row))]
    outs.append((jax.ShapeDtypeStruct((t, CONV_CH), _F32), pl.BlockSpec((tm, CONV_CH), row)))
    out_shape = [o[0] for o in outs]
    out_specs = [o[1] for o in outs]
    kern = functools.partial(_pre_kernel, d=d, latent=latent, sa=DH_A ** -0.5, sb=DH_B ** -0.5,
                             sc=(NOPE_C + ROPE_C) ** -0.5)
    return pl.pallas_call(kern, grid=grid, in_specs=in_specs, out_specs=out_specs, out_shape=out_shape,
                          compiler_params=_cparams(("arbitrary",), 48),
                          name="pre_lat" if latent else "pre_ctx")(*args)


SCORE_LOOKAHEAD = 5
SUM_ROWS = 16
MAX_UNROLLED_KEY_TILES = 16


def _softmax_init(qt_ref, qz_sc, m_sc, acc_sc, n_chains, dqk):
    qz_sc[...] = jnp.zeros(qz_sc.shape, qz_sc.dtype)
    for c in range(n_chains):
        qz_sc[c, (c * dqk) % LANES:(c * dqk) % LANES + dqk, :] = qt_ref[c * dqk:(c + 1) * dqk, :]
    m_sc[...] = jnp.full(m_sc.shape, -jnp.inf, _F32)
    acc_sc[...] = jnp.zeros(acc_sc.shape, _F32)


def _softmax_steps(tiles, *, qz_sc, m_sc, acc_sc, n_chains, n_maps, dqk):
    steps = [(ti, c) for ti in range(len(tiles)) for c in range(n_chains)]
    kts, vts = {}, {}

    def scores(ti, c):
        if ti not in kts:
            kts[ti] = tiles[ti][0]().astype(_MM)
        grp = (c * dqk) // LANES
        s = jnp.dot(kts[ti][:, grp * LANES:(grp + 1) * LANES], qz_sc[c], preferred_element_type=_F32)
        return s if tiles[ti][2] is None else s + tiles[ti][2](c)

    def values(ti, h):
        if (ti, h) not in vts:
            vt = tiles[ti][1](h).astype(_MM)
            vts[ti, h] = jnp.concatenate([vt, jnp.ones((SUM_ROWS, vt.shape[1]), _MM)], axis=0)
        return vts[ti, h]

    pending = [scores(*st) for st in steps[:SCORE_LOOKAHEAD]]
    for idx, (ti, c) in enumerate(steps):
        s = pending.pop(0)
        if idx + SCORE_LOOKAHEAD < len(steps):
            pending.append(scores(*steps[idx + SCORE_LOOKAHEAD]))
        m_prev = m_sc[c]
        m_new = jnp.maximum(m_prev, jnp.max(s, axis=0, keepdims=True))
        alpha = jnp.exp2(m_prev - m_new)
        p = jnp.exp2(s - m_new).astype(_MM)
        acc_sc[c] = alpha * acc_sc[c] + jnp.dot(values(ti, c // n_maps), p, preferred_element_type=_F32)
        m_sc[c] = m_new


def _attn_kernel(*refs, n_heads, n_maps, dqk, dv, tk, n_own, unroll, tkc, n_cache, diff, lam_init):
    qt_ref, k_ref, vt_ref = refs[:3]
    refs = refs[3:]
    if n_cache:
        kc_ref, vct_ref = refs[:2]
        refs = refs[2:]
    if diff:
        alam_ref, subg_ref = refs[:2]
        refs = refs[2:]
    o_ref, qz_sc, m_sc, acc_sc, ot_sc = refs
    n_chains = n_heads * n_maps
    _softmax_init(qt_ref, qz_sc, m_sc, acc_sc, n_chains, dqk)
    update = functools.partial(_softmax_steps, qz_sc=qz_sc, m_sc=m_sc, acc_sc=acc_sc, n_chains=n_chains,
                               n_maps=n_maps, dqk=dqk)

    cache_tiles = [(lambda j=j: kc_ref[j * tkc:(j + 1) * tkc, :],
                    lambda h, j=j: vct_ref[h * dv:(h + 1) * dv, j * tkc:(j + 1) * tkc], None) for j in range(n_cache)]

    def own_tiles(starts):
        return [(lambda st=st: k_ref[pl.ds(st, tk), :],
                 lambda h, st=st: vt_ref[h * dv:(h + 1) * dv, pl.ds(st, tk)], None) for st in starts]

    if unroll == n_own:
        update(cache_tiles + own_tiles([u * tk for u in range(n_own)]))
    else:
        if n_cache:
            update(cache_tiles)

        def body(j, carry):
            update(own_tiles([pl.multiple_of((j * unroll + u) * tk, tk) for u in range(unroll)]))
            return carry

        lax.fori_loop(0, n_own // unroll, body, 0)

    def normalised(c):
        a = acc_sc[c]
        return a[0:dv] / a[dv:dv + 1]

    if diff:
        lv = alam_ref[...]
        lam = (jnp.exp(jnp.sum(lv[0:1] * lv[1:2], axis=-1, keepdims=True))
               - jnp.exp(jnp.sum(lv[2:3] * lv[3:4], axis=-1, keepdims=True)) + lam_init)
    for h in range(n_heads):
        if diff:
            o = normalised(2 * h) - lam * normalised(2 * h + 1)
            o = o * lax.rsqrt(jnp.mean(o * o, axis=0, keepdims=True) + EPS) * subg_ref[...] * (1.0 - lam_init)
        else:
            o = normalised(h)
        ot_sc[h * dv:(h + 1) * dv, :] = o
    o_ref[...] = ot_sc[...].T.astype(o_ref.dtype)


def _attn(qt, k, vt, cache, diff_args, *, n_heads, n_maps, dqk, dv, lam_init=0.0, name):
    nb, s, _ = k.shape
    tq = min(ATTN_TILE, s)
    tk = min(ATTN_TILE, s)
    n_own = s // tk
    qmap = lambda b, i: (b, 0, i)
    bmap = lambda b, i: (b, 0, 0)
    in_specs = [pl.BlockSpec((None, HEAD_W, tq), qmap),
                pl.BlockSpec((None, s, HEAD_W), bmap),
                pl.BlockSpec((None, HEAD_W, s), bmap)]
    args = [qt, k, vt]
    tkc, n_cache = tk, 0
    if cache is not None:
        past = cache[0].shape[1]
        tkc = min(ATTN_TILE, past)
        n_cache = past // tkc
        in_specs += [pl.BlockSpec((None, past, HEAD_W), bmap), pl.BlockSpec((None, HEAD_W, past), bmap)]
        args += list(cache)
    if diff_args is not None:
        in_specs += [pl.BlockSpec(diff_args[0].shape, lambda b, i: (0, 0)),
                     pl.BlockSpec(diff_args[1].shape, lambda b, i: (0, 0))]
        args += list(diff_args)
    n_chains = n_heads * n_maps
    unroll = n_own if n_own <= MAX_UNROLLED_KEY_TILES else (8 if n_own % 8 == 0 else 1)
    kern = functools.partial(_attn_kernel, n_heads=n_heads, n_maps=n_maps, dqk=dqk, dv=dv, tk=tk, n_own=n_own,
                             unroll=unroll, tkc=tkc, n_cache=n_cache, diff=diff_args is not None, lam_init=lam_init)
    return pl.pallas_call(
        kern, grid=(nb, s // tq), in_specs=in_specs,
        out_specs=pl.BlockSpec((None, tq, HEAD_W), lambda b, i: (b, i, 0)),
        out_shape=jax.ShapeDtypeStruct((nb, s, HEAD_W), _MM),
        scratch_shapes=[pltpu.VMEM((n_chains, LANES, tq), _MM), pltpu.VMEM((n_chains, 1, tq), _F32),
                        pltpu.VMEM((n_chains, dv + SUM_ROWS, tq), _F32), pltpu.VMEM((n_heads * dv, tq), _F32)],
        compiler_params=_cparams(("arbitrary", "arbitrary"), 48),
        name=name,
    )(*args)


NBR_ROWS = 4
NBR_SPAN_ROWS = 12


def _nbr_kernel(qt_ref, k_ref, vt_ref, kc_ref, vct_ref, tb_ref, o_ref, qz_sc, m_sc, acc_sc, ot_sc, *, rows, tkc, n_cache):
    i = pl.program_id(1)
    ss = jnp.clip(i * NBR_ROWS - WIN_R // 2, 0, rows - NBR_SPAN_ROWS)
    tk = NBR_ROWS * GRID_W
    kstart = pl.multiple_of(ss * GRID_W, tk)
    hsl = lambda h: slice(h * DH_B, (h + 1) * DH_B)

    _softmax_init(qt_ref, qz_sc, m_sc, acc_sc, H_B, DH_B)
    tiles = [(lambda j=j: kc_ref[j * tkc:(j + 1) * tkc, :],
              lambda h, j=j: vct_ref[hsl(h), j * tkc:(j + 1) * tkc], None) for j in range(n_cache)]
    starts = [pl.multiple_of(kstart + u * tk, tk) for u in range(NBR_SPAN_ROWS // NBR_ROWS)]
    tiles += [(lambda st=st: k_ref[pl.ds(st, tk), :],
               lambda h, st=st: vt_ref[hsl(h), pl.ds(st, tk)],
               lambda c, u=u: tb_ref[c, u * tk:(u + 1) * tk, :]) for u, st in enumerate(starts)]
    _softmax_steps(tiles, qz_sc=qz_sc, m_sc=m_sc, acc_sc=acc_sc, n_chains=H_B, n_maps=1, dqk=DH_B)
    for h in range(H_B):
        a = acc_sc[h]
        ot_sc[hsl(h), :] = a[0:DH_B] / a[DH_B:DH_B + 1]
    o_ref[...] = ot_sc[...].T.astype(o_ref.dtype)


def _nbr_bias_table(rpb, rows):
    nh, n_r, n_c = rpb.shape
    kc = np.arange(GRID_W)[:, None]
    c = np.arange(GRID_W)[None, :]
    cidx = np.clip(kc - c + WIN_C - 1, 0, n_c - 1).reshape(1, -1)
    onehot = jnp.asarray((cidx == np.arange(n_c)[:, None]).astype(np.float32))
    toep = jnp.dot(rpb.reshape(nh * n_r, n_c).astype(_F32), onehot, precision=lax.Precision.HIGHEST)
    toep = toep.reshape(nh, n_r, GRID_W, GRID_W)
    cstart = np.clip(c - WIN_C // 2, 0, GRID_W - WIN_C)
    valid = (kc >= cstart) & (kc < cstart + WIN_C)
    toep = jnp.where(valid[None, None], toep, -jnp.inf) * LOG2E
    pad = jnp.full((nh, NBR_SPAN_ROWS, GRID_W, GRID_W), -jnp.inf, _F32)
    toep = jnp.concatenate([pad, toep, pad], axis=1)
    assert rows % NBR_ROWS == 0 and rows >= NBR_SPAN_ROWS
    variants = []
    for r0 in (0, NBR_ROWS, rows - NBR_ROWS):
        ss = int(np.clip(r0 - WIN_R // 2, 0, rows - NBR_SPAN_ROWS))
        key_row = ss + np.arange(NBR_SPAN_ROWS)
        per_query_row = []
        for r in range(r0, r0 + NBR_ROWS):
            rs = int(np.clip(r - WIN_R // 2, 0, rows - WIN_R))
            start = ss - r + WIN_R - 1 + NBR_SPAN_ROWS
            in_rows = (key_row >= rs) & (key_row < rs + WIN_R)
            slab = toep[:, start:start + NBR_SPAN_ROWS]
            per_query_row.append(jnp.where(in_rows[None, :, None, None], slab, -jnp.inf))
        tbv = jnp.stack(per_query_row, axis=3)
        variants.append(tbv.reshape(nh, NBR_SPAN_ROWS * GRID_W, NBR_ROWS * GRID_W))
    return jnp.stack(variants, axis=0)


def _nbr(qt, k, vt, kc, vct, tb):
    nb, s, _ = k.shape
    rows = s // GRID_W
    n_steps = rows // NBR_ROWS
    past = kc.shape[1]
    tq = NBR_ROWS * GRID_W
    bmap = lambda b, i: (b, 0, 0)
    variant = lambda b, i: (jnp.where(i == 0, 0, jnp.where(i == n_steps - 1, 2, 1)), 0, 0, 0)
    tkc = min(ATTN_TILE, past)
    return pl.pallas_call(
        functools.partial(_nbr_kernel, rows=rows, tkc=tkc, n_cache=past // tkc), grid=(nb, n_steps),
        in_specs=[pl.BlockSpec((None, HEAD_W, tq), lambda b, i: (b, 0, i)),
                  pl.BlockSpec((None, s, HEAD_W), bmap),
                  pl.BlockSpec((None, HEAD_W, s), bmap),
                  pl.BlockSpec((None, past, HEAD_W), bmap),
                  pl.BlockSpec((None, HEAD_W, past), bmap),
                  pl.BlockSpec((None,) + tb.shape[1:], variant)],
        out_specs=pl.BlockSpec((None, tq, HEAD_W), lambda b, i: (b, i, 0)),
        out_shape=jax.ShapeDtypeStruct((nb, s, HEAD_W), _MM),
        scratch_shapes=[pltpu.VMEM((H_B, LANES, tq), _MM), pltpu.VMEM((H_B, 1, tq), _F32),
                        pltpu.VMEM((H_B, DH_B + SUM_ROWS, tq), _F32), pltpu.VMEM((HEAD_W, tq), _F32)],
        compiler_params=_cparams(("arbitrary", "arbitrary"), 48),
        name="nbr_attn",
    )(qt, k, vt, kc, vct, tb)


def _mla_cache_kernel(ckv_ref, kr_ref, wkc_ref, wvt_ref, e_ref, kc_ref, vct_ref):
    ckv = ckv_ref[...]
    kc_ref[...] = (_dot(ckv, wkc_ref[...]) + _dot(kr_ref[...], e_ref[...])).astype(kc_ref.dtype)
    vct_ref[...] = _dot_nt(wvt_ref[...], ckv).astype(vct_ref.dtype)


def _mla_cache(ckv, kr, wkc, wvt, expand):
    nb, past, _ = ckv.shape
    bmap = lambda b: (b, 0, 0)
    const = lambda b: (0, 0)
    return pl.pallas_call(
        _mla_cache_kernel, grid=(nb,),
        in_specs=[pl.BlockSpec((None, past, KV_LORA), bmap), pl.BlockSpec((None, past, ROPE_C), bmap),
                  pl.BlockSpec(wkc.shape, const), pl.BlockSpec(wvt.shape, const), pl.BlockSpec(expand.shape, const)],
        out_specs=[pl.BlockSpec((None, past, HEAD_W), bmap), pl.BlockSpec((None, HEAD_W, past), bmap)],
        out_shape=[jax.ShapeDtypeStruct((nb, past, HEAD_W), _MM), jax.ShapeDtypeStruct((nb, HEAD_W, past), _MM)],
        compiler_params=_cparams(("arbitrary",), 32),
        name="mla_cache",
    )(ckv, kr, wkc, wvt, expand)


CONV_HALO = 16


def _conv_kernel(z_ref, w_ref, b_ref, g_ref, beta_ref, o_ref, zp_ref, *, seq, rc):
    zp_ref[0:CONV_HALO, :] = jnp.zeros((CONV_HALO, CONV_CH), _F32)
    zp_ref[CONV_HALO:CONV_HALO + seq, :] = z_ref[...]
    zp_ref[CONV_HALO + seq:2 * CONV_HALO + seq, :] = jnp.zeros((CONV_HALO, CONV_CH), _F32)
    w = w_ref[...]
    shift = CONV_HALO - CONV_K // 2

    def chunk(c, carry):
        base = pl.multiple_of(c * rc, rc)
        n_win = rc + 2 * CONV_HALO
        win = zp_ref[pl.ds(base, n_win), :]
        acc = jnp.zeros((rc, CONV_CH), _F32)
        for phase in range(SUBLANES):
            rolled = win if phase == 0 else pltpu.roll(win, n_win - phase, axis=0)
            for j in range(CONV_K):
                if (j + shift) % SUBLANES == phase:
                    a = (j + shift) - phase
                    acc = acc + rolled[a:a + rc, :] * w[j:j + 1, :]
        y = acc + b_ref[...]
        mu = jnp.mean(y, axis=-1, keepdims=True)
        yc = y - mu
        var = jnp.mean(yc * yc, axis=-1, keepdims=True)
        yn = yc * lax.rsqrt(var + EPS) * g_ref[...] + beta_ref[...]
        o_ref[pl.ds(base, rc), :] = (yn * _sigmoid(yn)).astype(o_ref.dtype)
        return carry

    lax.fori_loop(0, seq // rc, chunk, 0)


def _conv(z, w, b, g, beta):
    nb, seq, _ = z.shape
    rc = min(CONV_CHUNK, seq)
    bmap = lambda i: (i, 0, 0)
    const = lambda i: (0, 0)
    kern = functools.partial(_conv_kernel, seq=seq, rc=rc)
    return pl.pallas_call(
        kern, grid=(nb,),
        in_specs=[pl.BlockSpec((None, seq, CONV_CH), bmap), pl.BlockSpec(w.shape, const),
                  pl.BlockSpec(b.shape, const), pl.BlockSpec(g.shape, const), pl.BlockSpec(beta.shape, const)],
        out_specs=pl.BlockSpec((None, seq, CONV_CH), bmap),
        out_shape=jax.ShapeDtypeStruct((nb, seq, CONV_CH), _MM),
        scratch_shapes=[pltpu.VMEM((seq + 2 * CONV_HALO, CONV_CH), _F32)],
        compiler_params=_cparams(("arbitrary",), 40),
        name="conv",
    )(z, w, b, g, beta)


POST_ROW_PARTS = 2

def _post_kernel(x_ref, mod_ref, oa_ref, ob_ref, oc_ref, od_ref, g1_ref, wg_ref, bg_ref,
                 awo_ref, bwo_ref, cwo_ref, dwo_ref, wout_ref, g2_ref, rw_ref, rb_ref,
                 x1_ref, h2_ref, dw_ref, *, d):
    mod = mod_ref[0]
    shift1, scale1, gate1 = mod[:, 0:d], mod[:, d:2 * d], mod[:, 2 * d:3 * d]
    shift2, scale2 = mod[:, 3 * d:4 * d], mod[:, 4 * d:5 * d]
    branches = ((oa_ref, awo_ref), (ob_ref, bwo_ref), (oc_ref, cwo_ref), (od_ref, dwo_ref))
    tm = x_ref.shape[0]
    parts = [slice(k * tm // POST_ROW_PARTS, (k + 1) * tm // POST_ROW_PARTS) for k in range(POST_ROW_PARTS)]

    xs, branch_mm = [], []
    for rs in parts:
        x = x_ref[rs, :]
        h = (_rms(x, g1_ref[...]) * (1.0 + scale1) + shift1).astype(_MM)
        xs.append(x)
        branch_mm.append([(_dot(h, wg_ref[:, bi * d:(bi + 1) * d]), _dot(o_ref[rs, :], wo_ref[...]))
                          for bi, (o_ref, wo_ref) in enumerate(branches)])
    mixed = []
    for pairs in branch_mm:
        merged = None
        for bi, (gate_logits, out) in enumerate(pairs):
            term = _sigmoid(gate_logits + bg_ref[:, bi * d:(bi + 1) * d]) * out
            merged = term if merged is None else merged + term
        mixed.append(_dot(merged, wout_ref[...]))
    logits = []
    for rs, x, y in zip(parts, xs, mixed):
        x1 = x + gate1 * y
        x1_ref[rs, :] = x1
        h2_mm = (_rms(x1, g2_ref[...]) * (1.0 + scale2) + shift2).astype(_MM)
        h2_ref[rs, :] = h2_mm
        logits.append(_dot(h2_mm, rw_ref[...]) + rb_ref[...])
    for rs, r in zip(parts, logits):
        lgp, lep = r[:, 0:LANES], r[:, LANES:2 * LANES]
        lane = lax.broadcasted_iota(jnp.int32, lgp.shape, 1)
        neg = -jnp.inf
        lg = jnp.where(lane < N_GROUPS, lgp, neg)
        mg = jnp.max(lg, axis=-1, keepdims=True)
        gi = jnp.min(jnp.where(lg == mg, lane, LANES), axis=-1, keepdims=True)
        gw = 1.0 / jnp.sum(jnp.exp(lg - mg), axis=-1, keepdims=True)
        in_group = (lane < N_EXPERTS) & ((lane // EXPERTS_PER_GROUP) == gi)
        le = jnp.where(in_group, lep, neg)
        t1 = jnp.max(le, axis=-1, keepdims=True)
        i1 = jnp.min(jnp.where(le == t1, lane, LANES), axis=-1, keepdims=True)
        le2 = jnp.where(lane == i1, neg, le)
        t2 = jnp.max(le2, axis=-1, keepdims=True)
        i2 = jnp.min(jnp.where(le2 == t2, lane, LANES), axis=-1, keepdims=True)
        e2 = jnp.exp(t2 - t1)
        w1 = gw / (1.0 + e2)
        w2 = gw * e2 / (1.0 + e2)
        dw_ref[rs, :] = jnp.where(lane == i1, w1, 0.0) + jnp.where(lane == i2, w2, 0.0)


def _post(x2d, mods3, mod_row, oa, ob, oc, od, lw):
    t, d = x2d.shape
    tm = min(POST_TILE, t)
    row = lambda i: (i, 0)
    const = lambda i: (0, 0)
    wnames = ("g1", "wgate", "bgate", "awo", "bwo", "cwo", "dwo", "wout", "g2", "rw", "rb")
    in_specs = [pl.BlockSpec((tm, d), row), pl.BlockSpec((1, 1, 6 * d), lambda i: (mod_row(i * tm), 0, 0))]
    in_specs += [pl.BlockSpec((tm, HEAD_W), row)] * 4
    in_specs += [pl.BlockSpec(lw[n].shape, const) for n in wnames]
    args = [x2d, mods3, oa, ob, oc, od] + [lw[n] for n in wnames]
    return pl.pallas_call(
        functools.partial(_post_kernel, d=d), grid=(t // tm,), in_specs=in_specs,
        out_specs=[pl.BlockSpec((tm, d), row), pl.BlockSpec((tm, d), row), pl.BlockSpec((tm, LANES), row)],
        out_shape=[jax.ShapeDtypeStruct((t, d), _F32), jax.ShapeDtypeStruct((t, d), _MM),
                   jax.ShapeDtypeStruct((t, LANES), _F32)],
        compiler_params=_cparams(("arbitrary",), 56),
        name="post",
    )(*args)


def _moe_kernel(h2_ref, dw_ref, x1_ref, mod_ref, w1_ref, w2_ref, fg_ref, o_ref, acc_ref, *, d, final):
    g = pl.program_id(1)

    @pl.when(g == 0)
    def _():
        acc_ref[...] = jnp.zeros_like(acc_ref)

    h2 = h2_ref[...]
    dw = dw_ref[...]
    lane = lax.broadcasted_iota(jnp.int32, dw.shape, 1)
    acts = []
    for j in range(EXPERTS_PER_GROUP):
        hid = _dot(h2, w1_ref[j])
        gate, up = hid[:, 0:D_EXPERT], hid[:, D_EXPERT:2 * D_EXPERT]
        we = jnp.sum(jnp.where(lane == g * EXPERTS_PER_GROUP + j, dw, 0.0), axis=-1, keepdims=True)
        acts.append((gate * _sigmoid(gate) * up * we).astype(_MM))
    acc_ref[...] += _dot(jnp.concatenate(acts, axis=-1), w2_ref[...])

    @pl.when(g == pl.num_programs(1) - 1)
    def _():
        gate2 = mod_ref[0][:, 5 * d:6 * d]
        x2 = x1_ref[...] + gate2 * acc_ref[...]
        if final:
            x2 = _rms(x2, fg_ref[...])
        o_ref[...] = x2


def _moe(h2, dw, x1, mods3, mod_row, w1, w2, fg, *, final):
    t, d = x1.shape
    tm = min(MOE_TILE, t)
    n_e, _, d_hid = w1.shape
    w2g = w2.reshape(N_GROUPS, EXPERTS_PER_GROUP * D_EXPERT, d)
    row = lambda i, g: (i, 0)
    return pl.pallas_call(
        functools.partial(_moe_kernel, d=d, final=final), grid=(t // tm, N_GROUPS),
        in_specs=[pl.BlockSpec((tm, d), row), pl.BlockSpec((tm, LANES), row), pl.BlockSpec((tm, d), row),
                  pl.BlockSpec((1, 1, 6 * d), lambda i, g: (mod_row(i * tm), 0, 0)),
                  pl.BlockSpec((EXPERTS_PER_GROUP, d, d_hid), lambda i, g: (g, 0, 0)),
                  pl.BlockSpec((None,) + w2g.shape[1:], lambda i, g: (g, 0, 0)),
                  pl.BlockSpec((1, d), lambda i, g: (0, 0))],
        out_specs=pl.BlockSpec((tm, d), row),
        out_shape=jax.ShapeDtypeStruct((t, d), _F32),
        scratch_shapes=[pltpu.VMEM((tm, d), _F32)],
        compiler_params=_cparams(("arbitrary", "arbitrary"), 56),
        name="moe_final" if final else "moe",
    )(h2, dw, x1, mods3, w1, w2g, fg)


def _rope_tables(n_tok, dim):
    hq = dim // 4
    tpos = np.arange(n_tok)
    rows = (tpos // GRID_W).astype(np.float32)
    cols = (tpos % GRID_W).astype(np.float32)
    inv = jnp.asarray(ROPE_THETA, _F32) ** (-jnp.arange(hq, dtype=_F32) / hq)
    ar = jnp.asarray(rows)[:, None] * inv[None, :]
    ac = jnp.asarray(cols)[:, None] * inv[None, :]
    ang = jnp.concatenate([ar, ar, ac, ac], axis=-1)
    return jnp.cos(ang), jnp.sin(ang)


def _layer_weights(l, p, grid_rows):
    mm = lambda a: a.astype(_MM)
    w_in = p["w_in"][l]
    pts = np.cumsum([0, 2 * H_A * DH_A, 2 * H_A * DH_A, 2 * H_A * DH_A, H_B * DH_B, H_B * DH_B, H_B * DH_B,
                     Q_LORA, KV_LORA, ROPE_C, 2 * CONV_CH])
    seg = lambda i, j: w_in[:, pts[i]:pts[j]]
    d = w_in.shape[0]
    wckr = seg(8, 9)
    zeros_n = jnp.zeros((d, NOPE_C), w_in.dtype)
    wckr_t = jnp.concatenate([zeros_n, wckr] * H_C, axis=1)
    wukv = p["c_wukv"][l].reshape(KV_LORA, H_C, NOPE_C + DV_C)
    wkc = jnp.concatenate([wukv[:, :, :NOPE_C], jnp.zeros((KV_LORA, H_C, ROPE_C), wukv.dtype)], axis=-1)
    wkc = wkc.reshape(KV_LORA, H_C * (NOPE_C + ROPE_C))
    wv = wukv[:, :, NOPE_C:].reshape(KV_LORA, H_C * DV_C)
    lane_pad = lambda a, n: jnp.pad(a, ((0, 0), (0, LANES - n)))
    router = jnp.concatenate([lane_pad(p["moe_w_group"][l], N_GROUPS), lane_pad(p["moe_w_expert"][l], N_EXPERTS)], axis=1)
    rb = jnp.concatenate([lane_pad(p["moe_b_group"][l][None], N_GROUPS),
                          lane_pad(p["moe_b_expert"][l][None], N_EXPERTS)], axis=1)
    conv_w = p["d_conv_w"][l]
    return dict(
        g1=p["norm1_g"][l][None], g2=p["norm2_g"][l][None],
        wa=mm(seg(0, 3)), wb=mm(seg(3, 6)), wcq=mm(seg(6, 7)), wckv=mm(seg(7, 8)), wckr=mm(wckr_t), wdu=mm(seg(9, 10)),
        cqg=p["c_q_norm_g"][l][None], ckvg=p["c_kv_norm_g"][l][None],
        wuq=mm(p["c_wuq"][l]), wkc=mm(wkc), wv=mm(wv), wvt=mm(wv.T),
        alam=p["a_lambda"][l], subg=p["a_subln_g"][l][:, None],
        wgate=mm(p["w_gate"][l]), bgate=p["b_gate"][l][None],
        awo=mm(p["a_wo"][l]), bwo=mm(p["b_wo"][l]), cwo=mm(p["c_wo"][l]), dwo=mm(p["d_wo"][l]),
        wout=mm(p["w_out"][l]), rw=mm(router), rb=rb,
        conv_w=conv_w, conv_b=p["d_conv_b"][l][None], ln_g=p["d_ln_g"][l][None], ln_b=p["d_ln_b"][l][None],
        w1=mm(p["moe_w1"][l]), w2=mm(p["moe_w2"][l]),
        tb=_nbr_bias_table(p["b_rpb"][l], grid_rows),
    )


def _layer(x2d, mods3, mod_row, lw, l, n_layers, *, nb, seq, fg, ctx, tables):
    latent = ctx is not None
    t = x2d.shape[0]
    outs = _pre(x2d, mods3, mod_row, lw, latent=latent, seq=seq, tables=tables)
    r3 = lambda a: a.reshape(nb, seq, a.shape[-1])
    lam_init = 0.8 - 0.6 * math.exp(-0.3 * l)
    diff_args = (lw["alam"], lw["subg"])
    if latent:
        aqt, ak, avt, bqt, bk, bvt, qct, kc, vct, z = outs
        own = None
        a_k_c, a_vt_c, b_k_c, b_vt_c, c_k_c, c_vt_c = ctx
        oa = _attn(aqt, r3(ak), avt, (a_k_c, a_vt_c), diff_args, n_heads=H_A, n_maps=2,
                   dqk=DH_A, dv=2 * DH_A, lam_init=lam_init, name="attn_a_lat")
        ob = _nbr(bqt, r3(bk), bvt, b_k_c, b_vt_c, lw["tb"])
        oc = _attn(qct, r3(kc), vct, (c_k_c, c_vt_c), None, n_heads=H_C, n_maps=1,
                   dqk=NOPE_C + ROPE_C, dv=DV_C, name="attn_c_lat")
    else:
        aqt, ak, av, avt, bqt, bk, bv, bvt, qct, kc, vct, ckvn, ckr, z = outs
        own = (ak, av, bk, bv, ckvn, ckr)
        oa = _attn(aqt, r3(ak), avt, None, diff_args, n_heads=H_A, n_maps=2,
                   dqk=DH_A, dv=2 * DH_A, lam_init=lam_init, name="attn_a_ctx")
        ob = _attn(bqt, r3(bk), bvt, None, None, n_heads=H_B, n_maps=1, dqk=DH_B, dv=DH_B, name="attn_b_ctx")
        oc = _attn(qct, r3(kc), vct, None, None, n_heads=H_C, n_maps=1,
                   dqk=NOPE_C + ROPE_C, dv=DV_C, name="attn_c_ctx")
    od = _conv(r3(z), lw["conv_w"], lw["conv_b"], lw["ln_g"], lw["ln_b"])
    f2 = lambda a: a.reshape(t, a.shape[-1])
    x1, h2, dw = _post(x2d, mods3, mod_row, f2(oa), f2(ob), f2(oc), f2(od), lw)
    x2 = _moe(h2, dw, x1, mods3, mod_row, lw["w1"], lw["w2"], fg, final=(l == n_layers - 1))
    return x2, own


def kernel(x_prompt, x_sample, cache_a_k, cache_a_v, cache_b_k, cache_b_v, cache_c_kv, cache_c_krope, c, c_ctx, norm1_g, norm2_g, w_ada, b_ada, w_in, a_lambda, a_subln_g, a_wo, b_rpb, b_wo, c_q_norm_g, c_kv_norm_g, c_wuq, c_wukv, c_wo, d_conv_w, d_conv_b, d_ln_g, d_ln_b, d_wo, w_gate, b_gate, w_out, moe_w_group, moe_b_group, moe_w_expert, moe_b_expert, moe_w1, moe_w2, final_norm_g):
    p = dict(norm1_g=norm1_g, norm2_g=norm2_g, w_in=w_in, a_lambda=a_lambda, a_subln_g=a_subln_g, a_wo=a_wo,
             b_rpb=b_rpb, b_wo=b_wo, c_q_norm_g=c_q_norm_g, c_kv_norm_g=c_kv_norm_g, c_wuq=c_wuq, c_wukv=c_wukv,
             c_wo=c_wo, d_conv_w=d_conv_w, d_conv_b=d_conv_b, d_ln_g=d_ln_g, d_ln_b=d_ln_b, d_wo=d_wo,
             w_gate=w_gate, b_gate=b_gate, w_out=w_out, moe_w_group=moe_w_group, moe_b_group=moe_b_group,
             moe_w_expert=moe_w_expert, moe_b_expert=moe_b_expert, moe_w1=moe_w1, moe_w2=moe_w2)
    n_layers = w_in.shape[0]
    bc, sc, d = x_prompt.shape
    bl, sl, _ = x_sample.shape
    past = cache_a_k.shape[2]
    assert bl + 1 <= COND_ROWS and sl % GRID_W == 0

    cond8 = jnp.zeros((COND_ROWS, d), _F32).at[0].set(c_ctx).at[1:1 + bl].set(c)
    mods3 = _adaln(cond8, w_ada, b_ada).reshape(n_layers * COND_ROWS, 1, 6 * d)
    fg = final_norm_g[None]

    cos_a, sin_a = _rope_tables(sl, DH_A)
    cos_c, sin_c = _rope_tables(sl, ROPE_C)
    reps_a = HEAD_W // DH_A
    ones_n, zeros_n = jnp.ones((sl, NOPE_C), _F32), jnp.zeros((sl, NOPE_C), _F32)
    tables = (jnp.tile(cos_a, (1, reps_a)), jnp.tile(sin_a, (1, reps_a)),
              jnp.concatenate([ones_n, cos_c] * H_C, axis=1), jnp.concatenate([zeros_n, sin_c] * H_C, axis=1))
    expand = np.zeros((ROPE_C, HEAD_W), np.float32)
    for hh in range(H_C):
        expand[np.arange(ROPE_C), hh * (NOPE_C + ROPE_C) + NOPE_C + np.arange(ROPE_C)] = 1.0
    expand = jnp.asarray(expand, _MM)

    lws = [_layer_weights(l, p, sl // GRID_W) for l in range(n_layers)]

    xp = x_prompt.reshape(bc * sc, d)
    states = []
    for l in range(n_layers):
        xp, own = _layer(xp, mods3, (lambda tok, l=l: l * COND_ROWS), lws[l], l, n_layers, nb=bc, seq=sc, fg=fg, ctx=None,
                         tables=None)
        states.append(own)
    y_prompt = xp.reshape(bc, sc, d)

    xs = x_sample.reshape(bl * sl, d)
    for l in range(n_layers):
        flat = lambda a: a[:, l].reshape(bl, past, -1)
        c_k_c, c_vt_c = _mla_cache(flat(cache_c_kv), flat(cache_c_krope), lws[l]["wkc"], lws[l]["wvt"], expand)
        ctx = (flat(cache_a_k).astype(_MM), jnp.swapaxes(flat(cache_a_v), 1, 2).astype(_MM),
               flat(cache_b_k).astype(_MM), jnp.swapaxes(flat(cache_b_v), 1, 2).astype(_MM), c_k_c, c_vt_c)
        xs, _ = _layer(xs, mods3, (lambda tok, l=l: l * COND_ROWS + 1 + tok // sl), lws[l], l, n_layers,
                       nb=bl, seq=sl, fg=fg, ctx=ctx, tables=tables)
    y_sample = xs.reshape(bl, sl, d)

    st = lambda k, tail: jnp.stack([s[k].reshape(bc, sc, -1) for s in states], axis=1).reshape((bc, n_layers, sc) + tail)
    new_a_k = st(0, (H_A, 2 * DH_A))
    new_a_v = st(1, (H_A, 2 * DH_A))
    new_b_k = st(2, (H_B, DH_B))
    new_b_v = st(3, (H_B, DH_B))
    new_c_kv = st(4, (KV_LORA,))
    new_c_krope = st(5, (ROPE_C,))
    return (y_prompt, y_sample, new_a_k, new_a_v, new_b_k, new_b_v, new_c_kv, new_c_krope)
```

```python
import functools
import math

import numpy as np
import jax
import jax.numpy as jnp
from jax import lax
from jax.experimental import pallas as pl
from jax.experimental.pallas import tpu as pltpu

GRID_W = 64
EPS = 1e-6
ROPE_THETA = 10000.0
H_A, DH_A = 4, 32
H_B, DH_B = 4, 64
WIN_R, WIN_C = 8, 16
H_C, Q_LORA, KV_LORA, NOPE_C, ROPE_C, DV_C = 4, 256, 128, 32, 32, 64
CONV_CH, CONV_K = 256, 31
N_BRANCH = 4
N_GROUPS, EXPERTS_PER_GROUP, N_EXPERTS, D_EXPERT = 4, 4, 16, 256
HEAD_W = 256
LANES = 128
SUBLANES = 8
V7X_VMEM_BYTES = 64 * 1024 * 1024
LOG2E = math.log2(math.e)

PRE_TILE = 512
POST_TILE = 512
MOE_TILE = 1024
ATTN_TILE = 256
CONV_CHUNK = 128
ADALN_COLS = 1536
COND_ROWS = SUBLANES

_MM = jnp.bfloat16
_F32 = jnp.float32


def _cparams(sem, vmem_mb):
    return pltpu.CompilerParams(dimension_semantics=sem,
                                vmem_limit_bytes=min(vmem_mb * 1024 * 1024, V7X_VMEM_BYTES - 8 * 1024 * 1024))


def _dot(a, b):
    return jnp.dot(a.astype(_MM), b.astype(_MM), preferred_element_type=_F32)


def _dot_nt(a, b):
    return lax.dot_general(a.astype(_MM), b.astype(_MM), (((1,), (1,)), ((), ())),
                           preferred_element_type=_F32)


def _sigmoid(x):
    return 1.0 / (1.0 + jnp.exp(-x))


def _rms(x, g):
    return x * lax.rsqrt(jnp.mean(x * x, axis=-1, keepdims=True) + EPS) * g


def _rope(x, cos, sin):
    w = x.shape[-1]
    lane = lax.broadcasted_iota(jnp.int32, x.shape, 1)
    first = (lane % 16) < 8
    fwd = pltpu.roll(x, w - 8, axis=1)
    bwd = pltpu.roll(x, 8, axis=1)
    return x * cos + jnp.where(first, -fwd, bwd) * sin


def _adaln_kernel(cond_ref, w_ref, b_ref, o_ref):
    c = cond_ref[...]
    a = c * _sigmoid(c)
    o_ref[0] = _dot(a, w_ref[0]) + b_ref[0]


def _adaln(cond8, w_ada, b_ada):
    n_l, d, d6 = w_ada.shape
    tn = ADALN_COLS
    return pl.pallas_call(
        _adaln_kernel,
        grid=(n_l, d6 // tn),
        in_specs=[pl.BlockSpec((COND_ROWS, d), lambda l, j: (0, 0)),
                  pl.BlockSpec((1, d, tn), lambda l, j: (l, 0, j)),
                  pl.BlockSpec((1, 1, tn), lambda l, j: (l, 0, j))],
        out_specs=pl.BlockSpec((1, COND_ROWS, tn), lambda l, j: (l, 0, j)),
        out_shape=jax.ShapeDtypeStruct((n_l, COND_ROWS, d6), _F32),
        compiler_params=_cparams(("arbitrary", "arbitrary"), 40),
        name="adaln",
    )(cond8, w_ada, b_ada.reshape(n_l, 1, d6))


PRE_ROW_PARTS = 2


def _pre_kernel(*refs, d, latent, sa, sb, sc):
    (x_ref, mod_ref, g1_ref, wa_ref, wb_ref, wcq_ref, wckv_ref, wckr_ref, wdu_ref,
     cqg_ref, ckvg_ref, wuq_ref, wkc_ref, wv_ref) = refs[:14]
    refs = refs[14:]
    if latent:
        cosa_ref, sina_ref, cosc_ref, sinc_ref = refs[:4]
        refs = refs[4:]
        aqt_ref, ak_ref, avt_ref, bqt_ref, bk_ref, bvt_ref, qct_ref, kc_ref, vct_ref, z_ref = refs
    else:
        (aqt_ref, ak_ref, av_ref, avt_ref, bqt_ref, bk_ref, bv_ref, bvt_ref, qct_ref, kc_ref, vct_ref,
         ckvn_ref, ckr_ref, z_ref) = refs

    mod = mod_ref[0]
    shift1, scale1 = mod[:, 0:d], mod[:, d:2 * d]
    tm = x_ref.shape[0]
    n_parts = PRE_ROW_PARTS if tm % (PRE_ROW_PARTS * LANES) == 0 else 1
    parts = [slice(k * tm // n_parts, (k + 1) * tm // n_parts) for k in range(n_parts)]

    proj = []
    for rs in parts:
        h = (_rms(x_ref[rs, :], g1_ref[...]) * (1.0 + scale1) + shift1).astype(_MM)
        proj.append([_dot(h, w_ref[...]) for w_ref in (wa_ref, wb_ref, wcq_ref, wckv_ref, wckr_ref, wdu_ref)])

    for rs, (pa, pb, cq, ckv, kr, du) in zip(parts, proj):
        aq, ak, av = pa[:, 0:HEAD_W], pa[:, HEAD_W:2 * HEAD_W], pa[:, 2 * HEAD_W:3 * HEAD_W]
        if latent:
            cosa, sina = cosa_ref[rs, :], sina_ref[rs, :]
            aq = _rope(aq, cosa, sina)
            ak = _rope(ak, cosa, sina)
        else:
            av_ref[rs, :] = av
        aqt_ref[:, rs] = (aq * (sa * LOG2E)).T.astype(aqt_ref.dtype)
        ak_ref[rs, :] = ak.astype(ak_ref.dtype)
        avt_ref[:, rs] = av.T.astype(avt_ref.dtype)

        bq, bk, bv = pb[:, 0:HEAD_W], pb[:, HEAD_W:2 * HEAD_W], pb[:, 2 * HEAD_W:3 * HEAD_W]
        bk_ref[rs, :] = bk.astype(bk_ref.dtype)
        if not latent:
            bv_ref[rs, :] = bv
        bqt_ref[:, rs] = (bq * (sb * LOG2E)).T.astype(bqt_ref.dtype)
        bvt_ref[:, rs] = bv.T.astype(bvt_ref.dtype)

        qc = _dot(_rms(cq, cqg_ref[...]), wuq_ref[...])
        ckvn = _rms(ckv, ckvg_ref[...])
        if latent:
            cosc, sinc = cosc_ref[rs, :], sinc_ref[rs, :]
            qc = _rope(qc, cosc, sinc)
            kr = _rope(kr, cosc, sinc)
        else:
            ckvn_ref[rs, :] = ckvn
            ckr_ref[rs, :] = kr[:, NOPE_C:NOPE_C + ROPE_C]
        qct_ref[:, rs] = (qc * (sc * LOG2E)).T.astype(qct_ref.dtype)
        kc_ref[rs, :] = (_dot(ckvn, wkc_ref[...]) + kr).astype(kc_ref.dtype)
        vct_ref[:, rs] = _dot(ckvn, wv_ref[...]).T.astype(vct_ref.dtype)
        z_ref[rs, :] = du[:, 0:CONV_CH] * _sigmoid(du[:, CONV_CH:2 * CONV_CH])


def _pre(x2d, mods3, mod_row, lw, *, latent, seq, tables):
    t, d = x2d.shape
    tm = min(PRE_TILE, seq)
    n_seq_tiles = seq // tm
    grid = (t // tm,)
    row = lambda i: (i, 0)
    const = lambda i: (0, 0)
    wnames = ("wa", "wb", "wcq", "wckv", "wckr", "wdu", "cqg", "ckvg", "wuq", "wkc", "wv")
    in_specs = [pl.BlockSpec((tm, d), row),
                pl.BlockSpec((1, 1, 6 * d), lambda i: (mod_row(i * tm), 0, 0)),
                pl.BlockSpec((1, d), const)]
    args = [x2d, mods3, lw["g1"]]
    for n in wnames:
        in_specs.append(pl.BlockSpec(lw[n].shape, const))
        args.append(lw[n])
    if latent:
        for tb in tables:
            in_specs.append(pl.BlockSpec((tm, HEAD_W), lambda i: (i % n_seq_tiles, 0)))
            args.append(tb)
    nat = lambda dt: (jax.ShapeDtypeStruct((t, HEAD_W), dt), pl.BlockSpec((tm, HEAD_W), row))
    tr = (jax.ShapeDtypeStruct((t // seq, HEAD_W, seq), _MM),
          pl.BlockSpec((None, HEAD_W, tm), lambda i: (i // n_seq_tiles, 0, i % n_seq_tiles)))
    if latent:
        outs = [tr, nat(_MM), tr, tr, nat(_MM), tr, tr, nat(_MM), tr]
    else:
        outs = [tr, nat(_F32), nat(_F32), tr, tr, nat(_F32), nat(_F32), tr, tr, nat(_MM), tr,
                (jax.ShapeDtypeStruct((t, KV_LORA), _F32), pl.BlockSpec((tm, KV_LORA), row)),
                (jax.ShapeDtypeStruct((t, ROPE_C), _F32), pl.BlockSpec((tm, ROPE_C), row))]
    outs.append((jax.ShapeDtypeStruct((t, CONV_CH), _F32), pl.BlockSpec((tm, CONV_CH), row)))
    out_shape = [o[0] for o in outs]
    out_specs = [o[1] for o in outs]
    kern = functools.partial(_pre_kernel, d=d, latent=latent, sa=DH_A ** -0.5, sb=DH_B ** -0.5,
                             sc=(NOPE_C + ROPE_C) ** -0.5)
    return pl.pallas_call(kern, grid=grid, in_specs=in_specs, out_specs=out_specs, out_shape=out_shape,
                          compiler_params=_cparams(("arbitrary",), 48),
                          name="pre_lat" if latent else "pre_ctx")(*args)


SCORE_LOOKAHEAD = 5
SUM_ROWS = 16
MAX_UNROLLED_KEY_TILES = 16


def _softmax_init(qt_ref, qz_sc, m_sc, acc_sc, n_chains, dqk):
    qz_sc[...] = jnp.zeros(qz_sc.shape, qz_sc.dtype)
    for c in range(n_chains):
        qz_sc[c, (c * dqk) % LANES:(c * dqk) % LANES + dqk, :] = qt_ref[c * dqk:(c + 1) * dqk, :]
    m_sc[...] = jnp.full(m_sc.shape, -jnp.inf, _F32)
    acc_sc[...] = jnp.zeros(acc_sc.shape, _F32)


def _softmax_steps(tiles, *, qz_sc, m_sc, acc_sc, n_chains, n_maps, dqk):
    steps = [(ti, c) for ti in range(len(tiles)) for c in range(n_chains)]
    kts, vts = {}, {}

    def scores(ti, c):
        if ti not in kts:
            kts[ti] = tiles[ti][0]().astype(_MM)
        grp = (c * dqk) // LANES
        s = jnp.dot(kts[ti][:, grp * LANES:(grp + 1) * LANES], qz_sc[c], preferred_element_type=_F32)
        return s if tiles[ti][2] is None else s + tiles[ti][2](c)

    def values(ti, h):
        if (ti, h) not in vts:
            vt = tiles[ti][1](h).astype(_MM)
            vts[ti, h] = jnp.concatenate([vt, jnp.ones((SUM_ROWS, vt.shape[1]), _MM)], axis=0)
        return vts[ti, h]

    pending = [scores(*st) for st in steps[:SCORE_LOOKAHEAD]]
    for idx, (ti, c) in enumerate(steps):
        s = pending.pop(0)
        if idx + SCORE_LOOKAHEAD < len(steps):
            pending.append(scores(*steps[idx + SCORE_LOOKAHEAD]))
        m_prev = m_sc[c]
        m_new = jnp.maximum(m_prev, jnp.max(s, axis=0, keepdims=True))
        alpha = jnp.exp2(m_prev - m_new)
        p = jnp.exp2(s - m_new).astype(_MM)
        acc_sc[c] = alpha * acc_sc[c] + jnp.dot(values(ti, c // n_maps), p, preferred_element_type=_F32)
        m_sc[c] = m_new


def _attn_kernel(*refs, n_heads, n_maps, dqk, dv, tk, n_own, unroll, tkc, n_cache, diff, lam_init):
    qt_ref, k_ref, vt_ref = refs[:3]
    refs = refs[3:]
    if n_cache:
        kc_ref, vct_ref = refs[:2]
        refs = refs[2:]
    if diff:
        alam_ref, subg_ref = refs[:2]
        refs = refs[2:]
    o_ref, qz_sc, m_sc, acc_sc, ot_sc = refs
    n_chains = n_heads * n_maps
    _softmax_init(qt_ref, qz_sc, m_sc, acc_sc, n_chains, dqk)
    update = functools.partial(_softmax_steps, qz_sc=qz_sc, m_sc=m_sc, acc_sc=acc_sc, n_chains=n_chains,
                               n_maps=n_maps, dqk=dqk)

    cache_tiles = [(lambda j=j: kc_ref[j * tkc:(j + 1) * tkc, :],
                    lambda h, j=j: vct_ref[h * dv:(h + 1) * dv, j * tkc:(j + 1) * tkc], None) for j in range(n_cache)]

    def own_tiles(starts):
        return [(lambda st=st: k_ref[pl.ds(st, tk), :],
                 lambda h, st=st: vt_ref[h * dv:(h + 1) * dv, pl.ds(st, tk)], None) for st in starts]

    if unroll == n_own:
        update(cache_tiles + own_tiles([u * tk for u in range(n_own)]))
    else:
        if n_cache:
            update(cache_tiles)

        def body(j, carry):
            update(own_tiles([pl.multiple_of((j * unroll + u) * tk, tk) for u in range(unroll)]))
            return carry

        lax.fori_loop(0, n_own // unroll, body, 0)

    def normalised(c):
        a = acc_sc[c]
        return a[0:dv] / a[dv:dv + 1]

    if diff:
        lv = alam_ref[...]
        lam = (jnp.exp(jnp.sum(lv[0:1] * lv[1:2], axis=-1, keepdims=True))
               - jnp.exp(jnp.sum(lv[2:3] * lv[3:4], axis=-1, keepdims=True)) + lam_init)
    for h in range(n_heads):
        if diff:
            o = normalised(2 * h) - lam * normalised(2 * h + 1)
            o = o * lax.rsqrt(jnp.mean(o * o, axis=0, keepdims=True) + EPS) * subg_ref[...] * (1.0 - lam_init)
        else:
            o = normalised(h)
        ot_sc[h * dv:(h + 1) * dv, :] = o
    o_ref[...] = ot_sc[...].T.astype(o_ref.dtype)


def _attn(qt, k, vt, cache, diff_args, *, n_heads, n_maps, dqk, dv, lam_init=0.0, name):
    nb, s, _ = k.shape
    tq = min(ATTN_TILE, s)
    tk = min(ATTN_TILE, s)
    n_own = s // tk
    qmap = lambda b, i: (b, 0, i)
    bmap = lambda b, i: (b, 0, 0)
    in_specs = [pl.BlockSpec((None, HEAD_W, tq), qmap),
                pl.BlockSpec((None, s, HEAD_W), bmap),
                pl.BlockSpec((None, HEAD_W, s), bmap)]
    args = [qt, k, vt]
    tkc, n_cache = tk, 0
    if cache is not None:
        past = cache[0].shape[1]
        tkc = min(ATTN_TILE, past)
        n_cache = past // tkc
        in_specs += [pl.BlockSpec((None, past, HEAD_W), bmap), pl.BlockSpec((None, HEAD_W, past), bmap)]
        args += list(cache)
    if diff_args is not None:
        in_specs += [pl.BlockSpec(diff_args[0].shape, lambda b, i: (0, 0)),
                     pl.BlockSpec(diff_args[1].shape, lambda b, i: (0, 0))]
        args += list(diff_args)
    n_chains = n_heads * n_maps
    unroll = n_own if n_own <= MAX_UNROLLED_KEY_TILES else (8 if n_own % 8 == 0 else 1)
    kern = functools.partial(_attn_kernel, n_heads=n_heads, n_maps=n_maps, dqk=dqk, dv=dv, tk=tk, n_own=n_own,
                             unroll=unroll, tkc=tkc, n_cache=n_cache, diff=diff_args is not None, lam_init=lam_init)
    return pl.pallas_call(
        kern, grid=(nb, s // tq), in_specs=in_specs,
        out_specs=pl.BlockSpec((None, tq, HEAD_W), lambda b, i: (b, i, 0)),
        out_shape=jax.ShapeDtypeStruct((nb, s, HEAD_W), _MM),
        scratch_shapes=[pltpu.VMEM((n_chains, LANES, tq), _MM), pltpu.VMEM((n_chains, 1, tq), _F32),
                        pltpu.VMEM((n_chains, dv + SUM_ROWS, tq), _F32), pltpu.VMEM((n_heads * dv, tq), _F32)],
        compiler_params=_cparams(("arbitrary", "arbitrary"), 48),
        name=name,
    )(*args)


NBR_ROWS = 4
NBR_SPAN_ROWS = 12


def _nbr_kernel(qt_ref, k_ref, vt_ref, kc_ref, vct_ref, tb_ref, o_ref, qz_sc, m_sc, acc_sc, ot_sc, *, rows, tkc, n_cache):
    i = pl.program_id(1)
    ss = jnp.clip(i * NBR_ROWS - WIN_R // 2, 0, rows - NBR_SPAN_ROWS)
    tk = NBR_ROWS * GRID_W
    kstart = pl.multiple_of(ss * GRID_W, tk)
    hsl = lambda h: slice(h * DH_B, (h + 1) * DH_B)

    _softmax_init(qt_ref, qz_sc, m_sc, acc_sc, H_B, DH_B)
    tiles = [(lambda j=j: kc_ref[j * tkc:(j + 1) * tkc, :],
              lambda h, j=j: vct_ref[hsl(h), j * tkc:(j + 1) * tkc], None) for j in range(n_cache)]
    starts = [pl.multiple_of(kstart + u * tk, tk) for u in range(NBR_SPAN_ROWS // NBR_ROWS)]
    tiles += [(lambda st=st: k_ref[pl.ds(st, tk), :],
               lambda h, st=st: vt_ref[hsl(h), pl.ds(st, tk)],
               lambda c, u=u: tb_ref[c, u * tk:(u + 1) * tk, :]) for u, st in enumerate(starts)]
    _softmax_steps(tiles, qz_sc=qz_sc, m_sc=m_sc, acc_sc=acc_sc, n_chains=H_B, n_maps=1, dqk=DH_B)
    for h in range(H_B):
        a = acc_sc[h]
        ot_sc[hsl(h), :] = a[0:DH_B] / a[DH_B:DH_B + 1]
    o_ref[...] = ot_sc[...].T.astype(o_ref.dtype)


def _nbr_bias_table(rpb, rows):
    nh, n_r, n_c = rpb.shape
    c = np.arange(GRID_W)[:, None]
    kc = np.arange(GRID_W)[None, :]
    cidx = np.clip(kc - c + WIN_C - 1, 0, n_c - 1).reshape(1, -1)
    onehot = jnp.asarray((cidx == np.arange(n_c)[:, None]).astype(np.float32))
    toep = jnp.dot(rpb.reshape(nh * n_r, n_c).astype(_F32), onehot, precision=lax.Precision.HIGHEST)
    toep = toep.reshape(nh, n_r, GRID_W, GRID_W)
    cstart = np.clip(c - WIN_C // 2, 0, GRID_W - WIN_C)
    valid = (kc >= cstart) & (kc < cstart + WIN_C)
    toep = jnp.where(valid[None, None], toep, -jnp.inf) * LOG2E
    assert rows % NBR_ROWS == 0 and rows >= NBR_SPAN_ROWS
    r0 = np.array([0, NBR_ROWS, rows - NBR_ROWS])[:, None, None]
    ss = np.clip(r0 - WIN_R // 2, 0, rows - NBR_SPAN_ROWS)
    key_row = ss + np.arange(NBR_SPAN_ROWS)[None, :, None]
    r = r0 + np.arange(NBR_ROWS)[None, None, :]
    rs = np.clip(r - WIN_R // 2, 0, rows - WIN_R)
    in_rows = (key_row >= rs) & (key_row < rs + WIN_R)
    ridx = np.clip(key_row - r + WIN_R - 1, 0, n_r - 1)
    tb = jnp.take(toep, jnp.asarray(ridx.reshape(-1)), axis=1)
    tb = tb.reshape((nh,) + ridx.shape + (GRID_W, GRID_W))
    tb = jnp.where(in_rows[None, :, :, :, None, None], tb, -jnp.inf)
    tb = tb.transpose(1, 0, 2, 5, 3, 4)
    return tb.reshape(3, nh, NBR_SPAN_ROWS * GRID_W, NBR_ROWS * GRID_W)


def _nbr(qt, k, vt, kc, vct, tb):
    nb, s, _ = k.shape
    rows = s // GRID_W
    n_steps = rows // NBR_ROWS
    past = kc.shape[1]
    tq = NBR_ROWS * GRID_W
    bmap = lambda b, i: (b, 0, 0)
    variant = lambda b, i: (jnp.where(i == 0, 0, jnp.where(i == n_steps - 1, 2, 1)), 0, 0, 0)
    tkc = min(ATTN_TILE, past)
    return pl.pallas_call(
        functools.partial(_nbr_kernel, rows=rows, tkc=tkc, n_cache=past // tkc), grid=(nb, n_steps),
        in_specs=[pl.BlockSpec((None, HEAD_W, tq), lambda b, i: (b, 0, i)),
                  pl.BlockSpec((None, s, HEAD_W), bmap),
                  pl.BlockSpec((None, HEAD_W, s), bmap),
                  pl.BlockSpec((None, past, HEAD_W), bmap),
                  pl.BlockSpec((None, HEAD_W, past), bmap),
                  pl.BlockSpec((None,) + tb.shape[1:], variant)],
        out_specs=pl.BlockSpec((None, tq, HEAD_W), lambda b, i: (b, i, 0)),
        out_shape=jax.ShapeDtypeStruct((nb, s, HEAD_W), _MM),
        scratch_shapes=[pltpu.VMEM((H_B, LANES, tq), _MM), pltpu.VMEM((H_B, 1, tq), _F32),
                        pltpu.VMEM((H_B, DH_B + SUM_ROWS, tq), _F32), pltpu.VMEM((HEAD_W, tq), _F32)],
        compiler_params=_cparams(("arbitrary", "arbitrary"), 48),
        name="nbr_attn",
    )(qt, k, vt, kc, vct, tb)


def _mla_cache_kernel(ckv_ref, kr_ref, wkc_ref, wvt_ref, e_ref, kc_ref, vct_ref):
    ckv = ckv_ref[...]
    kc_ref[...] = (_dot(ckv, wkc_ref[...]) + _dot(kr_ref[...], e_ref[...])).astype(kc_ref.dtype)
    vct_ref[...] = _dot_nt(wvt_ref[...], ckv).astype(vct_ref.dtype)


def _mla_cache(ckv, kr, wkc, wvt, expand):
    nb, past, _ = ckv.shape
    bmap = lambda b: (b, 0, 0)
    const = lambda b: (0, 0)
    return pl.pallas_call(
        _mla_cache_kernel, grid=(nb,),
        in_specs=[pl.BlockSpec((None, past, KV_LORA), bmap), pl.BlockSpec((None, past, ROPE_C), bmap),
                  pl.BlockSpec(wkc.shape, const), pl.BlockSpec(wvt.shape, const), pl.BlockSpec(expand.shape, const)],
        out_specs=[pl.BlockSpec((None, past, HEAD_W), bmap), pl.BlockSpec((None, HEAD_W, past), bmap)],
        out_shape=[jax.ShapeDtypeStruct((nb, past, HEAD_W), _MM), jax.ShapeDtypeStruct((nb, HEAD_W, past), _MM)],
        compiler_params=_cparams(("arbitrary",), 32),
        name="mla_cache",
    )(ckv, kr, wkc, wvt, expand)


CONV_HALO = 16


def _conv_rows(win, w, b, g, beta):
    n_win = win.shape[0]
    rc = n_win - 2 * CONV_HALO
    shift = CONV_HALO - CONV_K // 2
    acc = jnp.zeros((rc, CONV_CH), _F32)
    for phase in range(SUBLANES):
        rolled = win if phase == 0 else pltpu.roll(win, n_win - phase, axis=0)
        for j in range(CONV_K):
            if (j + shift) % SUBLANES == phase:
                a = (j + shift) - phase
                acc = acc + rolled[a:a + rc, :] * w[j:j + 1, :]
    y = acc + b
    mu = jnp.mean(y, axis=-1, keepdims=True)
    yc = y - mu
    var = jnp.mean(yc * yc, axis=-1, keepdims=True)
    yn = yc * lax.rsqrt(var + EPS) * g + beta
    return yn * _sigmoid(yn)


def _conv_kernel(z_ref, w_ref, b_ref, g_ref, beta_ref, o_ref, zp_ref, *, seq, rc):
    zp_ref[0:CONV_HALO, :] = jnp.zeros((CONV_HALO, CONV_CH), _F32)
    zp_ref[CONV_HALO:CONV_HALO + seq, :] = z_ref[...]
    zp_ref[CONV_HALO + seq:2 * CONV_HALO + seq, :] = jnp.zeros((CONV_HALO, CONV_CH), _F32)
    w = w_ref[...]

    def chunk(c, carry):
        base = pl.multiple_of(c * rc, rc)
        win = zp_ref[pl.ds(base, rc + 2 * CONV_HALO), :]
        o_ref[pl.ds(base, rc), :] = _conv_rows(win, w, b_ref[...], g_ref[...], beta_ref[...]).astype(o_ref.dtype)
        return carry

    lax.fori_loop(0, seq // rc, chunk, 0)


def _conv(z, w, b, g, beta):
    nb, seq, _ = z.shape
    rc = min(CONV_CHUNK, seq)
    bmap = lambda i: (i, 0, 0)
    const = lambda i: (0, 0)
    kern = functools.partial(_conv_kernel, seq=seq, rc=rc)
    return pl.pallas_call(
        kern, grid=(nb,),
        in_specs=[pl.BlockSpec((None, seq, CONV_CH), bmap), pl.BlockSpec(w.shape, const),
                  pl.BlockSpec(b.shape, const), pl.BlockSpec(g.shape, const), pl.BlockSpec(beta.shape, const)],
        out_specs=pl.BlockSpec((None, seq, CONV_CH), bmap),
        out_shape=jax.ShapeDtypeStruct((nb, seq, CONV_CH), _MM),
        scratch_shapes=[pltpu.VMEM((seq + 2 * CONV_HALO, CONV_CH), _F32)],
        compiler_params=_cparams(("arbitrary",), 40),
        name="conv",
    )(z, w, b, g, beta)


POST_ROW_PARTS = 2

def _post_kernel(*refs, d, conv_tiles_per_seq):
    x_ref, mod_ref, oa_ref, ob_ref, oc_ref = refs[:5]
    refs = refs[5:]
    if conv_tiles_per_seq:
        z_ref, cw_ref, cb_ref, lg_ref, lb_ref = refs[:5]
        refs = refs[5:]
    else:
        od_ref = refs[0]
        refs = refs[1:]
    (g1_ref, wg_ref, bg_ref, awo_ref, bwo_ref, cwo_ref, dwo_ref, wout_ref, g2_ref, rw_ref, rb_ref,
     x1_ref, h2_ref, dw_ref) = refs
    mod = mod_ref[0]
    shift1, scale1, gate1 = mod[:, 0:d], mod[:, d:2 * d], mod[:, 2 * d:3 * d]
    shift2, scale2 = mod[:, 3 * d:4 * d], mod[:, 4 * d:5 * d]
    tm = x_ref.shape[0]
    parts = [slice(k * tm // POST_ROW_PARTS, (k + 1) * tm // POST_ROW_PARTS) for k in range(POST_ROW_PARTS)]

    if conv_tiles_per_seq:
        seq = z_ref.shape[0]
        t0 = pl.multiple_of((pl.program_id(0) % conv_tiles_per_seq) * tm, tm)
        before = z_ref[pl.ds(pl.multiple_of(jnp.maximum(t0 - CONV_HALO, 0), CONV_HALO), CONV_HALO), :]
        after = z_ref[pl.ds(pl.multiple_of(jnp.minimum(t0 + tm, seq - CONV_HALO), CONV_HALO), CONV_HALO), :]
        win = jnp.concatenate([jnp.where(t0 > 0, before, 0.0), z_ref[pl.ds(t0, tm), :],
                               jnp.where(t0 + tm < seq, after, 0.0)], axis=0)
        cw = cw_ref[...]

        def conv_branch(rs):
            chunks = [_conv_rows(win[a:a + CONV_CHUNK + 2 * CONV_HALO, :], cw, cb_ref[...], lg_ref[...], lb_ref[...])
                      for a in range(rs.start, rs.stop, CONV_CHUNK)]
            return jnp.concatenate(chunks, axis=0)
    else:
        conv_branch = lambda rs: od_ref[rs, :]
    branches = ((lambda rs: oa_ref[rs, :], awo_ref), (lambda rs: ob_ref[rs, :], bwo_ref),
                (lambda rs: oc_ref[rs, :], cwo_ref), (conv_branch, dwo_ref))

    xs, branch_mm = [], []
    for rs in parts:
        x = x_ref[rs, :]
        h = (_rms(x, g1_ref[...]) * (1.0 + scale1) + shift1).astype(_MM)
        xs.append(x)
        branch_mm.append([(_dot(h, wg_ref[:, bi * d:(bi + 1) * d]), _dot(load(rs), wo_ref[...]))
                          for bi, (load, wo_ref) in enumerate(branches)])
    mixed = []
    for pairs in branch_mm:
        merged = None
        for bi, (gate_logits, out) in enumerate(pairs):
            term = _sigmoid(gate_logits + bg_ref[:, bi * d:(bi + 1) * d]) * out
            merged = term if merged is None else merged + term
        mixed.append(_dot(merged, wout_ref[...]))
    logits = []
    for rs, x, y in zip(parts, xs, mixed):
        x1 = x + gate1 * y
        x1_ref[rs, :] = x1
        h2_mm = (_rms(x1, g2_ref[...]) * (1.0 + scale2) + shift2).astype(_MM)
        h2_ref[rs, :] = h2_mm
        logits.append(_dot(h2_mm, rw_ref[...]) + rb_ref[...])
    for rs, r in zip(parts, logits):
        lgp, lep = r[:, 0:LANES], r[:, LANES:2 * LANES]
        lane = lax.broadcasted_iota(jnp.int32, lgp.shape, 1)
        neg = -jnp.inf
        lg = jnp.where(lane < N_GROUPS, lgp, neg)
        mg = jnp.max(lg, axis=-1, keepdims=True)
        gi = jnp.min(jnp.where(lg == mg, lane, LANES), axis=-1, keepdims=True)
        gw = 1.0 / jnp.sum(jnp.exp(lg - mg), axis=-1, keepdims=True)
        in_group = (lane < N_EXPERTS) & ((lane // EXPERTS_PER_GROUP) == gi)
        le = jnp.where(in_group, lep, neg)
        t1 = jnp.max(le, axis=-1, keepdims=True)
        i1 = jnp.min(jnp.where(le == t1, lane, LANES), axis=-1, keepdims=True)
        le2 = jnp.where(lane == i1, neg, le)
        t2 = jnp.max(le2, axis=-1, keepdims=True)
        i2 = jnp.min(jnp.where(le2 == t2, lane, LANES), axis=-1, keepdims=True)
        e2 = jnp.exp(t2 - t1)
        w1 = gw / (1.0 + e2)
        w2 = gw * e2 / (1.0 + e2)
        dw_ref[rs, :] = jnp.where(lane == i1, w1, 0.0) + jnp.where(lane == i2, w2, 0.0)


def _post(x2d, mods3, mod_row, oa, ob, oc, od, lw, *, z=None):
    t, d = x2d.shape
    tm = min(POST_TILE, t)
    row = lambda i: (i, 0)
    const = lambda i: (0, 0)
    wnames = ("g1", "wgate", "bgate", "awo", "bwo", "cwo", "dwo", "wout", "g2", "rw", "rb")
    in_specs = [pl.BlockSpec((tm, d), row), pl.BlockSpec((1, 1, 6 * d), lambda i: (mod_row(i * tm), 0, 0))]
    in_specs += [pl.BlockSpec((tm, HEAD_W), row)] * 3
    args = [x2d, mods3, oa, ob, oc]
    conv_tiles_per_seq = 0
    if z is not None:
        seq = z.shape[1]
        assert seq % tm == 0 and tm % (POST_ROW_PARTS * CONV_CHUNK) == 0
        conv_tiles_per_seq = seq // tm
        cnames = ("conv_w", "conv_b", "ln_g", "ln_b")
        in_specs += [pl.BlockSpec((None, seq, CONV_CH), lambda i: (i // conv_tiles_per_seq, 0, 0))]
        in_specs += [pl.BlockSpec(lw[n].shape, const) for n in cnames]
        args += [z] + [lw[n] for n in cnames]
    else:
        in_specs += [pl.BlockSpec((tm, HEAD_W), row)]
        args += [od]
    in_specs += [pl.BlockSpec(lw[n].shape, const) for n in wnames]
    args += [lw[n] for n in wnames]
    return pl.pallas_call(
        functools.partial(_post_kernel, d=d, conv_tiles_per_seq=conv_tiles_per_seq), grid=(t // tm,), in_specs=in_specs,
        out_specs=[pl.BlockSpec((tm, d), row), pl.BlockSpec((tm, d), row), pl.BlockSpec((tm, LANES), row)],
        out_shape=[jax.ShapeDtypeStruct((t, d), _F32), jax.ShapeDtypeStruct((t, d), _MM),
                   jax.ShapeDtypeStruct((t, LANES), _F32)],
        compiler_params=_cparams(("arbitrary",), 56),
        name="post",
    )(*args)


def _moe_kernel(h2_ref, dw_ref, x1_ref, mod_ref, w1_ref, w2_ref, fg_ref, o_ref, acc_ref, *, d, final):
    g = pl.program_id(1)

    @pl.when(g == 0)
    def _():
        acc_ref[...] = jnp.zeros_like(acc_ref)

    h2 = h2_ref[...]
    dw = dw_ref[...]
    lane = lax.broadcasted_iota(jnp.int32, dw.shape, 1)
    acts = []
    for j in range(EXPERTS_PER_GROUP):
        hid = _dot(h2, w1_ref[j])
        gate, up = hid[:, 0:D_EXPERT], hid[:, D_EXPERT:2 * D_EXPERT]
        we = jnp.sum(jnp.where(lane == g * EXPERTS_PER_GROUP + j, dw, 0.0), axis=-1, keepdims=True)
        acts.append((gate * _sigmoid(gate) * up * we).astype(_MM))
    acc_ref[...] += _dot(jnp.concatenate(acts, axis=-1), w2_ref[...])

    @pl.when(g == pl.num_programs(1) - 1)
    def _():
        gate2 = mod_ref[0][:, 5 * d:6 * d]
        x2 = x1_ref[...] + gate2 * acc_ref[...]
        if final:
            x2 = _rms(x2, fg_ref[...])
        o_ref[...] = x2


def _moe(h2, dw, x1, mods3, mod_row, w1, w2, fg, *, final):
    t, d = x1.shape
    tm = min(MOE_TILE, t)
    n_e, _, d_hid = w1.shape
    w2g = w2.reshape(N_GROUPS, EXPERTS_PER_GROUP * D_EXPERT, d)
    row = lambda i, g: (i, 0)
    return pl.pallas_call(
        functools.partial(_moe_kernel, d=d, final=final), grid=(t // tm, N_GROUPS),
        in_specs=[pl.BlockSpec((tm, d), row), pl.BlockSpec((tm, LANES), row), pl.BlockSpec((tm, d), row),
                  pl.BlockSpec((1, 1, 6 * d), lambda i, g: (mod_row(i * tm), 0, 0)),
                  pl.BlockSpec((EXPERTS_PER_GROUP, d, d_hid), lambda i, g: (g, 0, 0)),
                  pl.BlockSpec((None,) + w2g.shape[1:], lambda i, g: (g, 0, 0)),
                  pl.BlockSpec((1, d), lambda i, g: (0, 0))],
        out_specs=pl.BlockSpec((tm, d), row),
        out_shape=jax.ShapeDtypeStruct((t, d), _F32),
        scratch_shapes=[pltpu.VMEM((tm, d), _F32)],
        compiler_params=_cparams(("arbitrary", "arbitrary"), 56),
        name="moe_final" if final else "moe",
    )(h2, dw, x1, mods3, w1, w2g, fg)


def _rope_tables(n_tok, dim):
    hq = dim // 4
    tpos = np.arange(n_tok)
    rows = (tpos // GRID_W).astype(np.float32)
    cols = (tpos % GRID_W).astype(np.float32)
    inv = jnp.asarray(ROPE_THETA, _F32) ** (-jnp.arange(hq, dtype=_F32) / hq)
    ar = jnp.asarray(rows)[:, None] * inv[None, :]
    ac = jnp.asarray(cols)[:, None] * inv[None, :]
    ang = jnp.concatenate([ar, ar, ac, ac], axis=-1)
    return jnp.cos(ang), jnp.sin(ang)


def _layer_weights(l, p, grid_rows):
    mm = lambda a: a.astype(_MM)
    w_in = p["w_in"][l]
    pts = np.cumsum([0, 2 * H_A * DH_A, 2 * H_A * DH_A, 2 * H_A * DH_A, H_B * DH_B, H_B * DH_B, H_B * DH_B,
                     Q_LORA, KV_LORA, ROPE_C, 2 * CONV_CH])
    seg = lambda i, j: w_in[:, pts[i]:pts[j]]
    d = w_in.shape[0]
    wckr = seg(8, 9)
    zeros_n = jnp.zeros((d, NOPE_C), w_in.dtype)
    wckr_t = jnp.concatenate([zeros_n, wckr] * H_C, axis=1)
    wukv = p["c_wukv"][l].reshape(KV_LORA, H_C, NOPE_C + DV_C)
    wkc = jnp.concatenate([wukv[:, :, :NOPE_C], jnp.zeros((KV_LORA, H_C, ROPE_C), wukv.dtype)], axis=-1)
    wkc = wkc.reshape(KV_LORA, H_C * (NOPE_C + ROPE_C))
    wv = wukv[:, :, NOPE_C:].reshape(KV_LORA, H_C * DV_C)
    router = jnp.zeros((d, 2 * LANES), _F32)
    router = router.at[:, 0:N_GROUPS].set(p["moe_w_group"][l]).at[:, LANES:LANES + N_EXPERTS].set(p["moe_w_expert"][l])
    rb = jnp.zeros((1, 2 * LANES), _F32)
    rb = rb.at[0, 0:N_GROUPS].set(p["moe_b_group"][l]).at[0, LANES:LANES + N_EXPERTS].set(p["moe_b_expert"][l])
    conv_w = p["d_conv_w"][l]
    return dict(
        g1=p["norm1_g"][l][None], g2=p["norm2_g"][l][None],
        wa=mm(seg(0, 3)), wb=mm(seg(3, 6)), wcq=mm(seg(6, 7)), wckv=mm(seg(7, 8)), wckr=mm(wckr_t), wdu=mm(seg(9, 10)),
        cqg=p["c_q_norm_g"][l][None], ckvg=p["c_kv_norm_g"][l][None],
        wuq=mm(p["c_wuq"][l]), wkc=mm(wkc), wv=mm(wv), wvt=mm(wv.T),
        alam=p["a_lambda"][l], subg=p["a_subln_g"][l][:, None],
        wgate=mm(p["w_gate"][l]), bgate=p["b_gate"][l][None],
        awo=mm(p["a_wo"][l]), bwo=mm(p["b_wo"][l]), cwo=mm(p["c_wo"][l]), dwo=mm(p["d_wo"][l]),
        wout=mm(p["w_out"][l]), rw=mm(router), rb=rb,
        conv_w=conv_w, conv_b=p["d_conv_b"][l][None], ln_g=p["d_ln_g"][l][None], ln_b=p["d_ln_b"][l][None],
        w1=mm(p["moe_w1"][l]), w2=mm(p["moe_w2"][l]),
        tb=_nbr_bias_table(p["b_rpb"][l], grid_rows),
    )


def _layer(x2d, mods3, mod_row, lw, l, n_layers, *, nb, seq, fg, ctx, tables):
    latent = ctx is not None
    t = x2d.shape[0]
    outs = _pre(x2d, mods3, mod_row, lw, latent=latent, seq=seq, tables=tables)
    r3 = lambda a: a.reshape(nb, seq, a.shape[-1])
    lam_init = 0.8 - 0.6 * math.exp(-0.3 * l)
    diff_args = (lw["alam"], lw["subg"])
    if latent:
        aqt, ak, avt, bqt, bk, bvt, qct, kc, vct, z = outs
        own = None
        a_k_c, a_vt_c, b_k_c, b_vt_c, c_k_c, c_vt_c = ctx
        oa = _attn(aqt, r3(ak), avt, (a_k_c, a_vt_c), diff_args, n_heads=H_A, n_maps=2,
                   dqk=DH_A, dv=2 * DH_A, lam_init=lam_init, name="attn_a_lat")
        ob = _nbr(bqt, r3(bk), bvt, b_k_c, b_vt_c, lw["tb"])
        oc = _attn(qct, r3(kc), vct, (c_k_c, c_vt_c), None, n_heads=H_C, n_maps=1,
                   dqk=NOPE_C + ROPE_C, dv=DV_C, name="attn_c_lat")
    else:
        aqt, ak, av, avt, bqt, bk, bv, bvt, qct, kc, vct, ckvn, ckr, z = outs
        own = (ak, av, bk, bv, ckvn, ckr)
        oa = _attn(aqt, r3(ak), avt, None, diff_args, n_heads=H_A, n_maps=2,
                   dqk=DH_A, dv=2 * DH_A, lam_init=lam_init, name="attn_a_ctx")
        ob = _attn(bqt, r3(bk), bvt, None, None, n_heads=H_B, n_maps=1, dqk=DH_B, dv=DH_B, name="attn_b_ctx")
        oc = _attn(qct, r3(kc), vct, None, None, n_heads=H_C, n_maps=1,
                   dqk=NOPE_C + ROPE_C, dv=DV_C, name="attn_c_ctx")
    f2 = lambda a: a.reshape(t, a.shape[-1])
    if seq % min(POST_TILE, t) == 0:
        x1, h2, dw = _post(x2d, mods3, mod_row, f2(oa), f2(ob), f2(oc), None, lw, z=r3(z))
    else:
        od = _conv(r3(z), lw["conv_w"], lw["conv_b"], lw["ln_g"], lw["ln_b"])
        x1, h2, dw = _post(x2d, mods3, mod_row, f2(oa), f2(ob), f2(oc), f2(od), lw)
    x2 = _moe(h2, dw, x1, mods3, mod_row, lw["w1"], lw["w2"], fg, final=(l == n_layers - 1))
    return x2, own


def kernel(x_prompt, x_sample, cache_a_k, cache_a_v, cache_b_k, cache_b_v, cache_c_kv, cache_c_krope, c, c_ctx, norm1_g, norm2_g, w_ada, b_ada, w_in, a_lambda, a_subln_g, a_wo, b_rpb, b_wo, c_q_norm_g, c_kv_norm_g, c_wuq, c_wukv, c_wo, d_conv_w, d_conv_b, d_ln_g, d_ln_b, d_wo, w_gate, b_gate, w_out, moe_w_group, moe_b_group, moe_w_expert, moe_b_expert, moe_w1, moe_w2, final_norm_g):
    p = dict(norm1_g=norm1_g, norm2_g=norm2_g, w_in=w_in, a_lambda=a_lambda, a_subln_g=a_subln_g, a_wo=a_wo,
             b_rpb=b_rpb, b_wo=b_wo, c_q_norm_g=c_q_norm_g, c_kv_norm_g=c_kv_norm_g, c_wuq=c_wuq, c_wukv=c_wukv,
             c_wo=c_wo, d_conv_w=d_conv_w, d_conv_b=d_conv_b, d_ln_g=d_ln_g, d_ln_b=d_ln_b, d_wo=d_wo,
             w_gate=w_gate, b_gate=b_gate, w_out=w_out, moe_w_group=moe_w_group, moe_b_group=moe_b_group,
             moe_w_expert=moe_w_expert, moe_b_expert=moe_b_expert, moe_w1=moe_w1, moe_w2=moe_w2)
    n_layers = w_in.shape[0]
    bc, sc, d = x_prompt.shape
    bl, sl, _ = x_sample.shape
    past = cache_a_k.shape[2]
    assert bl + 1 <= COND_ROWS and sl % GRID_W == 0

    cond8 = jnp.zeros((COND_ROWS, d), _F32).at[0].set(c_ctx).at[1:1 + bl].set(c)
    mods3 = _adaln(cond8, w_ada, b_ada).reshape(n_layers * COND_ROWS, 1, 6 * d)
    fg = final_norm_g[None]

    cos_a, sin_a = _rope_tables(sl, DH_A)
    cos_c, sin_c = _rope_tables(sl, ROPE_C)
    reps_a = HEAD_W // DH_A
    ones_n, zeros_n = jnp.ones((sl, NOPE_C), _F32), jnp.zeros((sl, NOPE_C), _F32)
    tables = (jnp.tile(cos_a, (1, reps_a)), jnp.tile(sin_a, (1, reps_a)),
              jnp.concatenate([ones_n, cos_c] * H_C, axis=1), jnp.concatenate([zeros_n, sin_c] * H_C, axis=1))
    expand = np.zeros((ROPE_C, HEAD_W), np.float32)
    for hh in range(H_C):
        expand[np.arange(ROPE_C), hh * (NOPE_C + ROPE_C) + NOPE_C + np.arange(ROPE_C)] = 1.0
    expand = jnp.asarray(expand, _MM)

    lws = [_layer_weights(l, p, sl // GRID_W) for l in range(n_layers)]

    xp = x_prompt.reshape(bc * sc, d)
    states = []
    for l in range(n_layers):
        xp, own = _layer(xp, mods3, (lambda tok, l=l: l * COND_ROWS), lws[l], l, n_layers, nb=bc, seq=sc, fg=fg, ctx=None,
                         tables=None)
        states.append(own)
    y_prompt = xp.reshape(bc, sc, d)

    xs = x_sample.reshape(bl * sl, d)
    for l in range(n_layers):
        flat = lambda a: a[:, l].reshape(bl, past, -1)
        c_k_c, c_vt_c = _mla_cache(flat(cache_c_kv), flat(cache_c_krope), lws[l]["wkc"], lws[l]["wvt"], expand)
        ctx = (flat(cache_a_k).astype(_MM), jnp.swapaxes(flat(cache_a_v), 1, 2).astype(_MM),
               flat(cache_b_k).astype(_MM), jnp.swapaxes(flat(cache_b_v), 1, 2).astype(_MM), c_k_c, c_vt_c)
        xs, _ = _layer(xs, mods3, (lambda tok, l=l: l * COND_ROWS + 1 + tok // sl), lws[l], l, n_layers,
                       nb=bl, seq=sl, fg=fg, ctx=ctx, tables=tables)
    y_sample = xs.reshape(bl, sl, d)

    st = lambda k, shape: jnp.stack([s[k].reshape(shape) for s in states], axis=1)
    new_a_k = st(0, (bc, sc, H_A, 2 * DH_A))
    new_a_v = st(1, (bc, sc, H_A, 2 * DH_A))
    new_b_k = st(2, (bc, sc, H_B, DH_B))
    new_b_v = st(3, (bc, sc, H_B, DH_B))
    new_c_kv = st(4, (bc, sc, KV_LORA))
    new_c_krope = st(5, (bc, sc, ROPE_C))
    return (y_prompt, y_sample, new_a_k, new_a_v, new_b_k, new_b_v, new_c_kv, new_c_krope)
```

```python
import functools
import math

import numpy as np
import jax
import jax.numpy as jnp
from jax import lax
from jax.experimental import pallas as pl
from jax.experimental.pallas import tpu as pltpu

GRID_W = 64
EPS = 1e-6
ROPE_THETA = 10000.0
H_A, DH_A = 4, 32
H_B, DH_B = 4, 64
WIN_R, WIN_C = 8, 16
H_C, Q_LORA, KV_LORA, NOPE_C, ROPE_C, DV_C = 4, 256, 128, 32, 32, 64
CONV_CH, CONV_K = 256, 31
N_BRANCH = 4
N_GROUPS, EXPERTS_PER_GROUP, N_EXPERTS, D_EXPERT = 4, 4, 16, 256
HEAD_W = 256
LANES = 128
SUBLANES = 8
V7X_VMEM_BYTES = 64 * 1024 * 1024
LOG2E = math.log2(math.e)

PRE_TILE = 512
POST_TILE = 512
MOE_TILE = 1024
ATTN_TILE = 256
ATTN_QUERY_BLOCKS = 2
CONV_CHUNK = 128
ADALN_COLS = 1536
COND_ROWS = SUBLANES

_MM = jnp.bfloat16
_F32 = jnp.float32


def _cparams(sem, vmem_mb):
    return pltpu.CompilerParams(dimension_semantics=sem,
                                vmem_limit_bytes=min(vmem_mb * 1024 * 1024, V7X_VMEM_BYTES - 8 * 1024 * 1024))


def _dot(a, b):
    return jnp.dot(a.astype(_MM), b.astype(_MM), preferred_element_type=_F32)


def _dot_nt(a, b):
    return lax.dot_general(a.astype(_MM), b.astype(_MM), (((1,), (1,)), ((), ())),
                           preferred_element_type=_F32)


def _sigmoid(x):
    return 1.0 / (1.0 + jnp.exp(-x))


def _rms(x, g):
    return x * lax.rsqrt(jnp.mean(x * x, axis=-1, keepdims=True) + EPS) * g


def _rope(x, cos, sin):
    w = x.shape[-1]
    lane = lax.broadcasted_iota(jnp.int32, x.shape, 1)
    first = (lane % 16) < 8
    fwd = pltpu.roll(x, w - 8, axis=1)
    bwd = pltpu.roll(x, 8, axis=1)
    return x * cos + jnp.where(first, -fwd, bwd) * sin


def _adaln_kernel(cond_ref, w_ref, b_ref, o_ref):
    c = cond_ref[...]
    a = c * _sigmoid(c)
    o_ref[0] = _dot(a, w_ref[0]) + b_ref[0]


def _adaln(cond8, w_ada, b_ada):
    n_l, d, d6 = w_ada.shape
    tn = ADALN_COLS
    return pl.pallas_call(
        _adaln_kernel,
        grid=(n_l, d6 // tn),
        in_specs=[pl.BlockSpec((COND_ROWS, d), lambda l, j: (0, 0)),
                  pl.BlockSpec((1, d, tn), lambda l, j: (l, 0, j)),
                  pl.BlockSpec((1, 1, tn), lambda l, j: (l, 0, j))],
        out_specs=pl.BlockSpec((1, COND_ROWS, tn), lambda l, j: (l, 0, j)),
        out_shape=jax.ShapeDtypeStruct((n_l, COND_ROWS, d6), _F32),
        compiler_params=_cparams(("arbitrary", "arbitrary"), 40),
        name="adaln",
    )(cond8, w_ada, b_ada.reshape(n_l, 1, d6))


PRE_ROW_PARTS = 2


def _pre_kernel(*refs, d, latent, sa, sb, sc):
    (x_ref, mod_ref, g1_ref, wa_ref, wb_ref, wcq_ref, wckv_ref, wckr_ref, wdu_ref,
     cqg_ref, ckvg_ref, wuq_ref, wkc_ref, wv_ref) = refs[:14]
    refs = refs[14:]
    if latent:
        cosa_ref, sina_ref, cosc_ref, sinc_ref = refs[:4]
        refs = refs[4:]
        aqt_ref, ak_ref, avt_ref, bqt_ref, bk_ref, bvt_ref, qct_ref, kc_ref, vct_ref, z_ref = refs
    else:
        (aqt_ref, ak_ref, av_ref, avt_ref, bqt_ref, bk_ref, bv_ref, bvt_ref, qct_ref, kc_ref, vct_ref,
         ckvn_ref, ckr_ref, z_ref) = refs

    mod = mod_ref[0]
    shift1, scale1 = mod[:, 0:d], mod[:, d:2 * d]
    tm = x_ref.shape[0]
    n_parts = PRE_ROW_PARTS if tm % (PRE_ROW_PARTS * LANES) == 0 else 1
    parts = [slice(k * tm // n_parts, (k + 1) * tm // n_parts) for k in range(n_parts)]

    proj = []
    for rs in parts:
        h = (_rms(x_ref[rs, :], g1_ref[...]) * (1.0 + scale1) + shift1).astype(_MM)
        proj.append([_dot(h, w_ref[...]) for w_ref in (wa_ref, wb_ref, wcq_ref, wckv_ref, wckr_ref, wdu_ref)])

    for rs, (pa, pb, cq, ckv, kr, du) in zip(parts, proj):
        aq, ak, av = pa[:, 0:HEAD_W], pa[:, HEAD_W:2 * HEAD_W], pa[:, 2 * HEAD_W:3 * HEAD_W]
        if latent:
            cosa, sina = cosa_ref[rs, :], sina_ref[rs, :]
            aq = _rope(aq, cosa, sina)
            ak = _rope(ak, cosa, sina)
        else:
            av_ref[rs, :] = av
        aqt_ref[:, rs] = (aq * (sa * LOG2E)).T.astype(aqt_ref.dtype)
        ak_ref[rs, :] = ak.astype(ak_ref.dtype)
        avt_ref[:, rs] = av.T.astype(avt_ref.dtype)

        bq, bk, bv = pb[:, 0:HEAD_W], pb[:, HEAD_W:2 * HEAD_W], pb[:, 2 * HEAD_W:3 * HEAD_W]
        bk_ref[rs, :] = bk.astype(bk_ref.dtype)
        if not latent:
            bv_ref[rs, :] = bv
        bqt_ref[:, rs] = (bq * (sb * LOG2E)).T.astype(bqt_ref.dtype)
        bvt_ref[:, rs] = bv.T.astype(bvt_ref.dtype)

        qc = _dot(_rms(cq, cqg_ref[...]), wuq_ref[...])
        ckvn = _rms(ckv, ckvg_ref[...])
        if latent:
            cosc, sinc = cosc_ref[rs, :], sinc_ref[rs, :]
            qc = _rope(qc, cosc, sinc)
            kr = _rope(kr, cosc, sinc)
        else:
            ckvn_ref[rs, :] = ckvn
            ckr_ref[rs, :] = kr[:, NOPE_C:NOPE_C + ROPE_C]
        qct_ref[:, rs] = (qc * (sc * LOG2E)).T.astype(qct_ref.dtype)
        kc_ref[rs, :] = (_dot(ckvn, wkc_ref[...]) + kr).astype(kc_ref.dtype)
        vct_ref[:, rs] = _dot(ckvn, wv_ref[...]).T.astype(vct_ref.dtype)
        z_ref[rs, :] = du[:, 0:CONV_CH] * _sigmoid(du[:, CONV_CH:2 * CONV_CH])


def _pre(x2d, mods3, mod_row, lw, *, latent, seq, tables):
    t, d = x2d.shape
    tm = min(PRE_TILE, seq)
    n_seq_tiles = seq // tm
    grid = (t // tm,)
    row = lambda i: (i, 0)
    const = lambda i: (0, 0)
    wnames = ("wa", "wb", "wcq", "wckv", "wckr", "wdu", "cqg", "ckvg", "wuq", "wkc", "wv")
    in_specs = [pl.BlockSpec((tm, d), row),
                pl.BlockSpec((1, 1, 6 * d), lambda i: (mod_row(i * tm), 0, 0)),
                pl.BlockSpec((1, d), const)]
    args = [x2d, mods3, lw["g1"]]
    for n in wnames:
        in_specs.append(pl.BlockSpec(lw[n].shape, const))
        args.append(lw[n])
    if latent:
        for tb in tables:
            in_specs.append(pl.BlockSpec((tm, HEAD_W), lambda i: (i % n_seq_tiles, 0)))
            args.append(tb)
    nat = lambda dt: (jax.ShapeDtypeStruct((t, HEAD_W), dt), pl.BlockSpec((tm, HEAD_W), row))
    tr = (jax.ShapeDtypeStruct((t // seq, HEAD_W, seq), _MM),
          pl.BlockSpec((None, HEAD_W, tm), lambda i: (i // n_seq_tiles, 0, i % n_seq_tiles)))
    if latent:
        outs = [tr, nat(_MM), tr, tr, nat(_MM), tr, tr, nat(_MM), tr]
    else:
        outs = [tr, nat(_F32), nat(_F32), tr, tr, nat(_F32), nat(_F32), tr, tr, nat(_MM), tr,
                (jax.ShapeDtypeStruct((t, KV_LORA), _F32), pl.BlockSpec((tm, KV_LORA), row)),
                (jax.ShapeDtypeStruct((t, ROPE_C), _F32), pl.BlockSpec((tm, ROPE_C), row))]
    outs.append((jax.ShapeDtypeStruct((t, CONV_CH), _F32), pl.BlockSpec((tm, CONV_CH), row)))
    out_shape = [o[0] for o in outs]
    out_specs = [o[1] for o in outs]
    kern = functools.partial(_pre_kernel, d=d, latent=latent, sa=DH_A ** -0.5, sb=DH_B ** -0.5,
                             sc=(NOPE_C + ROPE_C) ** -0.5)
    return pl.pallas_call(kern, grid=grid, in_specs=in_specs, out_specs=out_specs, out_shape=out_shape,
                          compiler_params=_cparams(("arbitrary",), 48),
                          name="pre_lat" if latent else "pre_ctx")(*args)


SCORE_LOOKAHEAD = 5
SUM_ROWS = 16
MAX_UNROLLED_KEY_TILES = 16


def _softmax_init(qt_ref, qz_sc, m_sc, acc_sc, n_chains, dqk, n_qblocks=1):
    tq = qz_sc.shape[-1]
    qz_sc[...] = jnp.zeros(qz_sc.shape, qz_sc.dtype)
    for qb in range(n_qblocks):
        for c in range(n_chains):
            qz_sc[qb * n_chains + c, (c * dqk) % LANES:(c * dqk) % LANES + dqk, :] = (
                qt_ref[c * dqk:(c + 1) * dqk, qb * tq:(qb + 1) * tq])
    m_sc[...] = jnp.full(m_sc.shape, -jnp.inf, _F32)
    acc_sc[...] = jnp.zeros(acc_sc.shape, _F32)


def _softmax_steps(tiles, *, qz_sc, m_sc, acc_sc, n_chains, n_maps, dqk, n_qblocks=1):
    steps = [(ti, c) for ti in range(len(tiles)) for c in range(n_qblocks * n_chains)]
    kts, vts = {}, {}

    def scores(ti, c):
        if ti not in kts:
            kts[ti] = tiles[ti][0]().astype(_MM)
        grp = ((c % n_chains) * dqk) // LANES
        s = jnp.dot(kts[ti][:, grp * LANES:(grp + 1) * LANES], qz_sc[c], preferred_element_type=_F32)
        return s if tiles[ti][2] is None else s + tiles[ti][2](c % n_chains)

    def values(ti, h):
        if (ti, h) not in vts:
            vt = tiles[ti][1](h).astype(_MM)
            vts[ti, h] = jnp.concatenate([vt, jnp.ones((SUM_ROWS, vt.shape[1]), _MM)], axis=0)
        return vts[ti, h]

    pending = [scores(*st) for st in steps[:SCORE_LOOKAHEAD]]
    for idx, (ti, c) in enumerate(steps):
        s = pending.pop(0)
        if idx + SCORE_LOOKAHEAD < len(steps):
            pending.append(scores(*steps[idx + SCORE_LOOKAHEAD]))
        m_prev = m_sc[c]
        m_new = jnp.maximum(m_prev, jnp.max(s, axis=0, keepdims=True))
        alpha = jnp.exp2(m_prev - m_new)
        p = jnp.exp2(s - m_new).astype(_MM)
        acc_sc[c] = alpha * acc_sc[c] + jnp.dot(values(ti, (c % n_chains) // n_maps), p, preferred_element_type=_F32)
        m_sc[c] = m_new


def _attn_kernel(*refs, n_heads, n_maps, dqk, dv, tk, n_own, unroll, tkc, n_cache, diff, lam_init):
    qt_ref, k_ref, vt_ref = refs[:3]
    refs = refs[3:]
    if n_cache:
        kc_ref, vct_ref = refs[:2]
        refs = refs[2:]
    if diff:
        alam_ref, subg_ref = refs[:2]
        refs = refs[2:]
    o_ref, qz_sc, m_sc, acc_sc, ot_sc = refs
    n_chains = n_heads * n_maps
    tq = qz_sc.shape[-1]
    n_qblocks = qt_ref.shape[-1] // tq
    _softmax_init(qt_ref, qz_sc, m_sc, acc_sc, n_chains, dqk, n_qblocks)
    update = functools.partial(_softmax_steps, qz_sc=qz_sc, m_sc=m_sc, acc_sc=acc_sc, n_chains=n_chains,
                               n_maps=n_maps, dqk=dqk, n_qblocks=n_qblocks)

    cache_tiles = [(lambda j=j: kc_ref[j * tkc:(j + 1) * tkc, :],
                    lambda h, j=j: vct_ref[h * dv:(h + 1) * dv, j * tkc:(j + 1) * tkc], None) for j in range(n_cache)]

    def own_tiles(starts):
        return [(lambda st=st: k_ref[pl.ds(st, tk), :],
                 lambda h, st=st: vt_ref[h * dv:(h + 1) * dv, pl.ds(st, tk)], None) for st in starts]

    if unroll == n_own:
        update(cache_tiles + own_tiles([u * tk for u in range(n_own)]))
    else:
        if n_cache:
            update(cache_tiles)

        def body(j, carry):
            update(own_tiles([pl.multiple_of((j * unroll + u) * tk, tk) for u in range(unroll)]))
            return carry

        lax.fori_loop(0, n_own // unroll, body, 0)

    def normalised(c):
        a = acc_sc[c]
        return a[0:dv] / a[dv:dv + 1]

    if diff:
        lv = alam_ref[...]
        lam = (jnp.exp(jnp.sum(lv[0:1] * lv[1:2], axis=-1, keepdims=True))
               - jnp.exp(jnp.sum(lv[2:3] * lv[3:4], axis=-1, keepdims=True)) + lam_init)
    for qb in range(n_qblocks):
        base = qb * n_chains
        for h in range(n_heads):
            if diff:
                o = normalised(base + 2 * h) - lam * normalised(base + 2 * h + 1)
                o = o * lax.rsqrt(jnp.mean(o * o, axis=0, keepdims=True) + EPS) * subg_ref[...] * (1.0 - lam_init)
            else:
                o = normalised(base + h)
            ot_sc[h * dv:(h + 1) * dv, :] = o
        o_ref[qb * tq:(qb + 1) * tq, :] = ot_sc[...].T.astype(o_ref.dtype)


def _attn(qt, k, vt, cache, diff_args, *, n_heads, n_maps, dqk, dv, lam_init=0.0, name):
    nb, s, _ = k.shape
    tq = min(ATTN_TILE, s)
    tk = min(ATTN_TILE, s)
    n_own = s // tk
    n_qblocks = ATTN_QUERY_BLOCKS if s % (ATTN_QUERY_BLOCKS * tq) == 0 else 1
    tqs = n_qblocks * tq
    qmap = lambda b, i: (b, 0, i)
    bmap = lambda b, i: (b, 0, 0)
    in_specs = [pl.BlockSpec((None, HEAD_W, tqs), qmap),
                pl.BlockSpec((None, s, HEAD_W), bmap),
                pl.BlockSpec((None, HEAD_W, s), bmap)]
    args = [qt, k, vt]
    tkc, n_cache = tk, 0
    if cache is not None:
        past = cache[0].shape[1]
        tkc = min(ATTN_TILE, past)
        n_cache = past // tkc
        in_specs += [pl.BlockSpec((None, past, HEAD_W), bmap), pl.BlockSpec((None, HEAD_W, past), bmap)]
        args += list(cache)
    if diff_args is not None:
        in_specs += [pl.BlockSpec(diff_args[0].shape, lambda b, i: (0, 0)),
                     pl.BlockSpec(diff_args[1].shape, lambda b, i: (0, 0))]
        args += list(diff_args)
    n_chains = n_heads * n_maps
    unroll = n_own if n_own <= MAX_UNROLLED_KEY_TILES else (8 if n_own % 8 == 0 else 1)
    kern = functools.partial(_attn_kernel, n_heads=n_heads, n_maps=n_maps, dqk=dqk, dv=dv, tk=tk, n_own=n_own,
                             unroll=unroll, tkc=tkc, n_cache=n_cache, diff=diff_args is not None, lam_init=lam_init)
    return pl.pallas_call(
        kern, grid=(nb, s // tqs), in_specs=in_specs,
        out_specs=pl.BlockSpec((None, tqs, HEAD_W), lambda b, i: (b, i, 0)),
        out_shape=jax.ShapeDtypeStruct((nb, s, HEAD_W), _MM),
        scratch_shapes=[pltpu.VMEM((n_qblocks * n_chains, LANES, tq), _MM),
                        pltpu.VMEM((n_qblocks * n_chains, 1, tq), _F32),
                        pltpu.VMEM((n_qblocks * n_chains, dv + SUM_ROWS, tq), _F32),
                        pltpu.VMEM((n_heads * dv, tq), _F32)],
        compiler_params=_cparams(("arbitrary", "arbitrary"), 48),
        name=name,
    )(*args)


NBR_ROWS = 4
NBR_SPAN_ROWS = 12


def _nbr_kernel(qt_ref, k_ref, vt_ref, kc_ref, vct_ref, tb_ref, o_ref, qz_sc, m_sc, acc_sc, ot_sc, *, rows, tkc, n_cache):
    i = pl.program_id(1)
    ss = jnp.clip(i * NBR_ROWS - WIN_R // 2, 0, rows - NBR_SPAN_ROWS)
    tk = NBR_ROWS * GRID_W
    kstart = pl.multiple_of(ss * GRID_W, tk)
    hsl = lambda h: slice(h * DH_B, (h + 1) * DH_B)

    _softmax_init(qt_ref, qz_sc, m_sc, acc_sc, H_B, DH_B)
    tiles = [(lambda j=j: kc_ref[j * tkc:(j + 1) * tkc, :],
              lambda h, j=j: vct_ref[hsl(h), j * tkc:(j + 1) * tkc], None) for j in range(n_cache)]
    starts = [pl.multiple_of(kstart + u * tk, tk) for u in range(NBR_SPAN_ROWS // NBR_ROWS)]
    tiles += [(lambda st=st: k_ref[pl.ds(st, tk), :],
               lambda h, st=st: vt_ref[hsl(h), pl.ds(st, tk)],
               lambda c, u=u: tb_ref[c, u * tk:(u + 1) * tk, :]) for u, st in enumerate(starts)]
    _softmax_steps(tiles, qz_sc=qz_sc, m_sc=m_sc, acc_sc=acc_sc, n_chains=H_B, n_maps=1, dqk=DH_B)
    for h in range(H_B):
        a = acc_sc[h]
        ot_sc[hsl(h), :] = a[0:DH_B] / a[DH_B:DH_B + 1]
    o_ref[...] = ot_sc[...].T.astype(o_ref.dtype)


def _nbr_bias_table(rpb, rows):
    nh, n_r, n_c = rpb.shape
    c = np.arange(GRID_W)[:, None]
    kc = np.arange(GRID_W)[None, :]
    cidx = np.clip(kc - c + WIN_C - 1, 0, n_c - 1).reshape(1, -1)
    onehot = jnp.asarray((cidx == np.arange(n_c)[:, None]).astype(np.float32))
    toep = jnp.dot(rpb.reshape(nh * n_r, n_c).astype(_F32), onehot, precision=lax.Precision.HIGHEST)
    toep = toep.reshape(nh, n_r, GRID_W, GRID_W)
    cstart = np.clip(c - WIN_C // 2, 0, GRID_W - WIN_C)
    valid = (kc >= cstart) & (kc < cstart + WIN_C)
    toep = jnp.where(valid[None, None], toep, -jnp.inf) * LOG2E
    assert rows % NBR_ROWS == 0 and rows >= NBR_SPAN_ROWS
    r0 = np.array([0, NBR_ROWS, rows - NBR_ROWS])[:, None, None]
    ss = np.clip(r0 - WIN_R // 2, 0, rows - NBR_SPAN_ROWS)
    key_row = ss + np.arange(NBR_SPAN_ROWS)[None, :, None]
    r = r0 + np.arange(NBR_ROWS)[None, None, :]
    rs = np.clip(r - WIN_R // 2, 0, rows - WIN_R)
    in_rows = (key_row >= rs) & (key_row < rs + WIN_R)
    ridx = np.clip(key_row - r + WIN_R - 1, 0, n_r - 1)
    tb = jnp.take(toep, jnp.asarray(ridx.reshape(-1)), axis=1)
    tb = tb.reshape((nh,) + ridx.shape + (GRID_W, GRID_W))
    tb = jnp.where(in_rows[None, :, :, :, None, None], tb, -jnp.inf)
    tb = tb.transpose(1, 0, 2, 5, 3, 4)
    return tb.reshape(3, nh, NBR_SPAN_ROWS * GRID_W, NBR_ROWS * GRID_W)


def _nbr(qt, k, vt, kc, vct, tb):
    nb, s, _ = k.shape
    rows = s // GRID_W
    n_steps = rows // NBR_ROWS
    past = kc.shape[1]
    tq = NBR_ROWS * GRID_W
    bmap = lambda b, i: (b, 0, 0)
    variant = lambda b, i: (jnp.where(i == 0, 0, jnp.where(i == n_steps - 1, 2, 1)), 0, 0, 0)
    tkc = min(ATTN_TILE, past)
    return pl.pallas_call(
        functools.partial(_nbr_kernel, rows=rows, tkc=tkc, n_cache=past // tkc), grid=(nb, n_steps),
        in_specs=[pl.BlockSpec((None, HEAD_W, tq), lambda b, i: (b, 0, i)),
                  pl.BlockSpec((None, s, HEAD_W), bmap),
                  pl.BlockSpec((None, HEAD_W, s), bmap),
                  pl.BlockSpec((None, past, HEAD_W), bmap),
                  pl.BlockSpec((None, HEAD_W, past), bmap),
                  pl.BlockSpec((None,) + tb.shape[1:], variant)],
        out_specs=pl.BlockSpec((None, tq, HEAD_W), lambda b, i: (b, i, 0)),
        out_shape=jax.ShapeDtypeStruct((nb, s, HEAD_W), _MM),
        scratch_shapes=[pltpu.VMEM((H_B, LANES, tq), _MM), pltpu.VMEM((H_B, 1, tq), _F32),
                        pltpu.VMEM((H_B, DH_B + SUM_ROWS, tq), _F32), pltpu.VMEM((HEAD_W, tq), _F32)],
        compiler_params=_cparams(("arbitrary", "arbitrary"), 48),
        name="nbr_attn",
    )(qt, k, vt, kc, vct, tb)


def _mla_cache_kernel(ckv_ref, kr_ref, wkc_ref, wvt_ref, e_ref, kc_ref, vct_ref):
    ckv = ckv_ref[...]
    kc_ref[...] = (_dot(ckv, wkc_ref[...]) + _dot(kr_ref[...], e_ref[...])).astype(kc_ref.dtype)
    vct_ref[...] = _dot_nt(wvt_ref[...], ckv).astype(vct_ref.dtype)


def _mla_cache(ckv, kr, wkc, wvt, expand):
    nb, past, _ = ckv.shape
    bmap = lambda b: (b, 0, 0)
    const = lambda b: (0, 0)
    return pl.pallas_call(
        _mla_cache_kernel, grid=(nb,),
        in_specs=[pl.BlockSpec((None, past, KV_LORA), bmap), pl.BlockSpec((None, past, ROPE_C), bmap),
                  pl.BlockSpec(wkc.shape, const), pl.BlockSpec(wvt.shape, const), pl.BlockSpec(expand.shape, const)],
        out_specs=[pl.BlockSpec((None, past, HEAD_W), bmap), pl.BlockSpec((None, HEAD_W, past), bmap)],
        out_shape=[jax.ShapeDtypeStruct((nb, past, HEAD_W), _MM), jax.ShapeDtypeStruct((nb, HEAD_W, past), _MM)],
        compiler_params=_cparams(("arbitrary",), 32),
        name="mla_cache",
    )(ckv, kr, wkc, wvt, expand)


CONV_HALO = 16


def _conv_rows(win, w, b, g, beta):
    n_win = win.shape[0]
    rc = n_win - 2 * CONV_HALO
    shift = CONV_HALO - CONV_K // 2
    acc = jnp.zeros((rc, CONV_CH), _F32)
    for phase in range(SUBLANES):
        rolled = win if phase == 0 else pltpu.roll(win, n_win - phase, axis=0)
        for j in range(CONV_K):
            if (j + shift) % SUBLANES == phase:
                a = (j + shift) - phase
                acc = acc + rolled[a:a + rc, :] * w[j:j + 1, :]
    y = acc + b
    mu = jnp.mean(y, axis=-1, keepdims=True)
    yc = y - mu
    var = jnp.mean(yc * yc, axis=-1, keepdims=True)
    yn = yc * lax.rsqrt(var + EPS) * g + beta
    return yn * _sigmoid(yn)


def _conv_kernel(z_ref, w_ref, b_ref, g_ref, beta_ref, o_ref, zp_ref, *, seq, rc):
    zp_ref[0:CONV_HALO, :] = jnp.zeros((CONV_HALO, CONV_CH), _F32)
    zp_ref[CONV_HALO:CONV_HALO + seq, :] = z_ref[...]
    zp_ref[CONV_HALO + seq:2 * CONV_HALO + seq, :] = jnp.zeros((CONV_HALO, CONV_CH), _F32)
    w = w_ref[...]

    def chunk(c, carry):
        base = pl.multiple_of(c * rc, rc)
        win = zp_ref[pl.ds(base, rc + 2 * CONV_HALO), :]
        o_ref[pl.ds(base, rc), :] = _conv_rows(win, w, b_ref[...], g_ref[...], beta_ref[...]).astype(o_ref.dtype)
        return carry

    lax.fori_loop(0, seq // rc, chunk, 0)


def _conv(z, w, b, g, beta):
    nb, seq, _ = z.shape
    rc = min(CONV_CHUNK, seq)
    bmap = lambda i: (i, 0, 0)
    const = lambda i: (0, 0)
    kern = functools.partial(_conv_kernel, seq=seq, rc=rc)
    return pl.pallas_call(
        kern, grid=(nb,),
        in_specs=[pl.BlockSpec((None, seq, CONV_CH), bmap), pl.BlockSpec(w.shape, const),
                  pl.BlockSpec(b.shape, const), pl.BlockSpec(g.shape, const), pl.BlockSpec(beta.shape, const)],
        out_specs=pl.BlockSpec((None, seq, CONV_CH), bmap),
        out_shape=jax.ShapeDtypeStruct((nb, seq, CONV_CH), _MM),
        scratch_shapes=[pltpu.VMEM((seq + 2 * CONV_HALO, CONV_CH), _F32)],
        compiler_params=_cparams(("arbitrary",), 40),
        name="conv",
    )(z, w, b, g, beta)


POST_ROW_PARTS = 2

def _post_kernel(*refs, d, conv_tiles_per_seq):
    x_ref, mod_ref, oa_ref, ob_ref, oc_ref = refs[:5]
    refs = refs[5:]
    if conv_tiles_per_seq:
        z_ref, cw_ref, cb_ref, lg_ref, lb_ref = refs[:5]
        refs = refs[5:]
    else:
        od_ref = refs[0]
        refs = refs[1:]
    (g1_ref, wg_ref, bg_ref, awo_ref, bwo_ref, cwo_ref, dwo_ref, wout_ref, g2_ref, rw_ref, rb_ref,
     x1_ref, h2_ref, dw_ref) = refs
    mod = mod_ref[0]
    shift1, scale1, gate1 = mod[:, 0:d], mod[:, d:2 * d], mod[:, 2 * d:3 * d]
    shift2, scale2 = mod[:, 3 * d:4 * d], mod[:, 4 * d:5 * d]
    tm = x_ref.shape[0]
    parts = [slice(k * tm // POST_ROW_PARTS, (k + 1) * tm // POST_ROW_PARTS) for k in range(POST_ROW_PARTS)]

    if conv_tiles_per_seq:
        seq = z_ref.shape[0]
        t0 = pl.multiple_of((pl.program_id(0) % conv_tiles_per_seq) * tm, tm)
        before = z_ref[pl.ds(pl.multiple_of(jnp.maximum(t0 - CONV_HALO, 0), CONV_HALO), CONV_HALO), :]
        after = z_ref[pl.ds(pl.multiple_of(jnp.minimum(t0 + tm, seq - CONV_HALO), CONV_HALO), CONV_HALO), :]
        win = jnp.concatenate([jnp.where(t0 > 0, before, 0.0), z_ref[pl.ds(t0, tm), :],
                               jnp.where(t0 + tm < seq, after, 0.0)], axis=0)
        cw = cw_ref[...]

        def conv_branch(rs):
            chunks = [_conv_rows(win[a:a + CONV_CHUNK + 2 * CONV_HALO, :], cw, cb_ref[...], lg_ref[...], lb_ref[...])
                      for a in range(rs.start, rs.stop, CONV_CHUNK)]
            return jnp.concatenate(chunks, axis=0)
    else:
        conv_branch = lambda rs: od_ref[rs, :]
    branches = ((lambda rs: oa_ref[rs, :], awo_ref), (lambda rs: ob_ref[rs, :], bwo_ref),
                (lambda rs: oc_ref[rs, :], cwo_ref), (conv_branch, dwo_ref))

    xs, branch_mm = [], []
    for rs in parts:
        x = x_ref[rs, :]
        h = (_rms(x, g1_ref[...]) * (1.0 + scale1) + shift1).astype(_MM)
        xs.append(x)
        branch_mm.append([(_dot(h, wg_ref[:, bi * d:(bi + 1) * d]), _dot(load(rs), wo_ref[...]))
                          for bi, (load, wo_ref) in enumerate(branches)])
    mixed = []
    for pairs in branch_mm:
        merged = None
        for bi, (gate_logits, out) in enumerate(pairs):
            term = _sigmoid(gate_logits + bg_ref[:, bi * d:(bi + 1) * d]) * out
            merged = term if merged is None else merged + term
        mixed.append(_dot(merged, wout_ref[...]))
    logits = []
    for rs, x, y in zip(parts, xs, mixed):
        x1 = x + gate1 * y
        x1_ref[rs, :] = x1
        h2_mm = (_rms(x1, g2_ref[...]) * (1.0 + scale2) + shift2).astype(_MM)
        h2_ref[rs, :] = h2_mm
        logits.append(_dot(h2_mm, rw_ref[...]) + rb_ref[...])
    for rs, r in zip(parts, logits):
        lgp, lep = r[:, 0:LANES], r[:, LANES:2 * LANES]
        lane = lax.broadcasted_iota(jnp.int32, lgp.shape, 1)
        neg = -jnp.inf
        lg = jnp.where(lane < N_GROUPS, lgp, neg)
        mg = jnp.max(lg, axis=-1, keepdims=True)
        gi = jnp.min(jnp.where(lg == mg, lane, LANES), axis=-1, keepdims=True)
        gw = 1.0 / jnp.sum(jnp.exp(lg - mg), axis=-1, keepdims=True)
        in_group = (lane < N_EXPERTS) & ((lane // EXPERTS_PER_GROUP) == gi)
        le = jnp.where(in_group, lep, neg)
        t1 = jnp.max(le, axis=-1, keepdims=True)
        i1 = jnp.min(jnp.where(le == t1, lane, LANES), axis=-1, keepdims=True)
        le2 = jnp.where(lane == i1, neg, le)
        t2 = jnp.max(le2, axis=-1, keepdims=True)
        i2 = jnp.min(jnp.where(le2 == t2, lane, LANES), axis=-1, keepdims=True)
        e2 = jnp.exp(t2 - t1)
        w1 = gw / (1.0 + e2)
        w2 = gw * e2 / (1.0 + e2)
        dw_ref[rs, :] = jnp.where(lane == i1, w1, 0.0) + jnp.where(lane == i2, w2, 0.0)


def _post(x2d, mods3, mod_row, oa, ob, oc, od, lw, *, z=None):
    t, d = x2d.shape
    tm = min(POST_TILE, t)
    row = lambda i: (i, 0)
    const = lambda i: (0, 0)
    wnames = ("g1", "wgate", "bgate", "awo", "bwo", "cwo", "dwo", "wout", "g2", "rw", "rb")
    in_specs = [pl.BlockSpec((tm, d), row), pl.BlockSpec((1, 1, 6 * d), lambda i: (mod_row(i * tm), 0, 0))]
    in_specs += [pl.BlockSpec((tm, HEAD_W), row)] * 3
    args = [x2d, mods3, oa, ob, oc]
    conv_tiles_per_seq = 0
    if z is not None:
        seq = z.shape[1]
        assert seq % tm == 0 and tm % (POST_ROW_PARTS * CONV_CHUNK) == 0
        conv_tiles_per_seq = seq // tm
        cnames = ("conv_w", "conv_b", "ln_g", "ln_b")
        in_specs += [pl.BlockSpec((None, seq, CONV_CH), lambda i: (i // conv_tiles_per_seq, 0, 0))]
        in_specs += [pl.BlockSpec(lw[n].shape, const) for n in cnames]
        args += [z] + [lw[n] for n in cnames]
    else:
        in_specs += [pl.BlockSpec((tm, HEAD_W), row)]
        args += [od]
    in_specs += [pl.BlockSpec(lw[n].shape, const) for n in wnames]
    args += [lw[n] for n in wnames]
    return pl.pallas_call(
        functools.partial(_post_kernel, d=d, conv_tiles_per_seq=conv_tiles_per_seq), grid=(t // tm,), in_specs=in_specs,
        out_specs=[pl.BlockSpec((tm, d), row), pl.BlockSpec((tm, d), row), pl.BlockSpec((tm, LANES), row)],
        out_shape=[jax.ShapeDtypeStruct((t, d), _F32), jax.ShapeDtypeStruct((t, d), _MM),
                   jax.ShapeDtypeStruct((t, LANES), _F32)],
        compiler_params=_cparams(("arbitrary",), 56),
        name="post",
    )(*args)


def _moe_kernel(h2_ref, dw_ref, x1_ref, mod_ref, w1_ref, w2_ref, fg_ref, o_ref, acc_ref, *, d, final):
    g = pl.program_id(1)

    @pl.when(g == 0)
    def _():
        acc_ref[...] = jnp.zeros_like(acc_ref)

    h2 = h2_ref[...]
    dw = dw_ref[...]
    lane = lax.broadcasted_iota(jnp.int32, dw.shape, 1)
    acts = []
    for j in range(EXPERTS_PER_GROUP):
        hid = _dot(h2, w1_ref[j])
        gate, up = hid[:, 0:D_EXPERT], hid[:, D_EXPERT:2 * D_EXPERT]
        we = jnp.sum(jnp.where(lane == g * EXPERTS_PER_GROUP + j, dw, 0.0), axis=-1, keepdims=True)
        acts.append((gate * _sigmoid(gate) * up * we).astype(_MM))
    acc_ref[...] += _dot(jnp.concatenate(acts, axis=-1), w2_ref[...])

    @pl.when(g == pl.num_programs(1) - 1)
    def _():
        gate2 = mod_ref[0][:, 5 * d:6 * d]
        x2 = x1_ref[...] + gate2 * acc_ref[...]
        if final:
            x2 = _rms(x2, fg_ref[...])
        o_ref[...] = x2


def _moe(h2, dw, x1, mods3, mod_row, w1, w2, fg, *, final):
    t, d = x1.shape
    tm = min(MOE_TILE, t)
    n_e, _, d_hid = w1.shape
    w2g = w2.reshape(N_GROUPS, EXPERTS_PER_GROUP * D_EXPERT, d)
    row = lambda i, g: (i, 0)
    return pl.pallas_call(
        functools.partial(_moe_kernel, d=d, final=final), grid=(t // tm, N_GROUPS),
        in_specs=[pl.BlockSpec((tm, d), row), pl.BlockSpec((tm, LANES), row), pl.BlockSpec((tm, d), row),
                  pl.BlockSpec((1, 1, 6 * d), lambda i, g: (mod_row(i * tm), 0, 0)),
                  pl.BlockSpec((EXPERTS_PER_GROUP, d, d_hid), lambda i, g: (g, 0, 0)),
                  pl.BlockSpec((None,) + w2g.shape[1:], lambda i, g: (g, 0, 0)),
                  pl.BlockSpec((1, d), lambda i, g: (0, 0))],
        out_specs=pl.BlockSpec((tm, d), row),
        out_shape=jax.ShapeDtypeStruct((t, d), _F32),
        scratch_shapes=[pltpu.VMEM((tm, d), _F32)],
        compiler_params=_cparams(("arbitrary", "arbitrary"), 56),
        name="moe_final" if final else "moe",
    )(h2, dw, x1, mods3, w1, w2g, fg)


def _rope_tables(n_tok, dim):
    hq = dim // 4
    tpos = np.arange(n_tok)
    rows = (tpos // GRID_W).astype(np.float32)
    cols = (tpos % GRID_W).astype(np.float32)
    inv = jnp.asarray(ROPE_THETA, _F32) ** (-jnp.arange(hq, dtype=_F32) / hq)
    ar = jnp.asarray(rows)[:, None] * inv[None, :]
    ac = jnp.asarray(cols)[:, None] * inv[None, :]
    ang = jnp.concatenate([ar, ar, ac, ac], axis=-1)
    return jnp.cos(ang), jnp.sin(ang)


def _layer_weights(l, p, grid_rows):
    mm = lambda a: a.astype(_MM)
    w_in = p["w_in"][l]
    pts = np.cumsum([0, 2 * H_A * DH_A, 2 * H_A * DH_A, 2 * H_A * DH_A, H_B * DH_B, H_B * DH_B, H_B * DH_B,
                     Q_LORA, KV_LORA, ROPE_C, 2 * CONV_CH])
    seg = lambda i, j: w_in[:, pts[i]:pts[j]]
    d = w_in.shape[0]
    wckr = seg(8, 9)
    zeros_n = jnp.zeros((d, NOPE_C), w_in.dtype)
    wckr_t = jnp.concatenate([zeros_n, wckr] * H_C, axis=1)
    wukv = p["c_wukv"][l].reshape(KV_LORA, H_C, NOPE_C + DV_C)
    wkc = jnp.concatenate([wukv[:, :, :NOPE_C], jnp.zeros((KV_LORA, H_C, ROPE_C), wukv.dtype)], axis=-1)
    wkc = wkc.reshape(KV_LORA, H_C * (NOPE_C + ROPE_C))
    wv = wukv[:, :, NOPE_C:].reshape(KV_LORA, H_C * DV_C)
    router = jnp.zeros((d, 2 * LANES), _F32)
    router = router.at[:, 0:N_GROUPS].set(p["moe_w_group"][l]).at[:, LANES:LANES + N_EXPERTS].set(p["moe_w_expert"][l])
    rb = jnp.zeros((1, 2 * LANES), _F32)
    rb = rb.at[0, 0:N_GROUPS].set(p["moe_b_group"][l]).at[0, LANES:LANES + N_EXPERTS].set(p["moe_b_expert"][l])
    conv_w = p["d_conv_w"][l]
    return dict(
        g1=p["norm1_g"][l][None], g2=p["norm2_g"][l][None],
        wa=mm(seg(0, 3)), wb=mm(seg(3, 6)), wcq=mm(seg(6, 7)), wckv=mm(seg(7, 8)), wckr=mm(wckr_t), wdu=mm(seg(9, 10)),
        cqg=p["c_q_norm_g"][l][None], ckvg=p["c_kv_norm_g"][l][None],
        wuq=mm(p["c_wuq"][l]), wkc=mm(wkc), wv=mm(wv), wvt=mm(wv.T),
        alam=p["a_lambda"][l], subg=p["a_subln_g"][l][:, None],
        wgate=mm(p["w_gate"][l]), bgate=p["b_gate"][l][None],
        awo=mm(p["a_wo"][l]), bwo=mm(p["b_wo"][l]), cwo=mm(p["c_wo"][l]), dwo=mm(p["d_wo"][l]),
        wout=mm(p["w_out"][l]), rw=mm(router), rb=rb,
        conv_w=conv_w, conv_b=p["d_conv_b"][l][None], ln_g=p["d_ln_g"][l][None], ln_b=p["d_ln_b"][l][None],
        w1=mm(p["moe_w1"][l]), w2=mm(p["moe_w2"][l]),
        tb=_nbr_bias_table(p["b_rpb"][l], grid_rows),
    )


def _layer(x2d, mods3, mod_row, lw, l, n_layers, *, nb, seq, fg, ctx, tables):
    latent = ctx is not None
    t = x2d.shape[0]
    outs = _pre(x2d, mods3, mod_row, lw, latent=latent, seq=seq, tables=tables)
    r3 = lambda a: a.reshape(nb, seq, a.shape[-1])
    lam_init = 0.8 - 0.6 * math.exp(-0.3 * l)
    diff_args = (lw["alam"], lw["subg"])
    if latent:
        aqt, ak, avt, bqt, bk, bvt, qct, kc, vct, z = outs
        own = None
        a_k_c, a_vt_c, b_k_c, b_vt_c, c_k_c, c_vt_c = ctx
        oa = _attn(aqt, r3(ak), avt, (a_k_c, a_vt_c), diff_args, n_heads=H_A, n_maps=2,
                   dqk=DH_A, dv=2 * DH_A, lam_init=lam_init, name="attn_a_lat")
        ob = _nbr(bqt, r3(bk), bvt, b_k_c, b_vt_c, lw["tb"])
        oc = _attn(qct, r3(kc), vct, (c_k_c, c_vt_c), None, n_heads=H_C, n_maps=1,
                   dqk=NOPE_C + ROPE_C, dv=DV_C, name="attn_c_lat")
    else:
        aqt, ak, av, avt, bqt, bk, bv, bvt, qct, kc, vct, ckvn, ckr, z = outs
        own = (ak, av, bk, bv, ckvn, ckr)
        oa = _attn(aqt, r3(ak), avt, None, diff_args, n_heads=H_A, n_maps=2,
                   dqk=DH_A, dv=2 * DH_A, lam_init=lam_init, name="attn_a_ctx")
        ob = _attn(bqt, r3(bk), bvt, None, None, n_heads=H_B, n_maps=1, dqk=DH_B, dv=DH_B, name="attn_b_ctx")
        oc = _attn(qct, r3(kc), vct, None, None, n_heads=H_C, n_maps=1,
                   dqk=NOPE_C + ROPE_C, dv=DV_C, name="attn_c_ctx")
    f2 = lambda a: a.reshape(t, a.shape[-1])
    if seq % min(POST_TILE, t) == 0:
        x1, h2, dw = _post(x2d, mods3, mod_row, f2(oa), f2(ob), f2(oc), None, lw, z=r3(z))
    else:
        od = _conv(r3(z), lw["conv_w"], lw["conv_b"], lw["ln_g"], lw["ln_b"])
        x1, h2, dw = _post(x2d, mods3, mod_row, f2(oa), f2(ob), f2(oc), f2(od), lw)
    x2 = _moe(h2, dw, x1, mods3, mod_row, lw["w1"], lw["w2"], fg, final=(l == n_layers - 1))
    return x2, own


def kernel(x_prompt, x_sample, cache_a_k, cache_a_v, cache_b_k, cache_b_v, cache_c_kv, cache_c_krope, c, c_ctx, norm1_g, norm2_g, w_ada, b_ada, w_in, a_lambda, a_subln_g, a_wo, b_rpb, b_wo, c_q_norm_g, c_kv_norm_g, c_wuq, c_wukv, c_wo, d_conv_w, d_conv_b, d_ln_g, d_ln_b, d_wo, w_gate, b_gate, w_out, moe_w_group, moe_b_group, moe_w_expert, moe_b_expert, moe_w1, moe_w2, final_norm_g):
    p = dict(norm1_g=norm1_g, norm2_g=norm2_g, w_in=w_in, a_lambda=a_lambda, a_subln_g=a_subln_g, a_wo=a_wo,
             b_rpb=b_rpb, b_wo=b_wo, c_q_norm_g=c_q_norm_g, c_kv_norm_g=c_kv_norm_g, c_wuq=c_wuq, c_wukv=c_wukv,
             c_wo=c_wo, d_conv_w=d_conv_w, d_conv_b=d_conv_b, d_ln_g=d_ln_g, d_ln_b=d_ln_b, d_wo=d_wo,
             w_gate=w_gate, b_gate=b_gate, w_out=w_out, moe_w_group=moe_w_group, moe_b_group=moe_b_group,
             moe_w_expert=moe_w_expert, moe_b_expert=moe_b_expert, moe_w1=moe_w1, moe_w2=moe_w2)
    n_layers = w_in.shape[0]
    bc, sc, d = x_prompt.shape
    bl, sl, _ = x_sample.shape
    past = cache_a_k.shape[2]
    assert bl + 1 <= COND_ROWS and sl % GRID_W == 0

    cond8 = jnp.zeros((COND_ROWS, d), _F32).at[0].set(c_ctx).at[1:1 + bl].set(c)
    mods3 = _adaln(cond8, w_ada, b_ada).reshape(n_layers * COND_ROWS, 1, 6 * d)
    fg = final_norm_g[None]

    cos_a, sin_a = _rope_tables(sl, DH_A)
    cos_c, sin_c = _rope_tables(sl, ROPE_C)
    reps_a = HEAD_W // DH_A
    ones_n, zeros_n = jnp.ones((sl, NOPE_C), _F32), jnp.zeros((sl, NOPE_C), _F32)
    tables = (jnp.tile(cos_a, (1, reps_a)), jnp.tile(sin_a, (1, reps_a)),
              jnp.concatenate([ones_n, cos_c] * H_C, axis=1), jnp.concatenate([zeros_n, sin_c] * H_C, axis=1))
    expand = np.zeros((ROPE_C, HEAD_W), np.float32)
    for hh in range(H_C):
        expand[np.arange(ROPE_C), hh * (NOPE_C + ROPE_C) + NOPE_C + np.arange(ROPE_C)] = 1.0
    expand = jnp.asarray(expand, _MM)

    lws = [_layer_weights(l, p, sl // GRID_W) for l in range(n_layers)]

    xp = x_prompt.reshape(bc * sc, d)
    states = []
    for l in range(n_layers):
        xp, own = _layer(xp, mods3, (lambda tok, l=l: l * COND_ROWS), lws[l], l, n_layers, nb=bc, seq=sc, fg=fg, ctx=None,
                         tables=None)
        states.append(own)
    y_prompt = xp.reshape(bc, sc, d)

    xs = x_sample.reshape(bl * sl, d)
    for l in range(n_layers):
        flat = lambda a: a[:, l].reshape(bl, past, -1)
        c_k_c, c_vt_c = _mla_cache(flat(cache_c_kv), flat(cache_c_krope), lws[l]["wkc"], lws[l]["wvt"], expand)
        ctx = (flat(cache_a_k).astype(_MM), jnp.swapaxes(flat(cache_a_v), 1, 2).astype(_MM),
               flat(cache_b_k).astype(_MM), jnp.swapaxes(flat(cache_b_v), 1, 2).astype(_MM), c_k_c, c_vt_c)
        xs, _ = _layer(xs, mods3, (lambda tok, l=l: l * COND_ROWS + 1 + tok // sl), lws[l], l, n_layers,
                       nb=bl, seq=sl, fg=fg, ctx=ctx, tables=tables)
    y_sample = xs.reshape(bl, sl, d)

    st = lambda k, shape: jnp.stack([s[k].reshape(shape) for s in states], axis=1)
    new_a_k = st(0, (bc, sc, H_A, 2 * DH_A))
    new_a_v = st(1, (bc, sc, H_A, 2 * DH_A))
    new_b_k = st(2, (bc, sc, H_B, DH_B))
    new_b_v = st(3, (bc, sc, H_B, DH_B))
    new_c_kv = st(4, (bc, sc, KV_LORA))
    new_c_krope = st(5, (bc, sc, ROPE_C))
    return (y_prompt, y_sample, new_a_k, new_a_v, new_b_k, new_b_v, new_c_kv, new_c_krope)
```

```python
import functools
import math

import numpy as np
import jax
import jax.numpy as jnp
from jax import lax
from jax.experimental import pallas as pl
from jax.experimental.pallas import tpu as pltpu

GRID_W = 64
EPS = 1e-6
ROPE_THETA = 10000.0
H_A, DH_A = 4, 32
H_B, DH_B = 4, 64
WIN_R, WIN_C = 8, 16
H_C, Q_LORA, KV_LORA, NOPE_C, ROPE_C, DV_C = 4, 256, 128, 32, 32, 64
CONV_CH, CONV_K = 256, 31
N_BRANCH = 4
N_GROUPS, EXPERTS_PER_GROUP, N_EXPERTS, D_EXPERT = 4, 4, 16, 256
HEAD_W = 256
LANES = 128
SUBLANES = 8
V7X_VMEM_BYTES = 64 * 1024 * 1024
LOG2E = math.log2(math.e)

PRE_TILE = 512
POST_TILE = 512
MOE_TILE = 1024
ATTN_TILE = 256
ATTN_QUERY_BLOCKS = 2
CONV_CHUNK = 128
ADALN_COLS = 1536
COND_ROWS = SUBLANES

_MM = jnp.bfloat16
_F32 = jnp.float32


def _cparams(sem, vmem_mb):
    return pltpu.CompilerParams(dimension_semantics=sem,
                                vmem_limit_bytes=min(vmem_mb * 1024 * 1024, V7X_VMEM_BYTES - 8 * 1024 * 1024))


def _dot(a, b):
    return jnp.dot(a.astype(_MM), b.astype(_MM), preferred_element_type=_F32)


def _dot_nt(a, b):
    return lax.dot_general(a.astype(_MM), b.astype(_MM), (((1,), (1,)), ((), ())),
                           preferred_element_type=_F32)


def _sigmoid(x):
    return 1.0 / (1.0 + jnp.exp2(x * (-LOG2E)))


def _sigmoid_exp(x):
    return 1.0 / (1.0 + jnp.exp(-x))


def _rms(x, g):
    return x * lax.rsqrt(jnp.mean(x * x, axis=-1, keepdims=True) + EPS) * g


def _rope(x, cos, sin):
    w = x.shape[-1]
    lane = lax.broadcasted_iota(jnp.int32, x.shape, 1)
    first = (lane % 16) < 8
    fwd = pltpu.roll(x, w - 8, axis=1)
    bwd = pltpu.roll(x, 8, axis=1)
    return x * cos + jnp.where(first, -fwd, bwd) * sin


def _adaln_kernel(cond_ref, w_ref, b_ref, o_ref):
    c = cond_ref[...]
    a = c * _sigmoid(c)
    o_ref[0] = _dot(a, w_ref[0]) + b_ref[0]


def _adaln(cond8, w_ada, b_ada):
    n_l, d, d6 = w_ada.shape
    tn = ADALN_COLS
    return pl.pallas_call(
        _adaln_kernel,
        grid=(n_l, d6 // tn),
        in_specs=[pl.BlockSpec((COND_ROWS, d), lambda l, j: (0, 0)),
                  pl.BlockSpec((1, d, tn), lambda l, j: (l, 0, j)),
                  pl.BlockSpec((1, 1, tn), lambda l, j: (l, 0, j))],
        out_specs=pl.BlockSpec((1, COND_ROWS, tn), lambda l, j: (l, 0, j)),
        out_shape=jax.ShapeDtypeStruct((n_l, COND_ROWS, d6), _F32),
        compiler_params=_cparams(("arbitrary", "arbitrary"), 40),
        name="adaln",
    )(cond8, w_ada, b_ada.reshape(n_l, 1, d6))


PRE_ROW_PARTS = 2


def _pre_kernel(*refs, d, latent, sa, sb, sc):
    (x_ref, mod_ref, g1_ref, wa_ref, wb_ref, wcq_ref, wckv_ref, wckr_ref, wdu_ref,
     cqg_ref, ckvg_ref, wuq_ref, wkc_ref, wv_ref) = refs[:14]
    refs = refs[14:]
    if latent:
        cosa_ref, sina_ref, cosc_ref, sinc_ref = refs[:4]
        refs = refs[4:]
        aqt_ref, ak_ref, avt_ref, bqt_ref, bk_ref, bvt_ref, qct_ref, kc_ref, vct_ref, z_ref = refs
    else:
        (aqt_ref, ak_ref, av_ref, avt_ref, bqt_ref, bk_ref, bv_ref, bvt_ref, qct_ref, kc_ref, vct_ref,
         ckvn_ref, ckr_ref, z_ref) = refs

    mod = mod_ref[0]
    shift1, scale1 = mod[:, 0:d], mod[:, d:2 * d]
    tm = x_ref.shape[0]
    n_parts = PRE_ROW_PARTS if tm % (PRE_ROW_PARTS * LANES) == 0 else 1
    parts = [slice(k * tm // n_parts, (k + 1) * tm // n_parts) for k in range(n_parts)]

    proj = []
    for rs in parts:
        h = (_rms(x_ref[rs, :], g1_ref[...]) * (1.0 + scale1) + shift1).astype(_MM)
        proj.append([_dot(h, w_ref[...]) for w_ref in (wa_ref, wb_ref, wcq_ref, wckv_ref, wckr_ref, wdu_ref)])

    for rs, (pa, pb, cq, ckv, kr, du) in zip(parts, proj):
        aq, ak, av = pa[:, 0:HEAD_W], pa[:, HEAD_W:2 * HEAD_W], pa[:, 2 * HEAD_W:3 * HEAD_W]
        if latent:
            cosa, sina = cosa_ref[rs, :], sina_ref[rs, :]
            aq = _rope(aq, cosa, sina)
            ak = _rope(ak, cosa, sina)
        else:
            av_ref[rs, :] = av
        aqt_ref[:, rs] = (aq * (sa * LOG2E)).T.astype(aqt_ref.dtype)
        ak_ref[rs, :] = ak.astype(ak_ref.dtype)
        avt_ref[:, rs] = av.T.astype(avt_ref.dtype)

        bq, bk, bv = pb[:, 0:HEAD_W], pb[:, HEAD_W:2 * HEAD_W], pb[:, 2 * HEAD_W:3 * HEAD_W]
        bk_ref[rs, :] = bk.astype(bk_ref.dtype)
        if not latent:
            bv_ref[rs, :] = bv
        bqt_ref[:, rs] = (bq * (sb * LOG2E)).T.astype(bqt_ref.dtype)
        bvt_ref[:, rs] = bv.T.astype(bvt_ref.dtype)

        qc = _dot(_rms(cq, cqg_ref[...]), wuq_ref[...])
        ckvn = _rms(ckv, ckvg_ref[...])
        if latent:
            cosc, sinc = cosc_ref[rs, :], sinc_ref[rs, :]
            qc = _rope(qc, cosc, sinc)
            kr = _rope(kr, cosc, sinc)
        else:
            ckvn_ref[rs, :] = ckvn
            ckr_ref[rs, :] = kr[:, NOPE_C:NOPE_C + ROPE_C]
        qct_ref[:, rs] = (qc * (sc * LOG2E)).T.astype(qct_ref.dtype)
        kc_ref[rs, :] = (_dot(ckvn, wkc_ref[...]) + kr).astype(kc_ref.dtype)
        vct_ref[:, rs] = _dot(ckvn, wv_ref[...]).T.astype(vct_ref.dtype)
        z_ref[rs, :] = du[:, 0:CONV_CH] * _sigmoid(du[:, CONV_CH:2 * CONV_CH])


def _pre(x2d, mods3, mod_row, lw, *, latent, seq, tables):
    t, d = x2d.shape
    tm = min(PRE_TILE, seq)
    n_seq_tiles = seq // tm
    grid = (t // tm,)
    row = lambda i: (i, 0)
    const = lambda i: (0, 0)
    wnames = ("wa", "wb", "wcq", "wckv", "wckr", "wdu", "cqg", "ckvg", "wuq", "wkc", "wv")
    in_specs = [pl.BlockSpec((tm, d), row),
                pl.BlockSpec((1, 1, 6 * d), lambda i: (mod_row(i * tm), 0, 0)),
                pl.BlockSpec((1, d), const)]
    args = [x2d, mods3, lw["g1"]]
    for n in wnames:
        in_specs.append(pl.BlockSpec(lw[n].shape, const))
        args.append(lw[n])
    if latent:
        for tb in tables:
            in_specs.append(pl.BlockSpec((tm, HEAD_W), lambda i: (i % n_seq_tiles, 0)))
            args.append(tb)
    nat = lambda dt: (jax.ShapeDtypeStruct((t, HEAD_W), dt), pl.BlockSpec((tm, HEAD_W), row))
    tr = (jax.ShapeDtypeStruct((t // seq, HEAD_W, seq), _MM),
          pl.BlockSpec((None, HEAD_W, tm), lambda i: (i // n_seq_tiles, 0, i % n_seq_tiles)))
    if latent:
        outs = [tr, nat(_MM), tr, tr, nat(_MM), tr, tr, nat(_MM), tr]
    else:
        outs = [tr, nat(_F32), nat(_F32), tr, tr, nat(_F32), nat(_F32), tr, tr, nat(_MM), tr,
                (jax.ShapeDtypeStruct((t, KV_LORA), _F32), pl.BlockSpec((tm, KV_LORA), row)),
                (jax.ShapeDtypeStruct((t, ROPE_C), _F32), pl.BlockSpec((tm, ROPE_C), row))]
    outs.append((jax.ShapeDtypeStruct((t, CONV_CH), _F32), pl.BlockSpec((tm, CONV_CH), row)))
    out_shape = [o[0] for o in outs]
    out_specs = [o[1] for o in outs]
    kern = functools.partial(_pre_kernel, d=d, latent=latent, sa=DH_A ** -0.5, sb=DH_B ** -0.5,
                             sc=(NOPE_C + ROPE_C) ** -0.5)
    return pl.pallas_call(kern, grid=grid, in_specs=in_specs, out_specs=out_specs, out_shape=out_shape,
                          compiler_params=_cparams(("arbitrary",), 48),
                          name="pre_lat" if latent else "pre_ctx")(*args)


SCORE_LOOKAHEAD = 5
SUM_ROWS = 16
MAX_UNROLLED_KEY_TILES = 16


def _softmax_init(qt_ref, qz_sc, m_sc, acc_sc, n_chains, dqk, n_qblocks=1):
    tq = qz_sc.shape[-1]
    qz_sc[...] = jnp.zeros(qz_sc.shape, qz_sc.dtype)
    for qb in range(n_qblocks):
        for c in range(n_chains):
            qz_sc[qb * n_chains + c, (c * dqk) % LANES:(c * dqk) % LANES + dqk, :] = (
                qt_ref[c * dqk:(c + 1) * dqk, qb * tq:(qb + 1) * tq])
    m_sc[...] = jnp.full(m_sc.shape, -jnp.inf, _F32)
    acc_sc[...] = jnp.zeros(acc_sc.shape, _F32)


def _softmax_steps(tiles, *, qz_sc, m_sc, acc_sc, n_chains, n_maps, dqk, n_qblocks=1):
    steps = [(ti, c) for ti in range(len(tiles)) for c in range(n_qblocks * n_chains)]
    kts, vts = {}, {}

    def scores(ti, c):
        if ti not in kts:
            kts[ti] = tiles[ti][0]().astype(_MM)
        grp = ((c % n_chains) * dqk) // LANES
        s = jnp.dot(kts[ti][:, grp * LANES:(grp + 1) * LANES], qz_sc[c], preferred_element_type=_F32)
        return s if tiles[ti][2] is None else s + tiles[ti][2](c % n_chains)

    def values(ti, h):
        if (ti, h) not in vts:
            vt = tiles[ti][1](h).astype(_MM)
            vts[ti, h] = jnp.concatenate([vt, jnp.ones((SUM_ROWS, vt.shape[1]), _MM)], axis=0)
        return vts[ti, h]

    pending = [scores(*st) for st in steps[:SCORE_LOOKAHEAD]]
    for idx, (ti, c) in enumerate(steps):
        s = pending.pop(0)
        if idx + SCORE_LOOKAHEAD < len(steps):
            pending.append(scores(*steps[idx + SCORE_LOOKAHEAD]))
        m_prev = m_sc[c]
        m_new = jnp.maximum(m_prev, jnp.max(s, axis=0, keepdims=True))
        alpha = jnp.exp2(m_prev - m_new)
        p = jnp.exp2(s - m_new).astype(_MM)
        acc_sc[c] = alpha * acc_sc[c] + jnp.dot(values(ti, (c % n_chains) // n_maps), p, preferred_element_type=_F32)
        m_sc[c] = m_new


def _attn_kernel(*refs, n_heads, n_maps, dqk, dv, tk, n_own, unroll, tkc, n_cache, diff, lam_init):
    qt_ref, k_ref, vt_ref = refs[:3]
    refs = refs[3:]
    if n_cache:
        kc_ref, vct_ref = refs[:2]
        refs = refs[2:]
    if diff:
        alam_ref, subg_ref = refs[:2]
        refs = refs[2:]
    o_ref, qz_sc, m_sc, acc_sc, ot_sc = refs
    n_chains = n_heads * n_maps
    tq = qz_sc.shape[-1]
    n_qblocks = qt_ref.shape[-1] // tq
    _softmax_init(qt_ref, qz_sc, m_sc, acc_sc, n_chains, dqk, n_qblocks)
    update = functools.partial(_softmax_steps, qz_sc=qz_sc, m_sc=m_sc, acc_sc=acc_sc, n_chains=n_chains,
                               n_maps=n_maps, dqk=dqk, n_qblocks=n_qblocks)

    cache_tiles = [(lambda j=j: kc_ref[j * tkc:(j + 1) * tkc, :],
                    lambda h, j=j: vct_ref[h * dv:(h + 1) * dv, j * tkc:(j + 1) * tkc], None) for j in range(n_cache)]

    def own_tiles(starts):
        return [(lambda st=st: k_ref[pl.ds(st, tk), :],
                 lambda h, st=st: vt_ref[h * dv:(h + 1) * dv, pl.ds(st, tk)], None) for st in starts]

    if unroll == n_own:
        update(cache_tiles + own_tiles([u * tk for u in range(n_own)]))
    else:
        if n_cache:
            update(cache_tiles)

        def body(j, carry):
            update(own_tiles([pl.multiple_of((j * unroll + u) * tk, tk) for u in range(unroll)]))
            return carry

        lax.fori_loop(0, n_own // unroll, body, 0)

    def normalised(c):
        a = acc_sc[c]
        return a[0:dv] / a[dv:dv + 1]

    if diff:
        lv = alam_ref[...]
        lam = (jnp.exp(jnp.sum(lv[0:1] * lv[1:2], axis=-1, keepdims=True))
               - jnp.exp(jnp.sum(lv[2:3] * lv[3:4], axis=-1, keepdims=True)) + lam_init)
    for qb in range(n_qblocks):
        base = qb * n_chains
        for h in range(n_heads):
            if diff:
                o = normalised(base + 2 * h) - lam * normalised(base + 2 * h + 1)
                o = o * lax.rsqrt(jnp.mean(o * o, axis=0, keepdims=True) + EPS) * subg_ref[...] * (1.0 - lam_init)
            else:
                o = normalised(base + h)
            ot_sc[h * dv:(h + 1) * dv, :] = o
        o_ref[qb * tq:(qb + 1) * tq, :] = ot_sc[...].T.astype(o_ref.dtype)


def _attn(qt, k, vt, cache, diff_args, *, n_heads, n_maps, dqk, dv, lam_init=0.0, name):
    nb, s, _ = k.shape
    tq = min(ATTN_TILE, s)
    tk = min(ATTN_TILE, s)
    n_own = s // tk
    n_qblocks = ATTN_QUERY_BLOCKS if s % (ATTN_QUERY_BLOCKS * tq) == 0 else 1
    tqs = n_qblocks * tq
    qmap = lambda b, i: (b, 0, i)
    bmap = lambda b, i: (b, 0, 0)
    in_specs = [pl.BlockSpec((None, HEAD_W, tqs), qmap),
                pl.BlockSpec((None, s, HEAD_W), bmap),
                pl.BlockSpec((None, HEAD_W, s), bmap)]
    args = [qt, k, vt]
    tkc, n_cache = tk, 0
    if cache is not None:
        past = cache[0].shape[1]
        tkc = min(ATTN_TILE, past)
        n_cache = past // tkc
        in_specs += [pl.BlockSpec((None, past, HEAD_W), bmap), pl.BlockSpec((None, HEAD_W, past), bmap)]
        args += list(cache)
    if diff_args is not None:
        in_specs += [pl.BlockSpec(diff_args[0].shape, lambda b, i: (0, 0)),
                     pl.BlockSpec(diff_args[1].shape, lambda b, i: (0, 0))]
        args += list(diff_args)
    n_chains = n_heads * n_maps
    unroll = n_own if n_own <= MAX_UNROLLED_KEY_TILES else (8 if n_own % 8 == 0 else 1)
    kern = functools.partial(_attn_kernel, n_heads=n_heads, n_maps=n_maps, dqk=dqk, dv=dv, tk=tk, n_own=n_own,
                             unroll=unroll, tkc=tkc, n_cache=n_cache, diff=diff_args is not None, lam_init=lam_init)
    return pl.pallas_call(
        kern, grid=(nb, s // tqs), in_specs=in_specs,
        out_specs=pl.BlockSpec((None, tqs, HEAD_W), lambda b, i: (b, i, 0)),
        out_shape=jax.ShapeDtypeStruct((nb, s, HEAD_W), _MM),
        scratch_shapes=[pltpu.VMEM((n_qblocks * n_chains, LANES, tq), _MM),
                        pltpu.VMEM((n_qblocks * n_chains, 1, tq), _F32),
                        pltpu.VMEM((n_qblocks * n_chains, dv + SUM_ROWS, tq), _F32),
                        pltpu.VMEM((n_heads * dv, tq), _F32)],
        compiler_params=_cparams(("arbitrary", "arbitrary"), 48),
        name=name,
    )(*args)


NBR_ROWS = 4
NBR_SPAN_ROWS = 12


def _nbr_kernel(qt_ref, k_ref, vt_ref, kc_ref, vct_ref, tb_ref, o_ref, qz_sc, m_sc, acc_sc, ot_sc, *, rows, tkc, n_cache):
    i = pl.program_id(1)
    ss = jnp.clip(i * NBR_ROWS - WIN_R // 2, 0, rows - NBR_SPAN_ROWS)
    tk = NBR_ROWS * GRID_W
    kstart = pl.multiple_of(ss * GRID_W, tk)
    hsl = lambda h: slice(h * DH_B, (h + 1) * DH_B)

    _softmax_init(qt_ref, qz_sc, m_sc, acc_sc, H_B, DH_B)
    tiles = [(lambda j=j: kc_ref[j * tkc:(j + 1) * tkc, :],
              lambda h, j=j: vct_ref[hsl(h), j * tkc:(j + 1) * tkc], None) for j in range(n_cache)]
    starts = [pl.multiple_of(kstart + u * tk, tk) for u in range(NBR_SPAN_ROWS // NBR_ROWS)]
    tiles += [(lambda st=st: k_ref[pl.ds(st, tk), :],
               lambda h, st=st: vt_ref[hsl(h), pl.ds(st, tk)],
               lambda c, u=u: tb_ref[c, u * tk:(u + 1) * tk, :]) for u, st in enumerate(starts)]
    _softmax_steps(tiles, qz_sc=qz_sc, m_sc=m_sc, acc_sc=acc_sc, n_chains=H_B, n_maps=1, dqk=DH_B)
    for h in range(H_B):
        a = acc_sc[h]
        ot_sc[hsl(h), :] = a[0:DH_B] / a[DH_B:DH_B + 1]
    o_ref[...] = ot_sc[...].T.astype(o_ref.dtype)


def _nbr_bias_table(rpb, rows):
    nh, n_r, n_c = rpb.shape
    c = np.arange(GRID_W)[:, None]
    kc = np.arange(GRID_W)[None, :]
    cidx = np.clip(kc - c + WIN_C - 1, 0, n_c - 1).reshape(1, -1)
    onehot = jnp.asarray((cidx == np.arange(n_c)[:, None]).astype(np.float32))
    toep = jnp.dot(rpb.reshape(nh * n_r, n_c).astype(_F32), onehot, precision=lax.Precision.HIGHEST)
    toep = toep.reshape(nh, n_r, GRID_W, GRID_W)
    cstart = np.clip(c - WIN_C // 2, 0, GRID_W - WIN_C)
    valid = (kc >= cstart) & (kc < cstart + WIN_C)
    toep = jnp.where(valid[None, None], toep, -jnp.inf) * LOG2E
    assert rows % NBR_ROWS == 0 and rows >= NBR_SPAN_ROWS
    r0 = np.array([0, NBR_ROWS, rows - NBR_ROWS])[:, None, None]
    ss = np.clip(r0 - WIN_R // 2, 0, rows - NBR_SPAN_ROWS)
    key_row = ss + np.arange(NBR_SPAN_ROWS)[None, :, None]
    r = r0 + np.arange(NBR_ROWS)[None, None, :]
    rs = np.clip(r - WIN_R // 2, 0, rows - WIN_R)
    in_rows = (key_row >= rs) & (key_row < rs + WIN_R)
    ridx = np.clip(key_row - r + WIN_R - 1, 0, n_r - 1)
    tb = jnp.take(toep, jnp.asarray(ridx.reshape(-1)), axis=1)
    tb = tb.reshape((nh,) + ridx.shape + (GRID_W, GRID_W))
    tb = jnp.where(in_rows[None, :, :, :, None, None], tb, -jnp.inf)
    tb = tb.transpose(1, 0, 2, 5, 3, 4)
    return tb.reshape(3, nh, NBR_SPAN_ROWS * GRID_W, NBR_ROWS * GRID_W)


def _nbr(qt, k, vt, kc, vct, tb):
    nb, s, _ = k.shape
    rows = s // GRID_W
    n_steps = rows // NBR_ROWS
    past = kc.shape[1]
    tq = NBR_ROWS * GRID_W
    bmap = lambda b, i: (b, 0, 0)
    variant = lambda b, i: (jnp.where(i == 0, 0, jnp.where(i == n_steps - 1, 2, 1)), 0, 0, 0)
    tkc = min(ATTN_TILE, past)
    return pl.pallas_call(
        functools.partial(_nbr_kernel, rows=rows, tkc=tkc, n_cache=past // tkc), grid=(nb, n_steps),
        in_specs=[pl.BlockSpec((None, HEAD_W, tq), lambda b, i: (b, 0, i)),
                  pl.BlockSpec((None, s, HEAD_W), bmap),
                  pl.BlockSpec((None, HEAD_W, s), bmap),
                  pl.BlockSpec((None, past, HEAD_W), bmap),
                  pl.BlockSpec((None, HEAD_W, past), bmap),
                  pl.BlockSpec((None,) + tb.shape[1:], variant)],
        out_specs=pl.BlockSpec((None, tq, HEAD_W), lambda b, i: (b, i, 0)),
        out_shape=jax.ShapeDtypeStruct((nb, s, HEAD_W), _MM),
        scratch_shapes=[pltpu.VMEM((H_B, LANES, tq), _MM), pltpu.VMEM((H_B, 1, tq), _F32),
                        pltpu.VMEM((H_B, DH_B + SUM_ROWS, tq), _F32), pltpu.VMEM((HEAD_W, tq), _F32)],
        compiler_params=_cparams(("arbitrary", "arbitrary"), 48),
        name="nbr_attn",
    )(qt, k, vt, kc, vct, tb)


def _mla_cache_kernel(ckv_ref, kr_ref, wkc_ref, wvt_ref, e_ref, kc_ref, vct_ref):
    ckv = ckv_ref[...]
    kc_ref[...] = (_dot(ckv, wkc_ref[...]) + _dot(kr_ref[...], e_ref[...])).astype(kc_ref.dtype)
    vct_ref[...] = _dot_nt(wvt_ref[...], ckv).astype(vct_ref.dtype)


def _mla_cache(ckv, kr, wkc, wvt, expand):
    nb, past, _ = ckv.shape
    bmap = lambda b: (b, 0, 0)
    const = lambda b: (0, 0)
    return pl.pallas_call(
        _mla_cache_kernel, grid=(nb,),
        in_specs=[pl.BlockSpec((None, past, KV_LORA), bmap), pl.BlockSpec((None, past, ROPE_C), bmap),
                  pl.BlockSpec(wkc.shape, const), pl.BlockSpec(wvt.shape, const), pl.BlockSpec(expand.shape, const)],
        out_specs=[pl.BlockSpec((None, past, HEAD_W), bmap), pl.BlockSpec((None, HEAD_W, past), bmap)],
        out_shape=[jax.ShapeDtypeStruct((nb, past, HEAD_W), _MM), jax.ShapeDtypeStruct((nb, HEAD_W, past), _MM)],
        compiler_params=_cparams(("arbitrary",), 32),
        name="mla_cache",
    )(ckv, kr, wkc, wvt, expand)


CONV_HALO = 16


def _conv_rows(win, w, b, g, beta):
    n_win = win.shape[0]
    rc = n_win - 2 * CONV_HALO
    shift = CONV_HALO - CONV_K // 2
    acc = jnp.zeros((rc, CONV_CH), _F32)
    for phase in range(SUBLANES):
        rolled = win if phase == 0 else pltpu.roll(win, n_win - phase, axis=0)
        for j in range(CONV_K):
            if (j + shift) % SUBLANES == phase:
                a = (j + shift) - phase
                acc = acc + rolled[a:a + rc, :] * w[j:j + 1, :]
    y = acc + b
    mu = jnp.mean(y, axis=-1, keepdims=True)
    yc = y - mu
    var = jnp.mean(yc * yc, axis=-1, keepdims=True)
    yn = yc * lax.rsqrt(var + EPS) * g + beta
    return yn * _sigmoid(yn)


def _conv_kernel(z_ref, w_ref, b_ref, g_ref, beta_ref, o_ref, zp_ref, *, seq, rc):
    zp_ref[0:CONV_HALO, :] = jnp.zeros((CONV_HALO, CONV_CH), _F32)
    zp_ref[CONV_HALO:CONV_HALO + seq, :] = z_ref[...]
    zp_ref[CONV_HALO + seq:2 * CONV_HALO + seq, :] = jnp.zeros((CONV_HALO, CONV_CH), _F32)
    w = w_ref[...]

    def chunk(c, carry):
        base = pl.multiple_of(c * rc, rc)
        win = zp_ref[pl.ds(base, rc + 2 * CONV_HALO), :]
        o_ref[pl.ds(base, rc), :] = _conv_rows(win, w, b_ref[...], g_ref[...], beta_ref[...]).astype(o_ref.dtype)
        return carry

    lax.fori_loop(0, seq // rc, chunk, 0)


def _conv(z, w, b, g, beta):
    nb, seq, _ = z.shape
    rc = min(CONV_CHUNK, seq)
    bmap = lambda i: (i, 0, 0)
    const = lambda i: (0, 0)
    kern = functools.partial(_conv_kernel, seq=seq, rc=rc)
    return pl.pallas_call(
        kern, grid=(nb,),
        in_specs=[pl.BlockSpec((None, seq, CONV_CH), bmap), pl.BlockSpec(w.shape, const),
                  pl.BlockSpec(b.shape, const), pl.BlockSpec(g.shape, const), pl.BlockSpec(beta.shape, const)],
        out_specs=pl.BlockSpec((None, seq, CONV_CH), bmap),
        out_shape=jax.ShapeDtypeStruct((nb, seq, CONV_CH), _MM),
        scratch_shapes=[pltpu.VMEM((seq + 2 * CONV_HALO, CONV_CH), _F32)],
        compiler_params=_cparams(("arbitrary",), 40),
        name="conv",
    )(z, w, b, g, beta)


POST_ROW_PARTS = 2

def _post_kernel(*refs, d, conv_tiles_per_seq):
    x_ref, mod_ref, oa_ref, ob_ref, oc_ref = refs[:5]
    refs = refs[5:]
    if conv_tiles_per_seq:
        z_ref, cw_ref, cb_ref, lg_ref, lb_ref = refs[:5]
        refs = refs[5:]
    else:
        od_ref = refs[0]
        refs = refs[1:]
    (g1_ref, wg_ref, bg_ref, awo_ref, bwo_ref, cwo_ref, dwo_ref, wout_ref, g2_ref, rw_ref, rb_ref,
     x1_ref, h2_ref, dw_ref) = refs
    mod = mod_ref[0]
    shift1, scale1, gate1 = mod[:, 0:d], mod[:, d:2 * d], mod[:, 2 * d:3 * d]
    shift2, scale2 = mod[:, 3 * d:4 * d], mod[:, 4 * d:5 * d]
    tm = x_ref.shape[0]
    parts = [slice(k * tm // POST_ROW_PARTS, (k + 1) * tm // POST_ROW_PARTS) for k in range(POST_ROW_PARTS)]

    if conv_tiles_per_seq:
        seq = z_ref.shape[0]
        t0 = pl.multiple_of((pl.program_id(0) % conv_tiles_per_seq) * tm, tm)
        before = z_ref[pl.ds(pl.multiple_of(jnp.maximum(t0 - CONV_HALO, 0), CONV_HALO), CONV_HALO), :]
        after = z_ref[pl.ds(pl.multiple_of(jnp.minimum(t0 + tm, seq - CONV_HALO), CONV_HALO), CONV_HALO), :]
        win = jnp.concatenate([jnp.where(t0 > 0, before, 0.0), z_ref[pl.ds(t0, tm), :],
                               jnp.where(t0 + tm < seq, after, 0.0)], axis=0)
        cw = cw_ref[...]

        def conv_branch(rs):
            chunks = [_conv_rows(win[a:a + CONV_CHUNK + 2 * CONV_HALO, :], cw, cb_ref[...], lg_ref[...], lb_ref[...])
                      for a in range(rs.start, rs.stop, CONV_CHUNK)]
            return jnp.concatenate(chunks, axis=0)
    else:
        conv_branch = lambda rs: od_ref[rs, :]
    branches = ((lambda rs: oa_ref[rs, :], awo_ref), (lambda rs: ob_ref[rs, :], bwo_ref),
                (lambda rs: oc_ref[rs, :], cwo_ref), (conv_branch, dwo_ref))

    xs, branch_mm = [], []
    for rs in parts:
        x = x_ref[rs, :]
        h = (_rms(x, g1_ref[...]) * (1.0 + scale1) + shift1).astype(_MM)
        xs.append(x)
        branch_mm.append([(_dot(h, wg_ref[:, bi * d:(bi + 1) * d]), _dot(load(rs), wo_ref[...]))
                          for bi, (load, wo_ref) in enumerate(branches)])
    mixed = []
    for pairs in branch_mm:
        merged = None
        for bi, (gate_logits, out) in enumerate(pairs):
            term = _sigmoid(gate_logits + bg_ref[:, bi * d:(bi + 1) * d]) * out
            merged = term if merged is None else merged + term
        mixed.append(_dot(merged, wout_ref[...]))
    logits = []
    for rs, x, y in zip(parts, xs, mixed):
        x1 = x + gate1 * y
        x1_ref[rs, :] = x1
        h2_mm = (_rms(x1, g2_ref[...]) * (1.0 + scale2) + shift2).astype(_MM)
        h2_ref[rs, :] = h2_mm
        logits.append(_dot(h2_mm, rw_ref[...]) + rb_ref[...])
    for rs, r in zip(parts, logits):
        lgp, lep = r[:, 0:LANES], r[:, LANES:2 * LANES]
        lane = lax.broadcasted_iota(jnp.int32, lgp.shape, 1)
        neg = -jnp.inf
        lg = jnp.where(lane < N_GROUPS, lgp, neg)
        mg = jnp.max(lg, axis=-1, keepdims=True)
        gi = jnp.min(jnp.where(lg == mg, lane, LANES), axis=-1, keepdims=True)
        gw = 1.0 / jnp.sum(jnp.exp(lg - mg), axis=-1, keepdims=True)
        in_group = (lane < N_EXPERTS) & ((lane // EXPERTS_PER_GROUP) == gi)
        le = jnp.where(in_group, lep, neg)
        t1 = jnp.max(le, axis=-1, keepdims=True)
        i1 = jnp.min(jnp.where(le == t1, lane, LANES), axis=-1, keepdims=True)
        le2 = jnp.where(lane == i1, neg, le)
        t2 = jnp.max(le2, axis=-1, keepdims=True)
        i2 = jnp.min(jnp.where(le2 == t2, lane, LANES), axis=-1, keepdims=True)
        e2 = jnp.exp(t2 - t1)
        w1 = gw / (1.0 + e2)
        w2 = gw * e2 / (1.0 + e2)
        dw_ref[rs, :] = jnp.where(lane == i1, w1, 0.0) + jnp.where(lane == i2, w2, 0.0)


def _post(x2d, mods3, mod_row, oa, ob, oc, od, lw, *, z=None):
    t, d = x2d.shape
    tm = min(POST_TILE, t)
    row = lambda i: (i, 0)
    const = lambda i: (0, 0)
    wnames = ("g1", "wgate", "bgate", "awo", "bwo", "cwo", "dwo", "wout", "g2", "rw", "rb")
    in_specs = [pl.BlockSpec((tm, d), row), pl.BlockSpec((1, 1, 6 * d), lambda i: (mod_row(i * tm), 0, 0))]
    in_specs += [pl.BlockSpec((tm, HEAD_W), row)] * 3
    args = [x2d, mods3, oa, ob, oc]
    conv_tiles_per_seq = 0
    if z is not None:
        seq = z.shape[1]
        assert seq % tm == 0 and tm % (POST_ROW_PARTS * CONV_CHUNK) == 0
        conv_tiles_per_seq = seq // tm
        cnames = ("conv_w", "conv_b", "ln_g", "ln_b")
        in_specs += [pl.BlockSpec((None, seq, CONV_CH), lambda i: (i // conv_tiles_per_seq, 0, 0))]
        in_specs += [pl.BlockSpec(lw[n].shape, const) for n in cnames]
        args += [z] + [lw[n] for n in cnames]
    else:
        in_specs += [pl.BlockSpec((tm, HEAD_W), row)]
        args += [od]
    in_specs += [pl.BlockSpec(lw[n].shape, const) for n in wnames]
    args += [lw[n] for n in wnames]
    return pl.pallas_call(
        functools.partial(_post_kernel, d=d, conv_tiles_per_seq=conv_tiles_per_seq), grid=(t // tm,), in_specs=in_specs,
        out_specs=[pl.BlockSpec((tm, d), row), pl.BlockSpec((tm, d), row), pl.BlockSpec((tm, LANES), row)],
        out_shape=[jax.ShapeDtypeStruct((t, d), _F32), jax.ShapeDtypeStruct((t, d), _MM),
                   jax.ShapeDtypeStruct((t, LANES), _F32)],
        compiler_params=_cparams(("arbitrary",), 56),
        name="post",
    )(*args)


def _moe_kernel(h2_ref, dw_ref, x1_ref, mod_ref, w1_ref, w2_ref, fg_ref, o_ref, acc_ref, *, d, final):
    g = pl.program_id(1)

    @pl.when(g == 0)
    def _():
        acc_ref[...] = jnp.zeros_like(acc_ref)

    h2 = h2_ref[...]
    dw = dw_ref[...]
    lane = lax.broadcasted_iota(jnp.int32, dw.shape, 1)
    acts = []
    for j in range(EXPERTS_PER_GROUP):
        hid = _dot(h2, w1_ref[j])
        gate, up = hid[:, 0:D_EXPERT], hid[:, D_EXPERT:2 * D_EXPERT]
        we = jnp.sum(jnp.where(lane == g * EXPERTS_PER_GROUP + j, dw, 0.0), axis=-1, keepdims=True)
        acts.append((gate * _sigmoid_exp(gate) * up * we).astype(_MM))
    acc_ref[...] += _dot(jnp.concatenate(acts, axis=-1), w2_ref[...])

    @pl.when(g == pl.num_programs(1) - 1)
    def _():
        gate2 = mod_ref[0][:, 5 * d:6 * d]
        x2 = x1_ref[...] + gate2 * acc_ref[...]
        if final:
            x2 = _rms(x2, fg_ref[...])
        o_ref[...] = x2


def _moe(h2, dw, x1, mods3, mod_row, w1, w2, fg, *, final):
    t, d = x1.shape
    tm = min(MOE_TILE, t)
    n_e, _, d_hid = w1.shape
    w2g = w2.reshape(N_GROUPS, EXPERTS_PER_GROUP * D_EXPERT, d)
    row = lambda i, g: (i, 0)
    return pl.pallas_call(
        functools.partial(_moe_kernel, d=d, final=final), grid=(t // tm, N_GROUPS),
        in_specs=[pl.BlockSpec((tm, d), row), pl.BlockSpec((tm, LANES), row), pl.BlockSpec((tm, d), row),
                  pl.BlockSpec((1, 1, 6 * d), lambda i, g: (mod_row(i * tm), 0, 0)),
                  pl.BlockSpec((EXPERTS_PER_GROUP, d, d_hid), lambda i, g: (g, 0, 0)),
                  pl.BlockSpec((None,) + w2g.shape[1:], lambda i, g: (g, 0, 0)),
                  pl.BlockSpec((1, d), lambda i, g: (0, 0))],
        out_specs=pl.BlockSpec((tm, d), row),
        out_shape=jax.ShapeDtypeStruct((t, d), _F32),
        scratch_shapes=[pltpu.VMEM((tm, d), _F32)],
        compiler_params=_cparams(("arbitrary", "arbitrary"), 56),
        name="moe_final" if final else "moe",
    )(h2, dw, x1, mods3, w1, w2g, fg)


def _rope_tables(n_tok, dim):
    hq = dim // 4
    tpos = np.arange(n_tok)
    rows = (tpos // GRID_W).astype(np.float32)
    cols = (tpos % GRID_W).astype(np.float32)
    inv = jnp.asarray(ROPE_THETA, _F32) ** (-jnp.arange(hq, dtype=_F32) / hq)
    ar = jnp.asarray(rows)[:, None] * inv[None, :]
    ac = jnp.asarray(cols)[:, None] * inv[None, :]
    ang = jnp.concatenate([ar, ar, ac, ac], axis=-1)
    return jnp.cos(ang), jnp.sin(ang)


def _layer_weights(l, p, grid_rows):
    mm = lambda a: a.astype(_MM)
    w_in = p["w_in"][l]
    pts = np.cumsum([0, 2 * H_A * DH_A, 2 * H_A * DH_A, 2 * H_A * DH_A, H_B * DH_B, H_B * DH_B, H_B * DH_B,
                     Q_LORA, KV_LORA, ROPE_C, 2 * CONV_CH])
    seg = lambda i, j: w_in[:, pts[i]:pts[j]]
    d = w_in.shape[0]
    wckr = seg(8, 9)
    zeros_n = jnp.zeros((d, NOPE_C), w_in.dtype)
    wckr_t = jnp.concatenate([zeros_n, wckr] * H_C, axis=1)
    wukv = p["c_wukv"][l].reshape(KV_LORA, H_C, NOPE_C + DV_C)
    wkc = jnp.concatenate([wukv[:, :, :NOPE_C], jnp.zeros((KV_LORA, H_C, ROPE_C), wukv.dtype)], axis=-1)
    wkc = wkc.reshape(KV_LORA, H_C * (NOPE_C + ROPE_C))
    wv = wukv[:, :, NOPE_C:].reshape(KV_LORA, H_C * DV_C)
    router = jnp.zeros((d, 2 * LANES), _F32)
    router = router.at[:, 0:N_GROUPS].set(p["moe_w_group"][l]).at[:, LANES:LANES + N_EXPERTS].set(p["moe_w_expert"][l])
    rb = jnp.zeros((1, 2 * LANES), _F32)
    rb = rb.at[0, 0:N_GROUPS].set(p["moe_b_group"][l]).at[0, LANES:LANES + N_EXPERTS].set(p["moe_b_expert"][l])
    conv_w = p["d_conv_w"][l]
    return dict(
        g1=p["norm1_g"][l][None], g2=p["norm2_g"][l][None],
        wa=mm(seg(0, 3)), wb=mm(seg(3, 6)), wcq=mm(seg(6, 7)), wckv=mm(seg(7, 8)), wckr=mm(wckr_t), wdu=mm(seg(9, 10)),
        cqg=p["c_q_norm_g"][l][None], ckvg=p["c_kv_norm_g"][l][None],
        wuq=mm(p["c_wuq"][l]), wkc=mm(wkc), wv=mm(wv), wvt=mm(wv.T),
        alam=p["a_lambda"][l], subg=p["a_subln_g"][l][:, None],
        wgate=mm(p["w_gate"][l]), bgate=p["b_gate"][l][None],
        awo=mm(p["a_wo"][l]), bwo=mm(p["b_wo"][l]), cwo=mm(p["c_wo"][l]), dwo=mm(p["d_wo"][l]),
        wout=mm(p["w_out"][l]), rw=mm(router), rb=rb,
        conv_w=conv_w, conv_b=p["d_conv_b"][l][None], ln_g=p["d_ln_g"][l][None], ln_b=p["d_ln_b"][l][None],
        w1=mm(p["moe_w1"][l]), w2=mm(p["moe_w2"][l]),
        tb=_nbr_bias_table(p["b_rpb"][l], grid_rows),
    )


def _layer(x2d, mods3, mod_row, lw, l, n_layers, *, nb, seq, fg, ctx, tables):
    latent = ctx is not None
    t = x2d.shape[0]
    outs = _pre(x2d, mods3, mod_row, lw, latent=latent, seq=seq, tables=tables)
    r3 = lambda a: a.reshape(nb, seq, a.shape[-1])
    lam_init = 0.8 - 0.6 * math.exp(-0.3 * l)
    diff_args = (lw["alam"], lw["subg"])
    if latent:
        aqt, ak, avt, bqt, bk, bvt, qct, kc, vct, z = outs
        own = None
        a_k_c, a_vt_c, b_k_c, b_vt_c, c_k_c, c_vt_c = ctx
        oa = _attn(aqt, r3(ak), avt, (a_k_c, a_vt_c), diff_args, n_heads=H_A, n_maps=2,
                   dqk=DH_A, dv=2 * DH_A, lam_init=lam_init, name="attn_a_lat")
        ob = _nbr(bqt, r3(bk), bvt, b_k_c, b_vt_c, lw["tb"])
        oc = _attn(qct, r3(kc), vct, (c_k_c, c_vt_c), None, n_heads=H_C, n_maps=1,
                   dqk=NOPE_C + ROPE_C, dv=DV_C, name="attn_c_lat")
    else:
        aqt, ak, av, avt, bqt, bk, bv, bvt, qct, kc, vct, ckvn, ckr, z = outs
        own = (ak, av, bk, bv, ckvn, ckr)
        oa = _attn(aqt, r3(ak), avt, None, diff_args, n_heads=H_A, n_maps=2,
                   dqk=DH_A, dv=2 * DH_A, lam_init=lam_init, name="attn_a_ctx")
        ob = _attn(bqt, r3(bk), bvt, None, None, n_heads=H_B, n_maps=1, dqk=DH_B, dv=DH_B, name="attn_b_ctx")
        oc = _attn(qct, r3(kc), vct, None, None, n_heads=H_C, n_maps=1,
                   dqk=NOPE_C + ROPE_C, dv=DV_C, name="attn_c_ctx")
    f2 = lambda a: a.reshape(t, a.shape[-1])
    if seq % min(POST_TILE, t) == 0:
        x1, h2, dw = _post(x2d, mods3, mod_row, f2(oa), f2(ob), f2(oc), None, lw, z=r3(z))
    else:
        od = _conv(r3(z), lw["conv_w"], lw["conv_b"], lw["ln_g"], lw["ln_b"])
        x1, h2, dw = _post(x2d, mods3, mod_row, f2(oa), f2(ob), f2(oc), f2(od), lw)
    x2 = _moe(h2, dw, x1, mods3, mod_row, lw["w1"], lw["w2"], fg, final=(l == n_layers - 1))
    return x2, own


def kernel(x_prompt, x_sample, cache_a_k, cache_a_v, cache_b_k, cache_b_v, cache_c_kv, cache_c_krope, c, c_ctx, norm1_g, norm2_g, w_ada, b_ada, w_in, a_lambda, a_subln_g, a_wo, b_rpb, b_wo, c_q_norm_g, c_kv_norm_g, c_wuq, c_wukv, c_wo, d_conv_w, d_conv_b, d_ln_g, d_ln_b, d_wo, w_gate, b_gate, w_out, moe_w_group, moe_b_group, moe_w_expert, moe_b_expert, moe_w1, moe_w2, final_norm_g):
    p = dict(norm1_g=norm1_g, norm2_g=norm2_g, w_in=w_in, a_lambda=a_lambda, a_subln_g=a_subln_g, a_wo=a_wo,
             b_rpb=b_rpb, b_wo=b_wo, c_q_norm_g=c_q_norm_g, c_kv_norm_g=c_kv_norm_g, c_wuq=c_wuq, c_wukv=c_wukv,
             c_wo=c_wo, d_conv_w=d_conv_w, d_conv_b=d_conv_b, d_ln_g=d_ln_g, d_ln_b=d_ln_b, d_wo=d_wo,
             w_gate=w_gate, b_gate=b_gate, w_out=w_out, moe_w_group=moe_w_group, moe_b_group=moe_b_group,
             moe_w_expert=moe_w_expert, moe_b_expert=moe_b_expert, moe_w1=moe_w1, moe_w2=moe_w2)
    n_layers = w_in.shape[0]
    bc, sc, d = x_prompt.shape
    bl, sl, _ = x_sample.shape
    past = cache_a_k.shape[2]
    assert bl + 1 <= COND_ROWS and sl % GRID_W == 0

    cond8 = jnp.zeros((COND_ROWS, d), _F32).at[0].set(c_ctx).at[1:1 + bl].set(c)
    mods3 = _adaln(cond8, w_ada, b_ada).reshape(n_layers * COND_ROWS, 1, 6 * d)
    fg = final_norm_g[None]

    cos_a, sin_a = _rope_tables(sl, DH_A)
    cos_c, sin_c = _rope_tables(sl, ROPE_C)
    reps_a = HEAD_W // DH_A
    ones_n, zeros_n = jnp.ones((sl, NOPE_C), _F32), jnp.zeros((sl, NOPE_C), _F32)
    tables = (jnp.tile(cos_a, (1, reps_a)), jnp.tile(sin_a, (1, reps_a)),
              jnp.concatenate([ones_n, cos_c] * H_C, axis=1), jnp.concatenate([zeros_n, sin_c] * H_C, axis=1))
    expand = np.zeros((ROPE_C, HEAD_W), np.float32)
    for hh in range(H_C):
        expand[np.arange(ROPE_C), hh * (NOPE_C + ROPE_C) + NOPE_C + np.arange(ROPE_C)] = 1.0
    expand = jnp.asarray(expand, _MM)

    lws = [_layer_weights(l, p, sl // GRID_W) for l in range(n_layers)]

    xp = x_prompt.reshape(bc * sc, d)
    states = []
    for l in range(n_layers):
        xp, own = _layer(xp, mods3, (lambda tok, l=l: l * COND_ROWS), lws[l], l, n_layers, nb=bc, seq=sc, fg=fg, ctx=None,
                         tables=None)
        states.append(own)
    y_prompt = xp.reshape(bc, sc, d)

    xs = x_sample.reshape(bl * sl, d)
    for l in range(n_layers):
        flat = lambda a: a[:, l].reshape(bl, past, -1)
        c_k_c, c_vt_c = _mla_cache(flat(cache_c_kv), flat(cache_c_krope), lws[l]["wkc"], lws[l]["wvt"], expand)
        ctx = (flat(cache_a_k).astype(_MM), jnp.swapaxes(flat(cache_a_v), 1, 2).astype(_MM),
               flat(cache_b_k).astype(_MM), jnp.swapaxes(flat(cache_b_v), 1, 2).astype(_MM), c_k_c, c_vt_c)
        xs, _ = _layer(xs, mods3, (lambda tok, l=l: l * COND_ROWS + 1 + tok // sl), lws[l], l, n_layers,
                       nb=bl, seq=sl, fg=fg, ctx=ctx, tables=tables)
    y_sample = xs.reshape(bl, sl, d)

    st = lambda k, shape: jnp.stack([s[k].reshape(shape) for s in states], axis=1)
    new_a_k = st(0, (bc, sc, H_A, 2 * DH_A))
    new_a_v = st(1, (bc, sc, H_A, 2 * DH_A))
    new_b_k = st(2, (bc, sc, H_B, DH_B))
    new_b_v = st(3, (bc, sc, H_B, DH_B))
    new_c_kv = st(4, (bc, sc, KV_LORA))
    new_c_krope = st(5, (bc, sc, ROPE_C))
    return (y_prompt, y_sample, new_a_k, new_a_v, new_b_k, new_b_v, new_c_kv, new_c_krope)
```
